```python
import numpy as np
import jax
import jax.numpy as jnp
from jax import lax

D_MODEL = 2048
BATCH = 16
SEQ = 2048
DEPTH = 4
DEC_BATCH = 1
DEC_SEQ = 8192
PAST_LEN = 128

HEAD_DIM = 128
GRID_W = 64
ROPE_THETA = 10000.0
LN_EPS = 1e-5
RMS_EPS = 1e-6

GLA_HEADS = 4
GLA_DK = 128
GLA_DV = 256
GLA_RANK = 16
GLA_TAU = 16.0
GLA_CHUNK = 64

DIL_CONFIGS = ((128, 1), (512, 4), (2048, 16))
DIL_HEADS = 4

GQA_Q_HEADS = 8
GQA_KV_HEADS = 2
Q_BLOCK = 128

N_EXPERTS = 16
N_EXPERT_GROUPS = 4
EXPERTS_PER_GROUP = N_EXPERTS // N_EXPERT_GROUPS
TOP_K = 2
D_FF_EXPERT = 1024

DEEPNORM_ALPHA = (2.0 * DEPTH) ** 0.25
DEEPNORM_BETA = (8.0 * DEPTH) ** -0.25

A_QK = GLA_HEADS * GLA_DK
A_V = GLA_HEADS * GLA_DV
A_LR = 2 * GLA_RANK
B_HEADS = len(DIL_CONFIGS) * DIL_HEADS
B_QKV = 3 * B_HEADS * HEAD_DIM
B_OUT = DIL_HEADS * HEAD_DIM
C_Q = GQA_Q_HEADS * HEAD_DIM
C_KV = GQA_KV_HEADS * HEAD_DIM
GATE_COLS = 3 * D_MODEL
IN_SPLITS = (A_QK, A_QK, A_V, A_V, A_LR, B_QKV, C_Q, C_KV, C_KV, GATE_COLS)
IN_COLS = 2 * A_QK + 2 * A_V + A_LR + B_QKV + C_Q + 2 * C_KV + GATE_COLS

kernel_name = 'hybrid_gla_dilated_gqa_moe_encoder'


def layer_norm(x, g, b):
    xf = x.astype(jnp.float32)
    mu = jnp.mean(xf, axis=-1, keepdims=True)
    var = jnp.mean(jnp.square(xf - mu), axis=-1, keepdims=True)
    y = (xf - mu) * lax.rsqrt(var + LN_EPS) * g.astype(jnp.float32) + b.astype(jnp.float32)
    return y.astype(x.dtype)


def rms_norm(x, g):
    xf = x.astype(jnp.float32)
    ms = jnp.mean(jnp.square(xf), axis=-1, keepdims=True)
    return (xf * lax.rsqrt(ms + RMS_EPS) * g.astype(jnp.float32)).astype(x.dtype)


def rope_angles(pos, dim):
    inv_freq = ROPE_THETA ** (-jnp.arange(0, dim, 2, dtype=jnp.float32) / dim)
    ang = pos.astype(jnp.float32)[:, None] * inv_freq[None, :]
    return jnp.cos(ang), jnp.sin(ang)


def apply_rotary(x, cos, sin):
    x1, x2 = jnp.split(x.astype(jnp.float32), 2, axis=-1)
    c, s = cos[:, None, :], sin[:, None, :]
    return jnp.concatenate([x1 * c - x2 * s, x2 * c + x1 * s], axis=-1).astype(x.dtype)


def axial_angles(t):
    rows = t // GRID_W
    row_pos = jnp.broadcast_to(jnp.arange(rows)[:, None], (rows, GRID_W)).reshape(-1)
    col_pos = jnp.broadcast_to(jnp.arange(GRID_W)[None, :], (rows, GRID_W)).reshape(-1)
    half = HEAD_DIM // 2
    cr, sr = rope_angles(row_pos, half)
    cc, sc = rope_angles(col_pos, half)
    return cr, sr, cc, sc


def apply_axial(x, ang):
    cr, sr, cc, sc = ang
    half = HEAD_DIM // 2
    return jnp.concatenate([apply_rotary(x[..., :half], cr, sr),
                            apply_rotary(x[..., half:], cc, sc)], axis=-1)


def gla_chunked(q, k, v, g, strict):
    bsz, t, h, dk = q.shape
    dv = v.shape[-1]
    n = t // GLA_CHUNK

    def chunks(a):
        return a.astype(jnp.float32).reshape(bsz, n, GLA_CHUNK, h, a.shape[-1]).transpose(1, 0, 3, 2, 4)

    q, k, v, g = chunks(q), chunks(k), chunks(v), chunks(g)
    b = jnp.cumsum(g, axis=3)
    b_last = b[:, :, :, -1:, :]
    q_t = q * jnp.exp(b)
    k_t = k * jnp.exp(-b)
    k_s = k * jnp.exp(b_last - b)
    idx = jnp.arange(GLA_CHUNK)
    mask = (idx[:, None] > idx[None, :]) if strict else (idx[:, None] >= idx[None, :])
    scores = jnp.where(mask, jnp.einsum('nbhck,nbhsk->nbhcs', q_t, k_t), 0.0)
    o_intra = jnp.einsum('nbhcs,nbhsv->nbhcv', scores, v)

    def step(state, xs):
        qc, kc, vc, dec = xs
        o = jnp.einsum('bhck,bhkv->bhcv', qc, state)
        state = state * dec[:, :, 0, :, None] + jnp.einsum('bhck,bhcv->bhkv', kc, vc)
        return state, o

    s0 = jnp.zeros((bsz, h, dk, dv), jnp.float32)
    _, o_inter = lax.scan(step, s0, (q_t, k_s, v, jnp.exp(b_last)))
    o = o_intra + o_inter
    return o.transpose(1, 0, 3, 2, 4).reshape(bsz, t, h, dv)


def dilated_window_attention(q, k, v, dilation, half):
    bsz, t, h, e = q.shape
    n_sub = t // dilation
    nb = -(-n_sub // half)
    pad = nb * half - n_sub

    def to_blocks(a):
        a = a.reshape(bsz, n_sub, dilation, h, e).transpose(0, 2, 1, 3, 4)
        a = jnp.pad(a, ((0, 0), (0, 0), (0, pad), (0, 0), (0, 0)))
        return a.reshape(bsz, dilation, nb, half, h, e)

    def with_neighbours(a):
        ap = jnp.pad(a, ((0, 0), (0, 0), (1, 1), (0, 0), (0, 0), (0, 0)))
        return jnp.concatenate([ap[:, :, :-2], ap[:, :, 1:-1], ap[:, :, 2:]], axis=3)

    qb = to_blocks(q)
    kb = with_neighbours(to_blocks(k))
    vb = with_neighbours(to_blocks(v)).astype(jnp.float32)
    qpos = jnp.arange(nb * half).reshape(nb, half)
    kpos = (jnp.arange(nb)[:, None] - 1) * half + jnp.arange(3 * half)[None, :]
    valid = ((jnp.abs(qpos[:, :, None] - kpos[:, None, :]) <= half)
             & (kpos[:, None, :] >= 0) & (kpos[:, None, :] < n_sub))
    s = jnp.einsum('bdnqhe,bdnkhe->bdnhqk', qb, kb).astype(jnp.float32) * (e ** -0.5)
    s = jnp.where(valid[None, None, :, None], s, -jnp.inf)
    m = jnp.max(s, axis=-1, keepdims=True)
    p = jnp.exp(s - m)
    l = jnp.sum(p, axis=-1, keepdims=True)
    o = jnp.einsum('bdnhqk,bdnkhe->bdnhqe', p / l, vb)
    lse = (m + jnp.log(l))[..., 0]
    o = o.transpose(0, 1, 2, 4, 3, 5).reshape(bsz, dilation, nb * half, h, e)[:, :, :n_sub]
    o = o.transpose(0, 2, 1, 3, 4).reshape(bsz, t, h, e)
    lse = lse.transpose(0, 1, 2, 4, 3).reshape(bsz, dilation, nb * half, h)[:, :, :n_sub]
    lse = lse.transpose(0, 2, 1, 3).reshape(bsz, t, h)
    return o, lse


def gqa_attention(q, k, v):
    bsz, t, hq, e = q.shape
    hkv = k.shape[2]
    grp = hq // hkv
    nq = t // Q_BLOCK
    qb = q.reshape(bsz, nq, Q_BLOCK, hkv, grp, e).transpose(1, 0, 2, 3, 4, 5)
    vf = v.astype(jnp.float32)
    scale = e ** -0.5

    def one_block(qblk):
        s = jnp.einsum('bqhge,bkhe->bhgqk', qblk, k).astype(jnp.float32) * scale
        p = jax.nn.softmax(s, axis=-1)
        return jnp.einsum('bhgqk,bkhe->bqhge', p, vf)

    o = lax.map(one_block, qb)
    return o.transpose(1, 0, 2, 3, 4, 5).reshape(bsz, t, hq * e)


def token_mixer(x, w_in, gla_w2_f, gla_b_f, gla_w2_b, gla_b_b, gla_norm_g,
                q_norm_g, k_norm_g, w_up_a, w_up_b, w_up_c, w_out):
    bsz, t, _ = x.shape
    offsets = np.cumsum(IN_SPLITS)[:-1].tolist()
    (a_q, a_k, a_v, a_r, a_lr, b_qkv, c_q, c_k, c_v, gate_in) = jnp.split(x @ w_in, offsets, axis=-1)

    qa = a_q.reshape(bsz, t, GLA_HEADS, GLA_DK) * (GLA_DK ** -0.5)
    ka = a_k.reshape(bsz, t, GLA_HEADS, GLA_DK)
    va = a_v.reshape(bsz, t, GLA_HEADS, GLA_DV)
    g_f = jax.nn.log_sigmoid((a_lr[..., :GLA_RANK] @ gla_w2_f + gla_b_f).astype(jnp.float32)) / GLA_TAU
    g_b = jax.nn.log_sigmoid((a_lr[..., GLA_RANK:] @ gla_w2_b + gla_b_b).astype(jnp.float32)) / GLA_TAU
    g_f = g_f.reshape(bsz, t, GLA_HEADS, GLA_DK)
    g_b = g_b.reshape(bsz, t, GLA_HEADS, GLA_DK)
    rev = lambda a: jnp.flip(a, axis=1)
    o_fwd = gla_chunked(qa, ka, va, g_f, strict=False)
    o_bwd = rev(gla_chunked(rev(qa), rev(ka), rev(va), rev(g_b), strict=True))
    o_a = rms_norm(o_fwd + o_bwd, gla_norm_g.reshape(GLA_HEADS, GLA_DV)).reshape(bsz, t, A_V)
    o_a = o_a.astype(x.dtype) * jax.nn.silu(a_r)

    qkv_b = b_qkv.reshape(bsz, t, 3, B_HEADS, HEAD_DIM)
    cos, sin = rope_angles(jnp.arange(t), HEAD_DIM)
    q_b = apply_rotary(qkv_b[:, :, 0], cos, sin)
    k_b = apply_rotary(qkv_b[:, :, 1], cos, sin)
    v_b = qkv_b[:, :, 2]
    outs, lses = [], []
    for gi, (window, dilation) in enumerate(DIL_CONFIGS):
        hs = slice(gi * DIL_HEADS, (gi + 1) * DIL_HEADS)
        o_g, lse_g = dilated_window_attention(q_b[:, :, hs], k_b[:, :, hs], v_b[:, :, hs],
                                              dilation, window // (2 * dilation))
        outs.append(o_g)
        lses.append(lse_g)
    wts = jax.nn.softmax(jnp.stack(lses, axis=0), axis=0)
    o_b = jnp.sum(wts[..., None] * jnp.stack(outs, axis=0), axis=0).reshape(bsz, t, B_OUT).astype(x.dtype)

    q_c = rms_norm(c_q.reshape(bsz, t, GQA_Q_HEADS, HEAD_DIM), q_norm_g)
    k_c = rms_norm(c_k.reshape(bsz, t, GQA_KV_HEADS, HEAD_DIM), k_norm_g)
    v_c = c_v.reshape(bsz, t, GQA_KV_HEADS, HEAD_DIM)
    ang = axial_angles(t)
    o_c = gqa_attention(apply_axial(q_c, ang), apply_axial(k_c, ang), v_c).astype(x.dtype)

    g_a, g_bb, g_c = jnp.split(jax.nn.sigmoid(gate_in), 3, axis=-1)
    merged = g_a * (o_a @ w_up_a) + g_bb * (o_b @ w_up_b) + g_c * (o_c @ w_up_c)
    return merged @ w_out


def moe_ffn(x, w_router, router_bias, w_gate, w_up, w_down):
    bsz, t, d = x.shape
    xf = x.reshape(-1, d)
    n_tok = xf.shape[0]
    scores = jax.nn.sigmoid((xf @ w_router).astype(jnp.float32))
    sel = scores + router_bias.astype(jnp.float32)
    grp_top, _ = lax.top_k(sel.reshape(n_tok, N_EXPERT_GROUPS, EXPERTS_PER_GROUP), TOP_K)
    best_group = jnp.argmax(jnp.sum(grp_top, axis=-1), axis=-1)
    in_group = best_group[:, None] == (jnp.arange(N_EXPERTS) // EXPERTS_PER_GROUP)[None, :]
    _, idx = lax.top_k(jnp.where(in_group, sel, -jnp.inf), TOP_K)
    w_sel = jnp.take_along_axis(scores, idx, axis=-1)
    w_sel = w_sel / jnp.sum(w_sel, axis=-1, keepdims=True)
    gates = jnp.einsum('nk,nke->ne', w_sel, jax.nn.one_hot(idx, N_EXPERTS, dtype=jnp.float32))
    y = jnp.zeros((n_tok, d), jnp.float32)
    for e in range(N_EXPERTS):
        h = jax.nn.silu(xf @ w_gate[e]) * (xf @ w_up[e])
        y = y + gates[:, e:e + 1] * (h @ w_down[e])
    return y.reshape(bsz, t, d).astype(x.dtype)


def run_trunk(x, w_in, gla_w2_f, gla_b_f, gla_w2_b, gla_b_b, gla_norm_g, q_norm_g, k_norm_g,
              w_up_a, w_up_b, w_up_c, w_out, ln1_g, ln1_b, w_router, router_bias,
              moe_w_gate, moe_w_up, moe_w_down, ln2_g, ln2_b):
    for l in range(DEPTH):
        h = token_mixer(x, w_in[l], gla_w2_f[l], gla_b_f[l], gla_w2_b[l], gla_b_b[l], gla_norm_g[l],
                        q_norm_g[l], k_norm_g[l], w_up_a[l], w_up_b[l], w_up_c[l], w_out[l])
        x = layer_norm(DEEPNORM_ALPHA * x + h, ln1_g[l], ln1_b[l])
        h = moe_ffn(x, w_router, router_bias, moe_w_gate[l], moe_w_up[l], moe_w_down[l])
        x = layer_norm(DEEPNORM_ALPHA * x + h, ln2_g[l], ln2_b[l])
    return x


def setup_inputs(seed: int = 0) -> dict:
    key = jax.random.key(seed)
    ks = jax.random.split(key, 23)

    def nrm(k, shape, scale):
        return jax.random.normal(k, shape, jnp.float32) * scale

    return {
        'x_prompt': nrm(ks[0], (BATCH, SEQ, D_MODEL), 1.0),
        'x_sample': nrm(ks[1], (DEC_BATCH, DEC_SEQ, D_MODEL), 1.0),
        'w_in': nrm(ks[2], (DEPTH, D_MODEL, IN_COLS), D_MODEL ** -0.5),
        'gla_w2_f': nrm(ks[3], (DEPTH, GLA_RANK, A_QK), GLA_RANK ** -0.5),
        'gla_b_f': nrm(ks[4], (DEPTH, A_QK), 0.1),
        'gla_w2_b': nrm(ks[5], (DEPTH, GLA_RANK, A_QK), GLA_RANK ** -0.5),
        'gla_b_b': nrm(ks[6], (DEPTH, A_QK), 0.1),
        'gla_norm_g': 1.0 + nrm(ks[7], (DEPTH, A_V), 0.1),
        'q_norm_g': 1.0 + nrm(ks[8], (DEPTH, HEAD_DIM), 0.1),
        'k_norm_g': 1.0 + nrm(ks[9], (DEPTH, HEAD_DIM), 0.1),
        'w_up_a': nrm(ks[10], (DEPTH, A_V, D_MODEL), A_V ** -0.5),
        'w_up_b': nrm(ks[11], (DEPTH, B_OUT, D_MODEL), B_OUT ** -0.5),
        'w_up_c': nrm(ks[12], (DEPTH, C_Q, D_MODEL), C_Q ** -0.5),
        'w_out': nrm(ks[13], (DEPTH, D_MODEL, D_MODEL), DEEPNORM_BETA * D_MODEL ** -0.5),
        'ln1_g': 1.0 + nrm(ks[14], (DEPTH, D_MODEL), 0.1),
        'ln1_b': nrm(ks[15], (DEPTH, D_MODEL), 0.02),
        'w_router': nrm(ks[16], (D_MODEL, N_EXPERTS), D_MODEL ** -0.5),
        'router_bias': nrm(ks[17], (N_EXPERTS,), 0.01),
        'moe_w_gate': nrm(ks[18], (DEPTH, N_EXPERTS, D_MODEL, D_FF_EXPERT), D_MODEL ** -0.5),
        'moe_w_up': nrm(ks[19], (DEPTH, N_EXPERTS, D_MODEL, D_FF_EXPERT), D_MODEL ** -0.5),
        'moe_w_down': nrm(ks[20], (DEPTH, N_EXPERTS, D_FF_EXPERT, D_MODEL), DEEPNORM_BETA * D_FF_EXPERT ** -0.5),
        'ln2_g': 1.0 + nrm(ks[21], (DEPTH, D_MODEL), 0.1),
        'ln2_b': nrm(ks[22], (DEPTH, D_MODEL), 0.02),
    }


def reference(x_prompt, x_sample, w_in, gla_w2_f, gla_b_f, gla_w2_b, gla_b_b, gla_norm_g,
              q_norm_g, k_norm_g, w_up_a, w_up_b, w_up_c, w_out, ln1_g, ln1_b,
              w_router, router_bias, moe_w_gate, moe_w_up, moe_w_down, ln2_g, ln2_b):
    y_prompt = run_trunk(x_prompt, w_in, gla_w2_f, gla_b_f, gla_w2_b, gla_b_b, gla_norm_g, q_norm_g,
                         k_norm_g, w_up_a, w_up_b, w_up_c, w_out, ln1_g, ln1_b, w_router, router_bias,
                         moe_w_gate, moe_w_up, moe_w_down, ln2_g, ln2_b)
    y_sample = run_trunk(x_sample, w_in, gla_w2_f, gla_b_f, gla_w2_b, gla_b_b, gla_norm_g, q_norm_g,
                         k_norm_g, w_up_a, w_up_b, w_up_c, w_out, ln1_g, ln1_b, w_router, router_bias,
                         moe_w_gate, moe_w_up, moe_w_down, ln2_g, ln2_b)
    return (y_prompt, y_sample)
```

```python
import functools

import jax
import jax.numpy as jnp
from jax import lax
from jax.experimental import pallas as pl
from jax.experimental.pallas import tpu as pltpu

D_MODEL = 2048
DEPTH = 4
HEAD_DIM = 128
GRID_W = 64
ROPE_THETA = 10000.0
LN_EPS = 1e-5
RMS_EPS = 1e-6
GLA_HEADS = 4
GLA_DK = 128
GLA_DV = 256
GLA_RANK = 16
GLA_TAU = 16.0
GLA_CHUNK = 64
DIL_CONFIGS = ((128, 1), (512, 4), (2048, 16))
DIL_HEADS = 4
GQA_Q_HEADS = 8
GQA_KV_HEADS = 2
N_EXPERTS = 16
N_EXPERT_GROUPS = 4
EXPERTS_PER_GROUP = N_EXPERTS // N_EXPERT_GROUPS
D_FF_EXPERT = 1024
DEEPNORM_ALPHA = (2.0 * DEPTH) ** 0.25

A_QK = GLA_HEADS * GLA_DK
A_V = GLA_HEADS * GLA_DV
A_LR = 2 * GLA_RANK
B_HEADS = len(DIL_CONFIGS) * DIL_HEADS
B_QKV = 3 * B_HEADS * HEAD_DIM
B_OUT = DIL_HEADS * HEAD_DIM
C_Q = GQA_Q_HEADS * HEAD_DIM
C_KV = GQA_KV_HEADS * HEAD_DIM
GATE_COLS = 3 * D_MODEL
IN_SPLITS = (A_QK, A_QK, A_V, A_V, A_LR, B_QKV, C_Q, C_KV, C_KV, GATE_COLS)

LANES = 128
VMEM_LIMIT = 56 * 1024 * 1024
LR_PAD = LANES

OFF_GATE = 0
OFF_AQ = OFF_GATE + GATE_COLS
OFF_AK = OFF_AQ + A_QK
OFF_AV = OFF_AK + A_QK
OFF_AR = OFF_AV + A_V
OFF_CQ = OFF_AR + A_V
OFF_CK = OFF_CQ + C_Q
OFF_CV = OFF_CK + C_KV
OFF_LR = OFF_CV + C_KV
MAIN_COLS = OFF_LR + LR_PAD

TM_PROJ = 1024
TN_MAIN = 640
TN_B = 512
TM_TOK = 256
TM_MOE = 512
DIL_HALF = 64
NEG_BIG = -1e30

f32 = jnp.float32
bf16 = jnp.bfloat16


def _cparams(sem):
    return pltpu.CompilerParams(dimension_semantics=sem, vmem_limit_bytes=VMEM_LIMIT)


def _nt_dot(a, b):
    return lax.dot_general(a, b, (((1,), (1,)), ((), ())), preferred_element_type=f32)


def _tn_dot(a, b):
    return lax.dot_general(a, b, (((0,), (0,)), ((), ())), preferred_element_type=f32)


def _mm_kernel(x_ref, w_ref, o_ref):
    o_ref[...] = jnp.dot(x_ref[...], w_ref[...], preferred_element_type=f32).astype(o_ref.dtype)


def _proj_main(xb, w_main, layer):
    n, k = xb.shape
    grid = (MAIN_COLS // TN_MAIN, n // TM_PROJ)
    return pl.pallas_call(
        _mm_kernel,
        grid=grid,
        in_specs=[
            pl.BlockSpec((TM_PROJ, k), lambda j, i: (i, 0)),
            pl.BlockSpec((None, k, TN_MAIN), lambda j, i: (layer, 0, j)),
        ],
        out_specs=pl.BlockSpec((TM_PROJ, TN_MAIN), lambda j, i: (i, j)),
        out_shape=jax.ShapeDtypeStruct((n, MAIN_COLS), bf16),
        compiler_params=_cparams(("parallel", "parallel")),
    )(xb, w_main)


def _proj_b_kernel(x_ref, w_ref, cos_ref, sin_ref, o_ref):
    j = pl.program_id(0)
    heads_per_tile = TN_B // HEAD_DIM
    q_tiles = B_HEADS // heads_per_tile
    acc = jnp.dot(x_ref[...], w_ref[...], preferred_element_type=f32)
    cos = cos_ref[...]
    sin = sin_ref[...]
    is_rot = j < 2 * q_tiles
    scale = jnp.where(j < q_tiles, HEAD_DIM ** -0.5, 1.0).astype(f32)
    for h in range(heads_per_tile):
        a = acc[:, h * HEAD_DIM:(h + 1) * HEAD_DIM]
        rot = (a * cos + pltpu.roll(a, HEAD_DIM // 2, 1) * sin) * scale
        o_ref[h] = jnp.where(is_rot, rot, a).astype(o_ref.dtype)


def _proj_b(xb, w_b, cos_b, sin_b, layer, t):
    n, k = xb.shape
    tpb = t // TM_PROJ
    grid = (B_QKV // TN_B, n // TM_PROJ)
    hpt = TN_B // HEAD_DIM
    return pl.pallas_call(
        _proj_b_kernel,
        grid=grid,
        in_specs=[
            pl.BlockSpec((TM_PROJ, k), lambda j, i: (i, 0)),
            pl.BlockSpec((None, k, TN_B), lambda j, i: (layer, 0, j)),
            pl.BlockSpec((TM_PROJ, HEAD_DIM), lambda j, i: (i % tpb, 0)),
            pl.BlockSpec((TM_PROJ, HEAD_DIM), lambda j, i: (i % tpb, 0)),
        ],
        out_specs=pl.BlockSpec((hpt, TM_PROJ, HEAD_DIM), lambda j, i: (j, i, 0)),
        out_shape=jax.ShapeDtypeStruct((3 * B_HEADS, n, HEAD_DIM), bf16),
        compiler_params=_cparams(("parallel", "parallel")),
    )(xb, w_b, cos_b, sin_b)


def _gla_kernel(q_ref, k_ref, v_ref, r_ref, lr_ref, w2f_ref, bf_ref, w2b_ref, bb_ref, g_ref,
                o_ref, acc_ref, *, t):
    c_len = GLA_CHUNK
    n_chunks = t // c_len
    row = lax.broadcasted_iota(jnp.int32, (c_len, c_len), 0)
    col = lax.broadcasted_iota(jnp.int32, (c_len, c_len), 1)

    def direction(fwd):
        tri = jnp.where((row >= col) if fwd else (row <= col), 1.0, 0.0).astype(bf16)
        smask = (col <= row) if fwd else (col > row)
        w2 = (w2f_ref if fwd else w2b_ref)[...].astype(bf16)
        bias = (bf_ref if fwd else bb_ref)[...]
        lo = 0 if fwd else GLA_RANK

        def body(step, s_t):
            c = step if fwd else n_chunks - 1 - step
            sl = pl.ds(pl.multiple_of(c * c_len, c_len), c_len)
            q = q_ref[sl, :].astype(f32) * (GLA_DK ** -0.5)
            k = k_ref[sl, :].astype(f32)
            v = v_ref[sl, :]
            lr = lr_ref[sl, :][:, lo:lo + GLA_RANK]
            z = jnp.dot(lr, w2, preferred_element_type=f32) + bias
            g = (jnp.minimum(z, 0.0) - jnp.log1p(jnp.exp(-jnp.abs(z)))) * (1.0 / GLA_TAU)
            g1 = g.astype(bf16)
            rem = g - g1.astype(f32)
            g2 = rem.astype(bf16)
            g3 = (rem - g2.astype(f32)).astype(bf16)
            b = (jnp.dot(tri, g1, preferred_element_type=f32)
                 + jnp.dot(tri, g2, preferred_element_type=f32)
                 + jnp.dot(tri, g3, preferred_element_type=f32))
            b_last = b[c_len - 1:c_len, :] if fwd else b[0:1, :]
            q_t = (q * jnp.exp(b)).astype(bf16)
            k_t = (k * jnp.exp(-b)).astype(bf16)
            k_s = (k * jnp.exp(b_last - b)).astype(bf16)
            dec = jnp.exp(b_last)
            scores = jnp.where(smask, _nt_dot(q_t, k_t), 0.0)
            o = (jnp.dot(scores.astype(bf16), v, preferred_element_type=f32)
                 + _nt_dot(q_t, s_t.astype(bf16)))
            if fwd:
                acc_ref[sl, :] = o
            else:
                acc_ref[sl, :] += o
            return s_t * dec + _tn_dot(v, k_s)

        lax.fori_loop(0, n_chunks, body, jnp.zeros((GLA_DV, GLA_DK), f32))

    direction(True)
    direction(False)

    rows = 256
    gain = g_ref[...]

    def finish(i, carry):
        sl = pl.ds(pl.multiple_of(i * rows, rows), rows)
        x = acc_ref[sl, :]
        ms = jnp.mean(x * x, axis=-1, keepdims=True)
        y = x * lax.rsqrt(ms + RMS_EPS) * gain
        r = r_ref[sl, :].astype(f32)
        o_ref[sl, :] = (y * (r * jax.nn.sigmoid(r))).astype(o_ref.dtype)
        return carry

    lax.fori_loop(0, t // rows, finish, 0)


def _gla(main, w2f, b_f, w2b, b_b, gain, layer, bsz, t):
    n = bsz * t
    qk_blk = lambda off: (lambda b, h: (b, off // GLA_DK + h))
    v_blk = lambda off: (lambda b, h: (b, off // GLA_DV + h))
    wspec = pl.BlockSpec((None, GLA_RANK, GLA_DK), lambda b, h: (layer, 0, h))
    bspec = pl.BlockSpec((None, 1, GLA_DK), lambda b, h: (layer, 0, h))
    return pl.pallas_call(
        functools.partial(_gla_kernel, t=t),
        grid=(bsz, GLA_HEADS),
        in_specs=[
            pl.BlockSpec((t, GLA_DK), qk_blk(OFF_AQ)),
            pl.BlockSpec((t, GLA_DK), qk_blk(OFF_AK)),
            pl.BlockSpec((t, GLA_DV), v_blk(OFF_AV)),
            pl.BlockSpec((t, GLA_DV), v_blk(OFF_AR)),
            pl.BlockSpec((t, LR_PAD), lambda b, h: (b, OFF_LR // LR_PAD)),
            wspec, bspec, wspec, bspec,
            pl.BlockSpec((None, 1, GLA_DV), lambda b, h: (layer, 0, h)),
        ],
        out_specs=pl.BlockSpec((t, GLA_DV), lambda b, h: (b, h)),
        out_shape=jax.ShapeDtypeStruct((n, A_V), bf16),
        scratch_shapes=[pltpu.VMEM((t, GLA_DV), f32)],
        compiler_params=_cparams(("parallel", "parallel")),
    )(main, main, main, main, main, w2f, b_f, w2b, b_b, gain)


def _dil_kernel(q_ref, k_ref, v_ref, o_ref, lse_ref, *, n_sub, dil):
    bq = min(128, n_sub)
    win = min(bq + 2 * DIL_HALF, n_sub)
    n_blk = n_sub // bq
    qi = lax.broadcasted_iota(jnp.int32, (bq, win), 0)
    ki = lax.broadcasted_iota(jnp.int32, (bq, win), 1)
    for r in range(dil):
        ls = slice(r * HEAD_DIM, (r + 1) * HEAD_DIM)

        def body(i, carry, ls=ls):
            q0 = pl.multiple_of(i * bq, bq)
            k0 = pl.multiple_of(jnp.clip(i * bq - DIL_HALF, 0, n_sub - win), DIL_HALF)
            q = q_ref[0, pl.ds(q0, bq), ls]
            k = k_ref[0, pl.ds(k0, win), ls]
            v = v_ref[0, pl.ds(k0, win), ls]
            s = _nt_dot(q, k)
            valid = jnp.abs((q0 + qi) - (k0 + ki)) <= DIL_HALF
            s = jnp.where(valid, s, NEG_BIG)
            m = jnp.max(s, axis=-1, keepdims=True)
            p = jnp.exp(s - m)
            l = jnp.sum(p, axis=-1, keepdims=True)
            o = jnp.dot(p.astype(bf16), v, preferred_element_type=f32) / l
            o_ref[0, pl.ds(q0, bq), ls] = o.astype(o_ref.dtype)
            lse_ref[0, pl.ds(q0, bq), ls] = jnp.broadcast_to(m + jnp.log(l), (bq, HEAD_DIM))
            return carry

        lax.fori_loop(0, n_blk, body, 0)


def _dilated(qkv_b, group, bsz, t):
    _, dil = DIL_CONFIGS[group]
    n = bsz * t
    n_sub = t // dil
    width = dil * HEAD_DIM
    view = qkv_b.reshape(3 * B_HEADS, n // dil, width)
    blk = (1, n_sub, width)
    head0 = group * DIL_HEADS
    o, lse = pl.pallas_call(
        functools.partial(_dil_kernel, n_sub=n_sub, dil=dil),
        grid=(bsz, DIL_HEADS),
        in_specs=[
            pl.BlockSpec(blk, lambda b, h: (head0 + h, b, 0)),
            pl.BlockSpec(blk, lambda b, h: (B_HEADS + head0 + h, b, 0)),
            pl.BlockSpec(blk, lambda b, h: (2 * B_HEADS + head0 + h, b, 0)),
        ],
        out_specs=[
            pl.BlockSpec(blk, lambda b, h: (h, b, 0)),
            pl.BlockSpec(blk, lambda b, h: (h, b, 0)),
        ],
        out_shape=[
            jax.ShapeDtypeStruct((DIL_HEADS, n // dil, width), bf16),
            jax.ShapeDtypeStruct((DIL_HEADS, n // dil, width), f32),
        ],
        compiler_params=_cparams(("parallel", "parallel")),
    )(view, view, view)
    return o.reshape(DIL_HEADS, n, HEAD_DIM), lse.reshape(DIL_HEADS, n, HEAD_DIM)


def _axial_partner(x):
    lane = lax.broadcasted_iota(jnp.int32, x.shape, 1)
    quarter = HEAD_DIM // 4
    first = (lane % (2 * quarter)) < quarter
    return jnp.where(first, pltpu.roll(x, HEAD_DIM - quarter, 1), pltpu.roll(x, quarter, 1))


def _c_prep_kernel(q_ref, k_ref, cos_ref, sin_ref, qg_ref, kg_ref, qo_ref, ko_ref):
    cos = cos_ref[...]
    sin = sin_ref[...]

    def prep(x, gain, scale):
        x = x.astype(f32)
        ms = jnp.mean(x * x, axis=-1, keepdims=True)
        y = x * lax.rsqrt(ms + RMS_EPS) * gain
        return (y * cos + _axial_partner(y) * sin) * scale

    qg = qg_ref[...]
    kg = kg_ref[...]
    for h in range(GQA_Q_HEADS):
        ls = slice(h * HEAD_DIM, (h + 1) * HEAD_DIM)
        qo_ref[:, ls] = prep(q_ref[:, ls], qg, HEAD_DIM ** -0.5).astype(qo_ref.dtype)
    for h in range(GQA_KV_HEADS):
        ls = slice(h * HEAD_DIM, (h + 1) * HEAD_DIM)
        ko_ref[:, ls] = prep(k_ref[:, ls], kg, 1.0).astype(ko_ref.dtype)


def _c_prep(main, cos_c, sin_c, q_gain, k_gain, layer, t):
    n = main.shape[0]
    tm = TM_PROJ
    tpb = t // tm
    gspec = pl.BlockSpec((None, 1, HEAD_DIM), lambda i: (layer, 0, 0))
    return pl.pallas_call(
        _c_prep_kernel,
        grid=(n // tm,),
        in_specs=[
            pl.BlockSpec((tm, C_Q), lambda i: (i, OFF_CQ // C_Q)),
            pl.BlockSpec((tm, C_KV), lambda i: (i, OFF_CK // C_KV)),
            pl.BlockSpec((tm, HEAD_DIM), lambda i: (i % tpb, 0)),
            pl.BlockSpec((tm, HEAD_DIM), lambda i: (i % tpb, 0)),
            gspec, gspec,
        ],
        out_specs=[
            pl.BlockSpec((tm, C_Q), lambda i: (i, 0)),
            pl.BlockSpec((tm, C_KV), lambda i: (i, 0)),
        ],
        out_shape=[
            jax.ShapeDtypeStruct((n, C_Q), bf16),
            jax.ShapeDtypeStruct((n, C_KV), bf16),
        ],
        compiler_params=_cparams(("parallel",)),
    )(main, main, cos_c, sin_c, q_gain, k_gain)


def _gqa_kernel(q_ref, k_ref, v_ref, o_ref, *, t, tq, tk):
    grp = GQA_Q_HEADS // GQA_KV_HEADS
    q4 = q_ref[...]
    q = jnp.concatenate([q4[:, h * HEAD_DIM:(h + 1) * HEAD_DIM] for h in range(grp)], axis=0)
    rows = grp * tq

    def body(c, carry):
        m, l, acc = carry
        sl = pl.ds(pl.multiple_of(c * tk, tk), tk)
        s = _nt_dot(q, k_ref[sl, :])
        m_new = jnp.maximum(m, jnp.max(s, axis=-1, keepdims=True))
        a = jnp.exp(m - m_new)
        p = jnp.exp(s - m_new)
        l = a * l + jnp.sum(p, axis=-1, keepdims=True)
        acc = a * acc + jnp.dot(p.astype(bf16), v_ref[sl, :], preferred_element_type=f32)
        return m_new, l, acc

    init = (jnp.full((rows, 1), NEG_BIG, f32), jnp.zeros((rows, 1), f32),
            jnp.zeros((rows, HEAD_DIM), f32))
    _, l, acc = lax.fori_loop(0, t // tk, body, init)
    o = (acc / l).astype(o_ref.dtype)
    o_ref[...] = jnp.concatenate([o[h * tq:(h + 1) * tq, :] for h in range(grp)], axis=1)


def _gqa(qc, kc, main, bsz, t):
    n = bsz * t
    tq = 256
    tk = min(t, 1024)
    grp_cols = (GQA_Q_HEADS // GQA_KV_HEADS) * HEAD_DIM
    nq = t // tq
    return pl.pallas_call(
        functools.partial(_gqa_kernel, t=t, tq=tq, tk=tk),
        grid=(bsz, GQA_KV_HEADS, nq),
        in_specs=[
            pl.BlockSpec((tq, grp_cols), lambda b, j, i: (b * nq + i, j)),
            pl.BlockSpec((t, HEAD_DIM), lambda b, j, i: (b, j)),
            pl.BlockSpec((t, HEAD_DIM), lambda b, j, i: (b, OFF_CV // HEAD_DIM + j)),
        ],
        out_specs=pl.BlockSpec((tq, grp_cols), lambda b, j, i: (b * nq + i, j)),
        out_shape=jax.ShapeDtypeStruct((n, C_Q), bf16),
        compiler_params=_cparams(("parallel", "parallel", "parallel")),
    )(qc, kc, main)


def _merge_kernel(oa_ref, ob0_ref, ob1_ref, ob2_ref, l0_ref, l1_ref, l2_ref, oc_ref,
                  ga_ref, gb_ref, gc_ref, wa_ref, wb_ref, wc_ref, o_ref):
    heads = []
    for h in range(DIL_HEADS):
        l0, l1, l2 = l0_ref[h], l1_ref[h], l2_ref[h]
        m = jnp.maximum(jnp.maximum(l0, l1), l2)
        e0, e1, e2 = jnp.exp(l0 - m), jnp.exp(l1 - m), jnp.exp(l2 - m)
        mix = (e0 * ob0_ref[h].astype(f32) + e1 * ob1_ref[h].astype(f32)
               + e2 * ob2_ref[h].astype(f32)) / (e0 + e1 + e2)
        heads.append(mix.astype(bf16))
    o_b = jnp.concatenate(heads, axis=1)
    up_a = jnp.dot(oa_ref[...], wa_ref[...], preferred_element_type=f32)
    up_b = jnp.dot(o_b, wb_ref[...], preferred_element_type=f32)
    up_c = jnp.dot(oc_ref[...], wc_ref[...], preferred_element_type=f32)
    merged = (jax.nn.sigmoid(ga_ref[...].astype(f32)) * up_a
              + jax.nn.sigmoid(gb_ref[...].astype(f32)) * up_b
              + jax.nn.sigmoid(gc_ref[...].astype(f32)) * up_c)
    o_ref[...] = merged.astype(o_ref.dtype)


def _merge(o_a, o_bs, lses, o_c, main, w_up_a, w_up_b, w_up_c, layer):
    n = o_a.shape[0]
    tm = TM_TOK
    gate_blk = OFF_GATE // D_MODEL
    hspec = pl.BlockSpec((DIL_HEADS, tm, HEAD_DIM), lambda i: (0, i, 0))
    gspec = lambda which: pl.BlockSpec((tm, D_MODEL), lambda i: (i, gate_blk + which))
    wspec = lambda rows: pl.BlockSpec((None, rows, D_MODEL), lambda i: (layer, 0, 0))
    return pl.pallas_call(
        _merge_kernel,
        grid=(n // tm,),
        in_specs=[
            pl.BlockSpec((tm, A_V), lambda i: (i, 0)),
            hspec, hspec, hspec, hspec, hspec, hspec,
            pl.BlockSpec((tm, C_Q), lambda i: (i, 0)),
            gspec(0), gspec(1), gspec(2),
            wspec(A_V), wspec(B_OUT), wspec(C_Q),
        ],
        out_specs=pl.BlockSpec((tm, D_MODEL), lambda i: (i, 0)),
        out_shape=jax.ShapeDtypeStruct((n, D_MODEL), bf16),
        compiler_params=_cparams(("parallel",)),
    )(o_a, o_bs[0], o_bs[1], o_bs[2], lses[0], lses[1], lses[2], o_c,
      main, main, main, w_up_a, w_up_b, w_up_c)


def _layer_norm_rows(y, g, b):
    mu = jnp.mean(y, axis=-1, keepdims=True)
    d = y - mu
    var = jnp.mean(d * d, axis=-1, keepdims=True)
    return d * lax.rsqrt(var + LN_EPS) * g + b


def _route_rows(logits_t, bias_col):
    scores = jax.nn.sigmoid(logits_t)
    sel = scores + bias_col
    rows = [sel[e:e + 1, :] for e in range(N_EXPERTS)]
    srow = [scores[e:e + 1, :] for e in range(N_EXPERTS)]
    best_val = None
    best_grp = None
    for g in range(N_EXPERT_GROUPS):
        mem = rows[g * EXPERTS_PER_GROUP:(g + 1) * EXPERTS_PER_GROUP]
        top2 = None
        for a in range(EXPERTS_PER_GROUP):
            for b in range(a + 1, EXPERTS_PER_GROUP):
                pair = mem[a] + mem[b]
                top2 = pair if top2 is None else jnp.maximum(top2, pair)
        if g == 0:
            best_val, best_grp = top2, jnp.zeros_like(top2, dtype=jnp.int32)
        else:
            upd = top2 > best_val
            best_val = jnp.where(upd, top2, best_val)
            best_grp = jnp.where(upd, g, best_grp)
    neg = jnp.full_like(best_val, -jnp.inf)
    cand = [jnp.where(best_grp == (e // EXPERTS_PER_GROUP), rows[e], neg) for e in range(N_EXPERTS)]

    def arg_top(vals):
        bv, bi = vals[0], jnp.zeros_like(best_grp)
        for e in range(1, N_EXPERTS):
            upd = vals[e] > bv
            bv = jnp.where(upd, vals[e], bv)
            bi = jnp.where(upd, e, bi)
        return bi

    idx1 = arg_top(cand)
    idx2 = arg_top([jnp.where(idx1 == e, neg, cand[e]) for e in range(N_EXPERTS)])
    zero = jnp.zeros_like(best_val)
    s1 = zero
    s2 = zero
    for e in range(N_EXPERTS):
        s1 = s1 + jnp.where(idx1 == e, srow[e], zero)
        s2 = s2 + jnp.where(idx2 == e, srow[e], zero)
    tot = s1 + s2
    return idx1, idx2, s1 / tot, s2 / tot


def _outproj_kernel(m_ref, w_ref, x_ref, g_ref, b_ref, wr_ref, rb_ref,
                    x1_ref, x1b_ref, idx_ref, wt_ref):
    h = jnp.dot(m_ref[...], w_ref[...], preferred_element_type=f32)
    x1 = _layer_norm_rows(DEEPNORM_ALPHA * x_ref[...] + h, g_ref[...], b_ref[...])
    x1_ref[...] = x1
    x1b_ref[...] = x1.astype(bf16)
    logits_t = lax.dot_general(wr_ref[...], x1, (((1,), (1,)), ((), ())),
                               precision=lax.Precision.HIGHEST, preferred_element_type=f32)
    i1, i2, w1, w2 = _route_rows(logits_t, rb_ref[...])
    idx_ref[...] = jnp.concatenate([i1, i2], axis=0)
    wt_ref[...] = jnp.concatenate([w1, w2], axis=0)


def _outproj_ln_route(merged, w_out, x, ln_g, ln_b, w_router_t, router_bias_col, layer):
    n = x.shape[0]
    tm = TM_TOK
    vspec = pl.BlockSpec((None, 1, D_MODEL), lambda i: (layer, 0, 0))
    return pl.pallas_call(
        _outproj_kernel,
        grid=(n // tm,),
        in_specs=[
            pl.BlockSpec((tm, D_MODEL), lambda i: (i, 0)),
            pl.BlockSpec((None, D_MODEL, D_MODEL), lambda i: (layer, 0, 0)),
            pl.BlockSpec((tm, D_MODEL), lambda i: (i, 0)),
            vspec, vspec,
            pl.BlockSpec((N_EXPERTS, D_MODEL), lambda i: (0, 0)),
            pl.BlockSpec((N_EXPERTS, 1), lambda i: (0, 0)),
        ],
        out_specs=[
            pl.BlockSpec((tm, D_MODEL), lambda i: (i, 0)),
            pl.BlockSpec((tm, D_MODEL), lambda i: (i, 0)),
            pl.BlockSpec((2, tm), lambda i: (0, i)),
            pl.BlockSpec((2, tm), lambda i: (0, i)),
        ],
        out_shape=[
            jax.ShapeDtypeStruct((n, D_MODEL), f32),
            jax.ShapeDtypeStruct((n, D_MODEL), bf16),
            jax.ShapeDtypeStruct((2, n), jnp.int32),
            jax.ShapeDtypeStruct((2, n), f32),
        ],
        compiler_params=_cparams(("parallel",)),
    )(merged, w_out, x, ln_g, ln_b, w_router_t, router_bias_col)


def _moe_kernel(te_ref, nt_ref, x_ref, wg_ref, wu_ref, wd_ref, o_ref):
    del te_ref
    tile = pl.program_id(0)

    @pl.when(tile < nt_ref[0])
    def _():
        x = x_ref[...]
        gate = jnp.dot(x, wg_ref[...], preferred_element_type=f32)
        up = jnp.dot(x, wu_ref[...], preferred_element_type=f32)
        h = (gate * jax.nn.sigmoid(gate) * up).astype(bf16)
        o_ref[...] = jnp.dot(h, wd_ref[...], preferred_element_type=f32).astype(o_ref.dtype)

    @pl.when(tile >= nt_ref[0])
    def _():
        o_ref[...] = jnp.zeros_like(o_ref)


def _moe_grouped(x_sorted, tile_expert, n_tiles_used, w_gate, w_up, w_down, layer):
    m_pad = x_sorted.shape[0]
    n_tiles = m_pad // TM_MOE
    wspec_in = pl.BlockSpec((None, None, D_MODEL, D_FF_EXPERT), lambda t, te, nt: (layer, te[t], 0, 0))
    wspec_out = pl.BlockSpec((None, None, D_FF_EXPERT, D_MODEL), lambda t, te, nt: (layer, te[t], 0, 0))
    grid_spec = pltpu.PrefetchScalarGridSpec(
        num_scalar_prefetch=2,
        grid=(n_tiles,),
        in_specs=[
            pl.BlockSpec((TM_MOE, D_MODEL), lambda t, te, nt: (t, 0)),
            wspec_in, wspec_in, wspec_out,
        ],
        out_specs=pl.BlockSpec((TM_MOE, D_MODEL), lambda t, te, nt: (t, 0)),
    )
    return pl.pallas_call(
        _moe_kernel,
        grid_spec=grid_spec,
        out_shape=jax.ShapeDtypeStruct((m_pad, D_MODEL), bf16),
        compiler_params=_cparams(("arbitrary",)),
    )(tile_expert, n_tiles_used, x_sorted, w_gate, w_up, w_down)


def _combine_kernel(x_ref, y1_ref, y2_ref, w1_ref, w2_ref, g_ref, b_ref, o_ref, ob_ref):
    w1 = w1_ref[...]
    w2 = w2_ref[...]
    for c in range(D_MODEL // LANES):
        ls = slice(c * LANES, (c + 1) * LANES)
        o_ref[:, ls] = (DEEPNORM_ALPHA * x_ref[:, ls] + w1 * y1_ref[:, ls].astype(f32)
                        + w2 * y2_ref[:, ls].astype(f32))
    x2 = _layer_norm_rows(o_ref[...], g_ref[...], b_ref[...])
    o_ref[...] = x2
    ob_ref[...] = x2.astype(bf16)


def _combine_ln(x1, y1, y2, w1b, w2b, ln_g, ln_b, layer):
    n = x1.shape[0]
    tm = TM_TOK
    row = pl.BlockSpec((tm, D_MODEL), lambda i: (i, 0))
    wsp = pl.BlockSpec((tm, LANES), lambda i: (i, 0))
    vspec = pl.BlockSpec((None, 1, D_MODEL), lambda i: (layer, 0, 0))
    return pl.pallas_call(
        _combine_kernel,
        grid=(n // tm,),
        in_specs=[row, row, row, wsp, wsp, vspec, vspec],
        out_specs=[row, row],
        out_shape=[jax.ShapeDtypeStruct((n, D_MODEL), f32), jax.ShapeDtypeStruct((n, D_MODEL), bf16)],
        compiler_params=_cparams(("parallel",)),
    )(x1, y1, y2, w1b, w2b, ln_g, ln_b)


def _dispatch_plan(idx):
    n = idx.shape[1]
    m_pad = 2 * n + N_EXPERTS * TM_MOE
    n_tiles = m_pad // TM_MOE
    e_flat = idx.reshape(-1)
    onehot = (e_flat[:, None] == jnp.arange(N_EXPERTS, dtype=jnp.int32)[None, :]).astype(jnp.int32)
    rank = jnp.sum((jnp.cumsum(onehot, axis=0) - onehot) * onehot, axis=1)
    counts = jnp.sum(onehot, axis=0)
    tiles_per = (counts + TM_MOE - 1) // TM_MOE
    tile_end = jnp.cumsum(tiles_per)
    start_row = (tile_end - tiles_per) * TM_MOE
    dest = start_row[e_flat] + rank
    tok = jnp.tile(jnp.arange(n, dtype=jnp.int32), 2)
    src = jnp.zeros((m_pad,), jnp.int32).at[dest].set(tok)
    n_used = tile_end[-1]
    tile_ids = jnp.minimum(jnp.arange(n_tiles, dtype=jnp.int32), n_used - 1)
    tile_expert = jnp.searchsorted(tile_end, tile_ids, side="right").astype(jnp.int32)
    tile_expert = jnp.minimum(tile_expert, N_EXPERTS - 1)
    return src, dest, tile_expert, n_used.reshape(1).astype(jnp.int32)


def _rope_tables(pos, dim):
    inv_freq = ROPE_THETA ** (-jnp.arange(0, dim, 2, dtype=f32) / dim)
    ang = pos.astype(f32)[:, None] * inv_freq[None, :]
    return jnp.cos(ang), jnp.sin(ang)


def _tables(t):
    cos, sin = _rope_tables(jnp.arange(t), HEAD_DIM)
    cos_b = jnp.concatenate([cos, cos], axis=-1)
    sin_b = jnp.concatenate([-sin, sin], axis=-1)
    tok = jnp.arange(t)
    cr, sr = _rope_tables(tok // GRID_W, HEAD_DIM // 2)
    cc, sc = _rope_tables(tok % GRID_W, HEAD_DIM // 2)
    cos_c = jnp.concatenate([cr, cr, cc, cc], axis=-1)
    sin_c = jnp.concatenate([-sr, sr, -sc, sc], axis=-1)
    return cos_b, sin_b, cos_c, sin_c


def _repack_w_in(w_in):
    offs = [0]
    for s in IN_SPLITS:
        offs.append(offs[-1] + s)
    seg = [w_in[:, :, offs[i]:offs[i + 1]] for i in range(len(IN_SPLITS))]
    a_q, a_k, a_v, a_r, a_lr, b_qkv, c_q, c_k, c_v, gate = seg
    lr_pad = jnp.pad(a_lr, ((0, 0), (0, 0), (0, LR_PAD - A_LR)))
    w_main = jnp.concatenate([gate, a_q, a_k, a_v, a_r, c_q, c_k, c_v, lr_pad], axis=-1).astype(bf16)
    return w_main, b_qkv.astype(bf16)


def _mixer(xb, wts, layer, bsz, t, tables):
    cos_b, sin_b, cos_c, sin_c = tables
    main = _proj_main(xb, wts["w_main"], layer)
    qkv_b = _proj_b(xb, wts["w_b"], cos_b, sin_b, layer, t)
    o_a = _gla(main, wts["gla_w2_f"], wts["gla_b_f"], wts["gla_w2_b"], wts["gla_b_b"],
               wts["gla_norm_g"], layer, bsz, t)
    o_bs, lses = [], []
    for group in range(len(DIL_CONFIGS)):
        o_g, lse_g = _dilated(qkv_b, group, bsz, t)
        o_bs.append(o_g)
        lses.append(lse_g)
    qc, kc = _c_prep(main, cos_c, sin_c, wts["q_norm_g"], wts["k_norm_g"], layer, t)
    o_c = _gqa(qc, kc, main, bsz, t)
    return _merge(o_a, o_bs, lses, o_c, main, wts["w_up_a"], wts["w_up_b"], wts["w_up_c"], layer)


def _moe(x1, x1b, idx, wsel, wts, layer):
    n = x1.shape[0]
    src, dest, tile_expert, n_used = _dispatch_plan(idx)
    x_sorted = jnp.take(x1b, src, axis=0)
    y_sorted = _moe_grouped(x_sorted, tile_expert, n_used, wts["moe_w_gate"], wts["moe_w_up"],
                            wts["moe_w_down"], layer)
    y1 = jnp.take(y_sorted, dest[:n], axis=0)
    y2 = jnp.take(y_sorted, dest[n:], axis=0)
    w1b = jnp.broadcast_to(wsel[0][:, None], (n, LANES))
    w2b = jnp.broadcast_to(wsel[1][:, None], (n, LANES))
    return _combine_ln(x1, y1, y2, w1b, w2b, wts["ln2_g"], wts["ln2_b"], layer)


def _trunk(x3, wts, depth=DEPTH):
    bsz, t, _ = x3.shape
    n = bsz * t
    x = x3.reshape(n, D_MODEL)
    xb = x.astype(bf16)
    tables = _tables(t)
    for layer in range(depth):
        merged = _mixer(xb, wts, layer, bsz, t, tables)
        x1, x1b, idx, wsel = _outproj_ln_route(merged, wts["w_out"], x, wts["ln1_g"], wts["ln1_b"],
                                               wts["w_router_t"], wts["router_bias_col"], layer)
        x, xb = _moe(x1, x1b, idx, wsel, wts, layer)
    return x.reshape(bsz, t, D_MODEL)


def _prepare_weights(w_in, gla_w2_f, gla_b_f, gla_w2_b, gla_b_b, gla_norm_g, q_norm_g, k_norm_g,
                     w_up_a, w_up_b, w_up_c, w_out, ln1_g, ln1_b, w_router, router_bias,
                     moe_w_gate, moe_w_up, moe_w_down, ln2_g, ln2_b):
    w_main, w_b = _repack_w_in(w_in)
    row = lambda a: a.reshape(DEPTH, 1, a.shape[-1]).astype(f32)
    return {
        "w_main": w_main,
        "w_b": w_b,
        "gla_w2_f": gla_w2_f.astype(f32),
        "gla_b_f": row(gla_b_f),
        "gla_w2_b": gla_w2_b.astype(f32),
        "gla_b_b": row(gla_b_b),
        "gla_norm_g": row(gla_norm_g),
        "q_norm_g": row(q_norm_g),
        "k_norm_g": row(k_norm_g),
        "w_up_a": w_up_a.astype(bf16),
        "w_up_b": w_up_b.astype(bf16),
        "w_up_c": w_up_c.astype(bf16),
        "w_out": w_out.astype(bf16),
        "ln1_g": row(ln1_g),
        "ln1_b": row(ln1_b),
        "w_router_t": w_router.astype(f32).T,
        "router_bias_col": router_bias.astype(f32).reshape(N_EXPERTS, 1),
        "moe_w_gate": moe_w_gate.astype(bf16),
        "moe_w_up": moe_w_up.astype(bf16),
        "moe_w_down": moe_w_down.astype(bf16),
        "ln2_g": row(ln2_g),
        "ln2_b": row(ln2_b),
    }


def kernel(x_prompt, x_sample, w_in, gla_w2_f, gla_b_f, gla_w2_b, gla_b_b, gla_norm_g, q_norm_g, k_norm_g, w_up_a, w_up_b, w_up_c, w_out, ln1_g, ln1_b, w_router, router_bias, moe_w_gate, moe_w_up, moe_w_down, ln2_g, ln2_b):
    wts = _prepare_weights(w_in, gla_w2_f, gla_b_f, gla_w2_b, gla_b_b, gla_norm_g, q_norm_g, k_norm_g,
                           w_up_a, w_up_b, w_up_c, w_out, ln1_g, ln1_b, w_router, router_bias,
                           moe_w_gate, moe_w_up, moe_w_down, ln2_g, ln2_b)
    return _trunk(x_prompt, wts), _trunk(x_sample, wts)
```

```python
import functools

import jax
import jax.numpy as jnp
from jax import lax
from jax.experimental import pallas as pl
from jax.experimental.pallas import tpu as pltpu

D_MODEL = 2048
DEPTH = 4
HEAD_DIM = 128
GRID_W = 64
ROPE_THETA = 10000.0
LN_EPS = 1e-5
RMS_EPS = 1e-6
GLA_HEADS = 4
GLA_DK = 128
GLA_DV = 256
GLA_RANK = 16
GLA_TAU = 16.0
GLA_CHUNK = 64
DIL_CONFIGS = ((128, 1), (512, 4), (2048, 16))
DIL_HEADS = 4
GQA_Q_HEADS = 8
GQA_KV_HEADS = 2
N_EXPERTS = 16
N_EXPERT_GROUPS = 4
EXPERTS_PER_GROUP = N_EXPERTS // N_EXPERT_GROUPS
D_FF_EXPERT = 1024
DEEPNORM_ALPHA = (2.0 * DEPTH) ** 0.25

A_QK = GLA_HEADS * GLA_DK
A_V = GLA_HEADS * GLA_DV
A_LR = 2 * GLA_RANK
B_HEADS = len(DIL_CONFIGS) * DIL_HEADS
B_QKV = 3 * B_HEADS * HEAD_DIM
B_OUT = DIL_HEADS * HEAD_DIM
C_Q = GQA_Q_HEADS * HEAD_DIM
C_KV = GQA_KV_HEADS * HEAD_DIM
GATE_COLS = 3 * D_MODEL
IN_SPLITS = (A_QK, A_QK, A_V, A_V, A_LR, B_QKV, C_Q, C_KV, C_KV, GATE_COLS)

LANES = 128
VMEM_LIMIT = 56 * 1024 * 1024
LR_PAD = LANES

OFF_GATE = 0
OFF_AQ = OFF_GATE + GATE_COLS
OFF_AK = OFF_AQ + A_QK
OFF_AV = OFF_AK + A_QK
OFF_AR = OFF_AV + A_V
OFF_CQ = OFF_AR + A_V
OFF_CK = OFF_CQ + C_Q
OFF_CV = OFF_CK + C_KV
OFF_LR = OFF_CV + C_KV
MAIN_USED = OFF_LR + LR_PAD

TM_PROJ = 1024
TN_MAIN = 1024
TM_ROUTE = 2048
MAIN_COLS = -(-MAIN_USED // TN_MAIN) * TN_MAIN
TN_B = 512
TM_TOK = 256
TM_MOE = 512
GQA_TQ = 256
GQA_TK = 1024
GQA_STRIP = 32
LOG2_E = 1.4426950408889634
GLA_SUPER = 8
DIL_UNROLL = 4
DIL_HALF = 64
NEG_BIG = -1e30

f32 = jnp.float32
bf16 = jnp.bfloat16


def _cparams(sem):
    return pltpu.CompilerParams(dimension_semantics=sem, vmem_limit_bytes=VMEM_LIMIT)


def _nt_dot(a, b):
    return lax.dot_general(a, b, (((1,), (1,)), ((), ())), preferred_element_type=f32)


def _tn_dot(a, b):
    return lax.dot_general(a, b, (((0,), (0,)), ((), ())), preferred_element_type=f32)


def _mm_kernel(x_ref, w_ref, o_ref):
    o_ref[...] = jnp.dot(x_ref[...], w_ref[...], preferred_element_type=f32).astype(o_ref.dtype)


def _proj_main(xb, w_main, layer):
    n, k = xb.shape
    grid = (MAIN_COLS // TN_MAIN, n // TM_PROJ)
    return pl.pallas_call(
        _mm_kernel,
        grid=grid,
        in_specs=[
            pl.BlockSpec((TM_PROJ, k), lambda j, i: (i, 0)),
            pl.BlockSpec((None, k, TN_MAIN), lambda j, i: (layer, 0, j)),
        ],
        out_specs=pl.BlockSpec((TM_PROJ, TN_MAIN), lambda j, i: (i, j)),
        out_shape=jax.ShapeDtypeStruct((n, MAIN_COLS), bf16),
        compiler_params=_cparams(("parallel", "parallel")),
    )(xb, w_main)


def _proj_b_kernel(x_ref, w_ref, cos_ref, sin_ref, o_ref):
    j = pl.program_id(0)
    heads_per_tile = TN_B // HEAD_DIM
    q_tiles = B_HEADS // heads_per_tile
    acc = jnp.dot(x_ref[...], w_ref[...], preferred_element_type=f32)
    cos = cos_ref[...]
    sin = sin_ref[...]
    is_rot = j < 2 * q_tiles
    scale = jnp.where(j < q_tiles, HEAD_DIM ** -0.5, 1.0).astype(f32)
    for h in range(heads_per_tile):
        a = acc[:, h * HEAD_DIM:(h + 1) * HEAD_DIM]
        rot = (a * cos + pltpu.roll(a, HEAD_DIM // 2, 1) * sin) * scale
        o_ref[h] = jnp.where(is_rot, rot, a).astype(o_ref.dtype)


def _proj_b(xb, w_b, cos_b, sin_b, layer, t):
    n, k = xb.shape
    tpb = t // TM_PROJ
    grid = (B_QKV // TN_B, n // TM_PROJ)
    hpt = TN_B // HEAD_DIM
    return pl.pallas_call(
        _proj_b_kernel,
        grid=grid,
        in_specs=[
            pl.BlockSpec((TM_PROJ, k), lambda j, i: (i, 0)),
            pl.BlockSpec((None, k, TN_B), lambda j, i: (layer, 0, j)),
            pl.BlockSpec((TM_PROJ, HEAD_DIM), lambda j, i: (i % tpb, 0)),
            pl.BlockSpec((TM_PROJ, HEAD_DIM), lambda j, i: (i % tpb, 0)),
        ],
        out_specs=pl.BlockSpec((hpt, TM_PROJ, HEAD_DIM), lambda j, i: (j, i, 0)),
        out_shape=jax.ShapeDtypeStruct((3 * B_HEADS, n, HEAD_DIM), bf16),
        compiler_params=_cparams(("parallel", "parallel")),
    )(xb, w_b, cos_b, sin_b)


def _gla_kernel(q_ref, k_ref, v_ref, r_ref, lr_ref, w2f_ref, bf_ref, w2b_ref, bb_ref, g_ref,
                o_ref, accf_ref, accb_ref, *, t):
    c_len = GLA_CHUNK
    sup = GLA_SUPER * c_len
    n_sup = t // sup
    row = lax.broadcasted_iota(jnp.int32, (sup, sup), 0)
    col = lax.broadcasted_iota(jnp.int32, (sup, sup), 1)
    same = (row // c_len) == (col // c_len)

    def half(base, fwd, s_t):
        tri = jnp.where(same & ((row >= col) if fwd else (row <= col)), 1.0, 0.0).astype(bf16)
        smask = same & ((col <= row) if fwd else (col > row))
        w2 = (w2f_ref if fwd else w2b_ref)[...].astype(bf16)
        bias = (bf_ref if fwd else bb_ref)[...]
        lo = 0 if fwd else GLA_RANK
        acc_ref = accf_ref if fwd else accb_ref
        sl = pl.ds(base, sup)
        q = q_ref[sl, :].astype(f32) * (GLA_DK ** -0.5)
        k = k_ref[sl, :].astype(f32)
        v = v_ref[sl, :]
        lr = lr_ref[sl, :][:, lo:lo + GLA_RANK]
        z = jnp.dot(lr, w2, preferred_element_type=f32) + bias
        g = (jnp.minimum(z, 0.0) - jnp.log1p(jnp.exp(-jnp.abs(z)))) * (1.0 / GLA_TAU)
        g_hi = g.astype(bf16)
        g_lo = (g - g_hi.astype(f32)).astype(bf16)
        b = (jnp.dot(tri, g_hi, preferred_element_type=f32)
             + jnp.dot(tri, g_lo, preferred_element_type=f32))
        edge = c_len - 1 if fwd else 0
        tot = jnp.concatenate(
            [jnp.broadcast_to(b[c * c_len + edge:c * c_len + edge + 1, :], (c_len, GLA_DK))
             for c in range(GLA_SUPER)], axis=0)
        q_t = (q * jnp.exp(b)).astype(bf16)
        k_t = (k * jnp.exp(-b)).astype(bf16)
        k_s = (k * jnp.exp(tot - b)).astype(bf16)
        dec = jnp.exp(tot)
        scores = jnp.where(smask, _nt_dot(q_t, k_t), 0.0)
        o_intra = jnp.dot(scores.astype(bf16), v, preferred_element_type=f32)
        order = range(GLA_SUPER) if fwd else range(GLA_SUPER - 1, -1, -1)
        for c in order:
            rs = slice(c * c_len, (c + 1) * c_len)
            acc_ref[pl.ds(base + c * c_len, c_len), :] = (
                o_intra[rs, :] + _nt_dot(q_t[rs, :], s_t.astype(bf16)))
            s_t = s_t * dec[c * c_len:c * c_len + 1, :] + _tn_dot(v[rs, :], k_s[rs, :])
        return s_t

    def body(i, carry):
        s_f, s_b = carry
        s_f = half(pl.multiple_of(i * sup, sup), True, s_f)
        s_b = half(pl.multiple_of((n_sup - 1 - i) * sup, sup), False, s_b)
        return s_f, s_b

    zero = jnp.zeros((GLA_DV, GLA_DK), f32)
    lax.fori_loop(0, n_sup, body, (zero, zero))

    rows = 256
    gain = g_ref[...]

    def finish(i, carry):
        sl = pl.ds(pl.multiple_of(i * rows, rows), rows)
        x = accf_ref[sl, :] + accb_ref[sl, :]
        ms = jnp.mean(x * x, axis=-1, keepdims=True)
        y = x * lax.rsqrt(ms + RMS_EPS) * gain
        r = r_ref[sl, :].astype(f32)
        o_ref[sl, :] = (y * (r * jax.nn.sigmoid(r))).astype(o_ref.dtype)
        return carry

    lax.fori_loop(0, t // rows, finish, 0)


def _gla(main, w2f, b_f, w2b, b_b, gain, layer, bsz, t):
    n = bsz * t
    qk_blk = lambda off: (lambda b, h: (b, off // GLA_DK + h))
    v_blk = lambda off: (lambda b, h: (b, off // GLA_DV + h))
    wspec = pl.BlockSpec((None, GLA_RANK, GLA_DK), lambda b, h: (layer, 0, h))
    bspec = pl.BlockSpec((None, 1, GLA_DK), lambda b, h: (layer, 0, h))
    return pl.pallas_call(
        functools.partial(_gla_kernel, t=t),
        grid=(bsz, GLA_HEADS),
        in_specs=[
            pl.BlockSpec((t, GLA_DK), qk_blk(OFF_AQ)),
            pl.BlockSpec((t, GLA_DK), qk_blk(OFF_AK)),
            pl.BlockSpec((t, GLA_DV), v_blk(OFF_AV)),
            pl.BlockSpec((t, GLA_DV), v_blk(OFF_AR)),
            pl.BlockSpec((t, LR_PAD), lambda b, h: (b, OFF_LR // LR_PAD)),
            wspec, bspec, wspec, bspec,
            pl.BlockSpec((None, 1, GLA_DV), lambda b, h: (layer, 0, h)),
        ],
        out_specs=pl.BlockSpec((t, GLA_DV), lambda b, h: (b, h)),
        out_shape=jax.ShapeDtypeStruct((n, A_V), bf16),
        scratch_shapes=[pltpu.VMEM((t, GLA_DV), f32), pltpu.VMEM((t, GLA_DV), f32)],
        compiler_params=_cparams(("parallel", "parallel")),
    )(main, main, main, main, main, w2f, b_f, w2b, b_b, gain)


def _dil_kernel(q_ref, k_ref, v_ref, o_ref, lse_ref, *scratch, t, dil):
    n_sub = t // dil
    bq = min(128, n_sub)
    win = min(bq + 2 * DIL_HALF, n_sub)
    n_blk = n_sub // bq
    qi = lax.broadcasted_iota(jnp.int32, (bq, win), 0)
    ki = lax.broadcasted_iota(jnp.int32, (bq, win), 1)
    cvt = 512

    if dil > 1:
        q32, k32, v32, o32 = scratch

        def widen(i, carry):
            sl = pl.ds(pl.multiple_of(i * cvt, cvt), cvt)
            q32[sl, :] = q_ref[0, sl, :].astype(f32)
            k32[sl, :] = k_ref[0, sl, :].astype(f32)
            v32[sl, :] = v_ref[0, sl, :].astype(f32)
            return carry

        lax.fori_loop(0, t // cvt, widen, 0)

    def body(idx, carry):
        r = idx // n_blk
        i = idx % n_blk
        q0 = i * bq
        k0 = jnp.clip(q0 - DIL_HALF, 0, n_sub - win)
        if dil == 1:
            qs = pl.ds(pl.multiple_of(q0, bq), bq)
            ks = pl.ds(pl.multiple_of(k0, DIL_HALF), win)
            q = q_ref[0, qs, :]
            k = k_ref[0, ks, :]
            v = v_ref[0, ks, :]
        else:
            qs = pl.ds(r + q0 * dil, bq, stride=dil)
            ks = pl.ds(r + k0 * dil, win, stride=dil)
            q = q32[qs, :].astype(bf16)
            k = k32[ks, :].astype(bf16)
            v = v32[ks, :].astype(bf16)
        s = _nt_dot(q, k)
        valid = jnp.abs((q0 + qi) - (k0 + ki)) <= DIL_HALF
        s = jnp.where(valid, s, NEG_BIG)
        m = jnp.max(s, axis=-1, keepdims=True)
        p = jnp.exp(s - m)
        l = jnp.sum(p, axis=-1, keepdims=True)
        o = jnp.dot(p.astype(bf16), v, preferred_element_type=f32) / l
        lse = jnp.broadcast_to(m + jnp.log(l), (bq, HEAD_DIM))
        if dil == 1:
            o_ref[0, qs, :] = o.astype(o_ref.dtype)
            lse_ref[0, qs, :] = lse
        else:
            o32[qs, :] = o
            lse_ref[0, qs, :] = lse
        return carry

    def body_group(j, carry):
        for u in range(DIL_UNROLL):
            body(j * DIL_UNROLL + u, carry)
        return carry

    lax.fori_loop(0, dil * n_blk // DIL_UNROLL, body_group, 0)

    if dil > 1:
        def narrow(i, carry):
            sl = pl.ds(pl.multiple_of(i * cvt, cvt), cvt)
            o_ref[0, sl, :] = o32[sl, :].astype(o_ref.dtype)
            return carry

        lax.fori_loop(0, t // cvt, narrow, 0)


def _dilated(qkv_b, group, bsz, t):
    _, dil = DIL_CONFIGS[group]
    n = bsz * t
    blk = (1, t, HEAD_DIM)
    head0 = group * DIL_HEADS
    scratch = [pltpu.VMEM((t, HEAD_DIM), f32)] * 4 if dil > 1 else []
    return pl.pallas_call(
        functools.partial(_dil_kernel, t=t, dil=dil),
        grid=(bsz, DIL_HEADS),
        in_specs=[
            pl.BlockSpec(blk, lambda b, h: (head0 + h, b, 0)),
            pl.BlockSpec(blk, lambda b, h: (B_HEADS + head0 + h, b, 0)),
            pl.BlockSpec(blk, lambda b, h: (2 * B_HEADS + head0 + h, b, 0)),
        ],
        out_specs=[
            pl.BlockSpec(blk, lambda b, h: (h, b, 0)),
            pl.BlockSpec(blk, lambda b, h: (h, b, 0)),
        ],
        out_shape=[
            jax.ShapeDtypeStruct((DIL_HEADS, n, HEAD_DIM), bf16),
            jax.ShapeDtypeStruct((DIL_HEADS, n, HEAD_DIM), f32),
        ],
        scratch_shapes=scratch,
        compiler_params=_cparams(("parallel", "parallel")),
    )(qkv_b, qkv_b, qkv_b)


def _axial_partner(x):
    lane = lax.broadcasted_iota(jnp.int32, x.shape, 1)
    quarter = HEAD_DIM // 4
    first = (lane % (2 * quarter)) < quarter
    return jnp.where(first, pltpu.roll(x, HEAD_DIM - quarter, 1), pltpu.roll(x, quarter, 1))


def _c_prep_kernel(q_ref, k_ref, cos_ref, sin_ref, qg_ref, kg_ref, qo_ref, ko_ref):
    cos = cos_ref[...]
    sin = sin_ref[...]

    def prep(x, gain, scale):
        x = x.astype(f32)
        ms = jnp.mean(x * x, axis=-1, keepdims=True)
        y = x * lax.rsqrt(ms + RMS_EPS) * gain
        return (y * cos + _axial_partner(y) * sin) * scale

    qg = qg_ref[...]
    kg = kg_ref[...]
    for h in range(GQA_Q_HEADS):
        ls = slice(h * HEAD_DIM, (h + 1) * HEAD_DIM)
        qo_ref[:, ls] = prep(q_ref[:, ls], qg, HEAD_DIM ** -0.5 * LOG2_E).astype(qo_ref.dtype)
    for h in range(GQA_KV_HEADS):
        ls = slice(h * HEAD_DIM, (h + 1) * HEAD_DIM)
        ko_ref[:, ls] = prep(k_ref[:, ls], kg, 1.0).astype(ko_ref.dtype)


def _c_prep(main, cos_c, sin_c, q_gain, k_gain, layer, t):
    n = main.shape[0]
    tm = TM_PROJ
    tpb = t // tm
    gspec = pl.BlockSpec((None, 1, HEAD_DIM), lambda i: (layer, 0, 0))
    return pl.pallas_call(
        _c_prep_kernel,
        grid=(n // tm,),
        in_specs=[
            pl.BlockSpec((tm, C_Q), lambda i: (i, OFF_CQ // C_Q)),
            pl.BlockSpec((tm, C_KV), lambda i: (i, OFF_CK // C_KV)),
            pl.BlockSpec((tm, HEAD_DIM), lambda i: (i % tpb, 0)),
            pl.BlockSpec((tm, HEAD_DIM), lambda i: (i % tpb, 0)),
            gspec, gspec,
        ],
        out_specs=[
            pl.BlockSpec((tm, C_Q), lambda i: (i, 0)),
            pl.BlockSpec((tm, C_KV), lambda i: (i, 0)),
        ],
        out_shape=[
            jax.ShapeDtypeStruct((n, C_Q), bf16),
            jax.ShapeDtypeStruct((n, C_KV), bf16),
        ],
        compiler_params=_cparams(("parallel",)),
    )(main, main, cos_c, sin_c, q_gain, k_gain)


def _gqa_kernel(q_ref, k_ref, v_ref, o_ref, s_scr, p_scr, m_scr, l_scr, acc_scr, *, t, tq, tk):
    grp = GQA_Q_HEADS // GQA_KV_HEADS
    q4 = q_ref[...]
    q = jnp.concatenate([q4[:, h * HEAD_DIM:(h + 1) * HEAD_DIM] for h in range(grp)], axis=0)
    rows = grp * tq
    n_strips = rows // GQA_STRIP
    m_scr[...] = jnp.full((rows, LANES), NEG_BIG, f32)
    l_scr[...] = jnp.zeros((rows, LANES), f32)
    acc_scr[...] = jnp.zeros((rows, HEAD_DIM), f32)

    def widen(col):
        return jnp.broadcast_to(col, (col.shape[0], LANES))

    def chunk(c, carry):
        sl = pl.ds(pl.multiple_of(c * tk, tk), tk)
        s_scr[...] = _nt_dot(q, k_ref[sl, :])
        mx = []
        for i in range(n_strips):
            rs = slice(i * GQA_STRIP, (i + 1) * GQA_STRIP)
            mx.append(widen(jnp.max(s_scr[rs, :], axis=-1, keepdims=True)))
        m_old = m_scr[...]
        m_new = jnp.maximum(m_old, jnp.concatenate(mx, axis=0))
        a = jnp.exp2(m_old - m_new)
        m_scr[...] = m_new
        sums = []
        for i in range(n_strips):
            rs = slice(i * GQA_STRIP, (i + 1) * GQA_STRIP)
            m_wide = jnp.concatenate([m_new[rs, :]] * (tk // LANES), axis=1)
            p = jnp.exp2(s_scr[rs, :] - m_wide)
            sums.append(widen(jnp.sum(p, axis=-1, keepdims=True)))
            p_scr[rs, :] = p.astype(bf16)
        l_scr[...] = a * l_scr[...] + jnp.concatenate(sums, axis=0)
        acc_scr[...] = acc_scr[...] * a + jnp.dot(p_scr[...], v_ref[sl, :], preferred_element_type=f32)
        return carry

    lax.fori_loop(0, t // tk, chunk, 0)
    o = (acc_scr[...] / l_scr[...]).astype(o_ref.dtype)
    o_ref[...] = jnp.concatenate([o[h * tq:(h + 1) * tq, :] for h in range(grp)], axis=1)


def _gqa(qc, kc, main, bsz, t):
    n = bsz * t
    tq = GQA_TQ
    tk = min(t, GQA_TK)
    grp = GQA_Q_HEADS // GQA_KV_HEADS
    grp_cols = grp * HEAD_DIM
    rows = grp * tq
    nq = t // tq
    return pl.pallas_call(
        functools.partial(_gqa_kernel, t=t, tq=tq, tk=tk),
        grid=(bsz, GQA_KV_HEADS, nq),
        in_specs=[
            pl.BlockSpec((tq, grp_cols), lambda b, j, i: (b * nq + i, j)),
            pl.BlockSpec((t, HEAD_DIM), lambda b, j, i: (b, j)),
            pl.BlockSpec((t, HEAD_DIM), lambda b, j, i: (b, OFF_CV // HEAD_DIM + j)),
        ],
        out_specs=pl.BlockSpec((tq, grp_cols), lambda b, j, i: (b * nq + i, j)),
        out_shape=jax.ShapeDtypeStruct((n, C_Q), bf16),
        scratch_shapes=[
            pltpu.VMEM((rows, tk), f32),
            pltpu.VMEM((rows, tk), bf16),
            pltpu.VMEM((rows, LANES), f32),
            pltpu.VMEM((rows, LANES), f32),
            pltpu.VMEM((rows, HEAD_DIM), f32),
        ],
        compiler_params=_cparams(("parallel", "parallel", "parallel")),
    )(qc, kc, main)


def _merge_kernel(oa_ref, ob0_ref, ob1_ref, ob2_ref, l0_ref, l1_ref, l2_ref, oc_ref,
                  ga_ref, gb_ref, gc_ref, wa_ref, wb_ref, wc_ref, o_ref):
    heads = []
    for h in range(DIL_HEADS):
        l0, l1, l2 = l0_ref[h], l1_ref[h], l2_ref[h]
        m = jnp.maximum(jnp.maximum(l0, l1), l2)
        e0, e1, e2 = jnp.exp(l0 - m), jnp.exp(l1 - m), jnp.exp(l2 - m)
        mix = (e0 * ob0_ref[h].astype(f32) + e1 * ob1_ref[h].astype(f32)
               + e2 * ob2_ref[h].astype(f32)) / (e0 + e1 + e2)
        heads.append(mix.astype(bf16))
    o_b = jnp.concatenate(heads, axis=1)
    up_a = jnp.dot(oa_ref[...], wa_ref[...], preferred_element_type=f32)
    up_b = jnp.dot(o_b, wb_ref[...], preferred_element_type=f32)
    up_c = jnp.dot(oc_ref[...], wc_ref[...], preferred_element_type=f32)
    merged = (jax.nn.sigmoid(ga_ref[...].astype(f32)) * up_a
              + jax.nn.sigmoid(gb_ref[...].astype(f32)) * up_b
              + jax.nn.sigmoid(gc_ref[...].astype(f32)) * up_c)
    o_ref[...] = merged.astype(o_ref.dtype)


def _merge(o_a, o_bs, lses, o_c, main, w_up_a, w_up_b, w_up_c, layer):
    n = o_a.shape[0]
    tm = TM_TOK
    gate_blk = OFF_GATE // D_MODEL
    hspec = pl.BlockSpec((DIL_HEADS, tm, HEAD_DIM), lambda i: (0, i, 0))
    gspec = lambda which: pl.BlockSpec((tm, D_MODEL), lambda i: (i, gate_blk + which))
    wspec = lambda rows: pl.BlockSpec((None, rows, D_MODEL), lambda i: (layer, 0, 0))
    return pl.pallas_call(
        _merge_kernel,
        grid=(n // tm,),
        in_specs=[
            pl.BlockSpec((tm, A_V), lambda i: (i, 0)),
            hspec, hspec, hspec, hspec, hspec, hspec,
            pl.BlockSpec((tm, C_Q), lambda i: (i, 0)),
            gspec(0), gspec(1), gspec(2),
            wspec(A_V), wspec(B_OUT), wspec(C_Q),
        ],
        out_specs=pl.BlockSpec((tm, D_MODEL), lambda i: (i, 0)),
        out_shape=jax.ShapeDtypeStruct((n, D_MODEL), bf16),
        compiler_params=_cparams(("parallel",)),
    )(o_a, o_bs[0], o_bs[1], o_bs[2], lses[0], lses[1], lses[2], o_c,
      main, main, main, w_up_a, w_up_b, w_up_c)


def _layer_norm_rows(y, g, b):
    mu = jnp.mean(y, axis=-1, keepdims=True)
    d = y - mu
    var = jnp.mean(d * d, axis=-1, keepdims=True)
    return d * lax.rsqrt(var + LN_EPS) * g + b


def _route_rows(logits_t, bias_col):
    scores = jax.nn.sigmoid(logits_t)
    sel = scores + bias_col
    rows = [sel[e:e + 1, :] for e in range(N_EXPERTS)]
    srow = [scores[e:e + 1, :] for e in range(N_EXPERTS)]
    best_val = None
    best_grp = None
    for g in range(N_EXPERT_GROUPS):
        mem = rows[g * EXPERTS_PER_GROUP:(g + 1) * EXPERTS_PER_GROUP]
        top2 = None
        for a in range(EXPERTS_PER_GROUP):
            for b in range(a + 1, EXPERTS_PER_GROUP):
                pair = mem[a] + mem[b]
                top2 = pair if top2 is None else jnp.maximum(top2, pair)
        if g == 0:
            best_val, best_grp = top2, jnp.zeros_like(top2, dtype=jnp.int32)
        else:
            upd = top2 > best_val
            best_val = jnp.where(upd, top2, best_val)
            best_grp = jnp.where(upd, g, best_grp)
    neg = jnp.full_like(best_val, -jnp.inf)
    cand = [jnp.where(best_grp == (e // EXPERTS_PER_GROUP), rows[e], neg) for e in range(N_EXPERTS)]

    def arg_top(vals):
        bv, bi = vals[0], jnp.zeros_like(best_grp)
        for e in range(1, N_EXPERTS):
            upd = vals[e] > bv
            bv = jnp.where(upd, vals[e], bv)
            bi = jnp.where(upd, e, bi)
        return bi

    idx1 = arg_top(cand)
    idx2 = arg_top([jnp.where(idx1 == e, neg, cand[e]) for e in range(N_EXPERTS)])
    zero = jnp.zeros_like(best_val)
    s1 = zero
    s2 = zero
    for e in range(N_EXPERTS):
        s1 = s1 + jnp.where(idx1 == e, srow[e], zero)
        s2 = s2 + jnp.where(idx2 == e, srow[e], zero)
    tot = s1 + s2
    return idx1, idx2, s1 / tot, s2 / tot


def _outproj_kernel(m_ref, w_ref, x_ref, g_ref, b_ref, wr_ref, x1_ref, x1b_ref, lg_ref):
    h = jnp.dot(m_ref[...], w_ref[...], preferred_element_type=f32)
    x1 = _layer_norm_rows(DEEPNORM_ALPHA * x_ref[...] + h, g_ref[...], b_ref[...])
    x1_ref[...] = x1
    x_hi = x1.astype(bf16)
    x1b_ref[...] = x_hi
    x_lo = (x1 - x_hi.astype(f32)).astype(bf16)
    wr = wr_ref[...]
    w_hi = wr.astype(bf16)
    w_lo = (wr - w_hi.astype(f32)).astype(bf16)
    lg_ref[...] = (jnp.dot(x_hi, w_hi, preferred_element_type=f32)
                   + jnp.dot(x_hi, w_lo, preferred_element_type=f32)
                   + jnp.dot(x_lo, w_hi, preferred_element_type=f32))


def _outproj_ln(merged, w_out, x, ln_g, ln_b, w_router_pad, layer):
    n = x.shape[0]
    tm = TM_TOK
    vspec = pl.BlockSpec((None, 1, D_MODEL), lambda i: (layer, 0, 0))
    row = pl.BlockSpec((tm, D_MODEL), lambda i: (i, 0))
    return pl.pallas_call(
        _outproj_kernel,
        grid=(n // tm,),
        in_specs=[
            row,
            pl.BlockSpec((None, D_MODEL, D_MODEL), lambda i: (layer, 0, 0)),
            row,
            vspec, vspec,
            pl.BlockSpec((D_MODEL, LANES), lambda i: (0, 0)),
        ],
        out_specs=[row, row, pl.BlockSpec((tm, LANES), lambda i: (i, 0))],
        out_shape=[
            jax.ShapeDtypeStruct((n, D_MODEL), f32),
            jax.ShapeDtypeStruct((n, D_MODEL), bf16),
            jax.ShapeDtypeStruct((n, LANES), f32),
        ],
        compiler_params=_cparams(("parallel",)),
    )(merged, w_out, x, ln_g, ln_b, w_router_pad)


def _route_kernel(lg_ref, rb_ref, idx_ref, w1_ref, w2_ref):
    logits_t = lg_ref[...].T[0:N_EXPERTS, :]
    i1, i2, w1, w2 = _route_rows(logits_t, rb_ref[...])
    idx_ref[0:1, :] = i1
    idx_ref[1:2, :] = i2
    tr = logits_t.shape[1]
    w1_ref[...] = jnp.broadcast_to(w1, (LANES, tr)).T
    w2_ref[...] = jnp.broadcast_to(w2, (LANES, tr)).T


def _route(logits, router_bias_col):
    n = logits.shape[0]
    tr = TM_ROUTE
    return pl.pallas_call(
        _route_kernel,
        grid=(n // tr,),
        in_specs=[
            pl.BlockSpec((tr, LANES), lambda i: (i, 0)),
            pl.BlockSpec((N_EXPERTS, 1), lambda i: (0, 0)),
        ],
        out_specs=[
            pl.BlockSpec((2, tr), lambda i: (0, i)),
            pl.BlockSpec((tr, LANES), lambda i: (i, 0)),
            pl.BlockSpec((tr, LANES), lambda i: (i, 0)),
        ],
        out_shape=[
            jax.ShapeDtypeStruct((2, n), jnp.int32),
            jax.ShapeDtypeStruct((n, LANES), f32),
            jax.ShapeDtypeStruct((n, LANES), f32),
        ],
        compiler_params=_cparams(("parallel",)),
    )(logits, router_bias_col)


def _moe_kernel(te_ref, nt_ref, x_ref, wg_ref, wu_ref, wd_ref, o_ref):
    del te_ref
    tile = pl.program_id(0)

    @pl.when(tile < nt_ref[0])
    def _():
        x = x_ref[...]
        gate = jnp.dot(x, wg_ref[...], preferred_element_type=f32)
        up = jnp.dot(x, wu_ref[...], preferred_element_type=f32)
        h = (gate * jax.nn.sigmoid(gate) * up).astype(bf16)
        o_ref[...] = jnp.dot(h, wd_ref[...], preferred_element_type=f32).astype(o_ref.dtype)

    @pl.when(tile >= nt_ref[0])
    def _():
        o_ref[...] = jnp.zeros_like(o_ref)


def _moe_grouped(x_sorted, tile_expert, n_tiles_used, w_gate, w_up, w_down, layer):
    m_pad = x_sorted.shape[0]
    n_tiles = m_pad // TM_MOE
    wspec_in = pl.BlockSpec((None, None, D_MODEL, D_FF_EXPERT), lambda t, te, nt: (layer, te[t], 0, 0))
    wspec_out = pl.BlockSpec((None, None, D_FF_EXPERT, D_MODEL), lambda t, te, nt: (layer, te[t], 0, 0))
    grid_spec = pltpu.PrefetchScalarGridSpec(
        num_scalar_prefetch=2,
        grid=(n_tiles,),
        in_specs=[
            pl.BlockSpec((TM_MOE, D_MODEL), lambda t, te, nt: (t, 0)),
            wspec_in, wspec_in, wspec_out,
        ],
        out_specs=pl.BlockSpec((TM_MOE, D_MODEL), lambda t, te, nt: (t, 0)),
    )
    return pl.pallas_call(
        _moe_kernel,
        grid_spec=grid_spec,
        out_shape=jax.ShapeDtypeStruct((m_pad, D_MODEL), bf16),
        compiler_params=_cparams(("arbitrary",)),
    )(tile_expert, n_tiles_used, x_sorted, w_gate, w_up, w_down)


def _combine_kernel(x_ref, y1_ref, y2_ref, w1_ref, w2_ref, g_ref, b_ref, o_ref, ob_ref):
    w1 = w1_ref[...]
    w2 = w2_ref[...]
    for c in range(D_MODEL // LANES):
        ls = slice(c * LANES, (c + 1) * LANES)
        o_ref[:, ls] = (DEEPNORM_ALPHA * x_ref[:, ls] + w1 * y1_ref[:, ls].astype(f32)
                        + w2 * y2_ref[:, ls].astype(f32))
    x2 = _layer_norm_rows(o_ref[...], g_ref[...], b_ref[...])
    o_ref[...] = x2
    ob_ref[...] = x2.astype(bf16)


def _combine_ln(x1, y1, y2, w1b, w2b, ln_g, ln_b, layer):
    n = x1.shape[0]
    tm = TM_TOK
    row = pl.BlockSpec((tm, D_MODEL), lambda i: (i, 0))
    wsp = pl.BlockSpec((tm, LANES), lambda i: (i, 0))
    vspec = pl.BlockSpec((None, 1, D_MODEL), lambda i: (layer, 0, 0))
    return pl.pallas_call(
        _combine_kernel,
        grid=(n // tm,),
        in_specs=[row, row, row, wsp, wsp, vspec, vspec],
        out_specs=[row, row],
        out_shape=[jax.ShapeDtypeStruct((n, D_MODEL), f32), jax.ShapeDtypeStruct((n, D_MODEL), bf16)],
        compiler_params=_cparams(("parallel",)),
    )(x1, y1, y2, w1b, w2b, ln_g, ln_b)


def _dispatch_plan(idx):
    n = idx.shape[1]
    m_pad = 2 * n + N_EXPERTS * TM_MOE
    n_tiles = m_pad // TM_MOE
    e_flat = idx.reshape(-1)
    onehot = (e_flat[:, None] == jnp.arange(N_EXPERTS, dtype=jnp.int32)[None, :]).astype(jnp.int32)
    rank = jnp.sum((jnp.cumsum(onehot, axis=0) - onehot) * onehot, axis=1)
    counts = jnp.sum(onehot, axis=0)
    tiles_per = (counts + TM_MOE - 1) // TM_MOE
    tile_end = jnp.cumsum(tiles_per)
    start_row = (tile_end - tiles_per) * TM_MOE
    dest = start_row.at[e_flat].get(mode="promise_in_bounds") + rank
    tok = jnp.tile(jnp.arange(n, dtype=jnp.int32), 2)
    src = jnp.zeros((m_pad,), jnp.int32).at[dest].set(tok, mode="promise_in_bounds", unique_indices=True)
    n_used = tile_end[-1]
    tile_ids = jnp.minimum(jnp.arange(n_tiles, dtype=jnp.int32), n_used - 1)
    tile_expert = jnp.sum((tile_ids[:, None] >= tile_end[None, :]).astype(jnp.int32), axis=1)
    tile_expert = jnp.minimum(tile_expert, N_EXPERTS - 1)
    return src, dest, tile_expert, n_used.reshape(1).astype(jnp.int32)


def _rope_tables(pos, dim):
    inv_freq = ROPE_THETA ** (-jnp.arange(0, dim, 2, dtype=f32) / dim)
    ang = pos.astype(f32)[:, None] * inv_freq[None, :]
    return jnp.cos(ang), jnp.sin(ang)


def _tables(t):
    cos, sin = _rope_tables(jnp.arange(t), HEAD_DIM)
    cos_b = jnp.concatenate([cos, cos], axis=-1)
    sin_b = jnp.concatenate([-sin, sin], axis=-1)
    tok = jnp.arange(t)
    cr, sr = _rope_tables(tok // GRID_W, HEAD_DIM // 2)
    cc, sc = _rope_tables(tok % GRID_W, HEAD_DIM // 2)
    cos_c = jnp.concatenate([cr, cr, cc, cc], axis=-1)
    sin_c = jnp.concatenate([-sr, sr, -sc, sc], axis=-1)
    return cos_b, sin_b, cos_c, sin_c


def _repack_w_in(w_in):
    offs = [0]
    for s in IN_SPLITS:
        offs.append(offs[-1] + s)
    seg = [w_in[:, :, offs[i]:offs[i + 1]] for i in range(len(IN_SPLITS))]
    a_q, a_k, a_v, a_r, a_lr, b_qkv, c_q, c_k, c_v, gate = seg
    lr_pad = jnp.pad(a_lr, ((0, 0), (0, 0), (0, LR_PAD - A_LR)))
    w_main = jnp.concatenate([gate, a_q, a_k, a_v, a_r, c_q, c_k, c_v, lr_pad], axis=-1)
    w_main = jnp.pad(w_main, ((0, 0), (0, 0), (0, MAIN_COLS - MAIN_USED))).astype(bf16)
    return w_main, b_qkv.astype(bf16)


def _mixer(xb, wts, layer, bsz, t, tables):
    cos_b, sin_b, cos_c, sin_c = tables
    main = _proj_main(xb, wts["w_main"], layer)
    qkv_b = _proj_b(xb, wts["w_b"], cos_b, sin_b, layer, t)
    o_a = _gla(main, wts["gla_w2_f"], wts["gla_b_f"], wts["gla_w2_b"], wts["gla_b_b"],
               wts["gla_norm_g"], layer, bsz, t)
    o_bs, lses = [], []
    for group in range(len(DIL_CONFIGS)):
        o_g, lse_g = _dilated(qkv_b, group, bsz, t)
        o_bs.append(o_g)
        lses.append(lse_g)
    qc, kc = _c_prep(main, cos_c, sin_c, wts["q_norm_g"], wts["k_norm_g"], layer, t)
    o_c = _gqa(qc, kc, main, bsz, t)
    return _merge(o_a, o_bs, lses, o_c, main, wts["w_up_a"], wts["w_up_b"], wts["w_up_c"], layer)


def _gather_rows(a, rows):
    return a.at[rows].get(mode="promise_in_bounds")


def _moe(x1, x1b, idx, w1b, w2b, wts, layer):
    n = x1.shape[0]
    src, dest, tile_expert, n_used = _dispatch_plan(idx)
    x_sorted = _gather_rows(x1b, src)
    y_sorted = _moe_grouped(x_sorted, tile_expert, n_used, wts["moe_w_gate"], wts["moe_w_up"],
                            wts["moe_w_down"], layer)
    y1 = _gather_rows(y_sorted, dest[:n])
    y2 = _gather_rows(y_sorted, dest[n:])
    return _combine_ln(x1, y1, y2, w1b, w2b, wts["ln2_g"], wts["ln2_b"], layer)


def _trunk(x3, wts, depth=DEPTH):
    bsz, t, _ = x3.shape
    n = bsz * t
    x = x3.reshape(n, D_MODEL)
    xb = x.astype(bf16)
    tables = _tables(t)
    for layer in range(depth):
        merged = _mixer(xb, wts, layer, bsz, t, tables)
        x1, x1b, logits = _outproj_ln(merged, wts["w_out"], x, wts["ln1_g"], wts["ln1_b"],
                                      wts["w_router_pad"], layer)
        idx, w1b, w2b = _route(logits, wts["router_bias_col"])
        x, xb = _moe(x1, x1b, idx, w1b, w2b, wts, layer)
    return x.reshape(bsz, t, D_MODEL)


def _prepare_weights(w_in, gla_w2_f, gla_b_f, gla_w2_b, gla_b_b, gla_norm_g, q_norm_g, k_norm_g,
                     w_up_a, w_up_b, w_up_c, w_out, ln1_g, ln1_b, w_router, router_bias,
                     moe_w_gate, moe_w_up, moe_w_down, ln2_g, ln2_b):
    w_main, w_b = _repack_w_in(w_in)
    row = lambda a: a.reshape(DEPTH, 1, a.shape[-1]).astype(f32)
    return {
        "w_main": w_main,
        "w_b": w_b,
        "gla_w2_f": gla_w2_f.astype(f32),
        "gla_b_f": row(gla_b_f),
        "gla_w2_b": gla_w2_b.astype(f32),
        "gla_b_b": row(gla_b_b),
        "gla_norm_g": row(gla_norm_g),
        "q_norm_g": row(q_norm_g),
        "k_norm_g": row(k_norm_g),
        "w_up_a": w_up_a.astype(bf16),
        "w_up_b": w_up_b.astype(bf16),
        "w_up_c": w_up_c.astype(bf16),
        "w_out": w_out.astype(bf16),
        "ln1_g": row(ln1_g),
        "ln1_b": row(ln1_b),
        "w_router_pad": jnp.pad(w_router.astype(f32), ((0, 0), (0, LANES - N_EXPERTS))),
        "router_bias_col": router_bias.astype(f32).reshape(N_EXPERTS, 1),
        "moe_w_gate": moe_w_gate.astype(bf16),
        "moe_w_up": moe_w_up.astype(bf16),
        "moe_w_down": moe_w_down.astype(bf16),
        "ln2_g": row(ln2_g),
        "ln2_b": row(ln2_b),
    }


def kernel(x_prompt, x_sample, w_in, gla_w2_f, gla_b_f, gla_w2_b, gla_b_b, gla_norm_g, q_norm_g, k_norm_g, w_up_a, w_up_b, w_up_c, w_out, ln1_g, ln1_b, w_router, router_bias, moe_w_gate, moe_w_up, moe_w_down, ln2_g, ln2_b):
    wts = _prepare_weights(w_in, gla_w2_f, gla_b_f, gla_w2_b, gla_b_b, gla_norm_g, q_norm_g, k_norm_g,
                           w_up_a, w_up_b, w_up_c, w_out, ln1_g, ln1_b, w_router, router_bias,
                           moe_w_gate, moe_w_up, moe_w_down, ln2_g, ln2_b)
    return _trunk(x_prompt, wts), _trunk(x_sample, wts)
```

```python
import functools

import jax
import jax.numpy as jnp
from jax import lax
from jax.experimental import pallas as pl
from jax.experimental.pallas import tpu as pltpu

D_MODEL = 2048
DEPTH = 4
HEAD_DIM = 128
GRID_W = 64
ROPE_THETA = 10000.0
LN_EPS = 1e-5
RMS_EPS = 1e-6
GLA_HEADS = 4
GLA_DK = 128
GLA_DV = 256
GLA_RANK = 16
GLA_TAU = 16.0
GLA_CHUNK = 64
DIL_CONFIGS = ((128, 1), (512, 4), (2048, 16))
DIL_HEADS = 4
GQA_Q_HEADS = 8
GQA_KV_HEADS = 2
N_EXPERTS = 16
N_EXPERT_GROUPS = 4
EXPERTS_PER_GROUP = N_EXPERTS // N_EXPERT_GROUPS
D_FF_EXPERT = 1024
DEEPNORM_ALPHA = (2.0 * DEPTH) ** 0.25

A_QK = GLA_HEADS * GLA_DK
A_V = GLA_HEADS * GLA_DV
A_LR = 2 * GLA_RANK
B_HEADS = len(DIL_CONFIGS) * DIL_HEADS
B_QKV = 3 * B_HEADS * HEAD_DIM
B_OUT = DIL_HEADS * HEAD_DIM
C_Q = GQA_Q_HEADS * HEAD_DIM
C_KV = GQA_KV_HEADS * HEAD_DIM
GATE_COLS = 3 * D_MODEL
IN_SPLITS = (A_QK, A_QK, A_V, A_V, A_LR, B_QKV, C_Q, C_KV, C_KV, GATE_COLS)

LANES = 128
VMEM_LIMIT = 56 * 1024 * 1024
LR_PAD = LANES

OFF_GATE = 0
OFF_AQ = OFF_GATE + GATE_COLS
OFF_AK = OFF_AQ + A_QK
OFF_AV = OFF_AK + A_QK
OFF_AR = OFF_AV + A_V
OFF_CQ = OFF_AR + A_V
OFF_CK = OFF_CQ + C_Q
OFF_CV = OFF_CK + C_KV
OFF_LR = OFF_CV + C_KV
MAIN_USED = OFF_LR + LR_PAD

TM_PROJ = 1024
TN_MAIN = 1024
TM_ROUTE = 2048
MAIN_COLS = -(-MAIN_USED // TN_MAIN) * TN_MAIN
TN_B = B_HEADS * HEAD_DIM
TM_TOK = 256
TM_MOE = 512
GQA_TQ = 256
GQA_TK = 1024
GQA_GROUP = 2
GQA_STRIP = 32
LOG2_E = 1.4426950408889634
GLA_SUPER = 8
DIL_UNROLL = 4
DIL_HALF = 64
NEG_BIG = -1e30

f32 = jnp.float32
bf16 = jnp.bfloat16


def _cparams(sem):
    return pltpu.CompilerParams(dimension_semantics=sem, vmem_limit_bytes=VMEM_LIMIT)


def _nt_dot(a, b):
    return lax.dot_general(a, b, (((1,), (1,)), ((), ())), preferred_element_type=f32)


def _tn_dot(a, b):
    return lax.dot_general(a, b, (((0,), (0,)), ((), ())), preferred_element_type=f32)


def _mm_kernel(x_ref, w_ref, o_ref):
    o_ref[...] = jnp.dot(x_ref[...], w_ref[...], preferred_element_type=f32).astype(o_ref.dtype)


def _proj_main(xb, w_main, layer):
    n, k = xb.shape
    grid = (MAIN_COLS // TN_MAIN, n // TM_PROJ)
    return pl.pallas_call(
        _mm_kernel,
        grid=grid,
        in_specs=[
            pl.BlockSpec((TM_PROJ, k), lambda j, i: (i, 0)),
            pl.BlockSpec((None, k, TN_MAIN), lambda j, i: (layer, 0, j)),
        ],
        out_specs=pl.BlockSpec((TM_PROJ, TN_MAIN), lambda j, i: (i, j)),
        out_shape=jax.ShapeDtypeStruct((n, MAIN_COLS), bf16),
        compiler_params=_cparams(("parallel", "parallel")),
    )(xb, w_main)


def _proj_b_kernel(x_ref, w_ref, cos_ref, sin_ref, o_ref):
    j = pl.program_id(0)
    pair = 2 * HEAD_DIM

    def emit(rotary, scale):
        x = x_ref[...]
        if rotary:
            cos = cos_ref[...] * scale
            sin = sin_ref[...] * scale
        for c in range(TN_B // pair):
            acc = jnp.dot(x, w_ref[:, c * pair:(c + 1) * pair], preferred_element_type=f32)
            for h in range(2):
                a = acc[:, h * HEAD_DIM:(h + 1) * HEAD_DIM]
                if rotary:
                    a = a * cos + pltpu.roll(a, HEAD_DIM // 2, 1) * sin
                o_ref[2 * c + h] = a.astype(o_ref.dtype)

    @pl.when(j == 0)
    def _():
        emit(True, HEAD_DIM ** -0.5)

    @pl.when(j == 1)
    def _():
        emit(True, 1.0)

    @pl.when(j == 2)
    def _():
        emit(False, 1.0)


def _proj_b(xb, w_b, cos_b, sin_b, layer, t):
    n, k = xb.shape
    tpb = t // TM_PROJ
    grid = (B_QKV // TN_B, n // TM_PROJ)
    hpt = TN_B // HEAD_DIM
    return pl.pallas_call(
        _proj_b_kernel,
        grid=grid,
        in_specs=[
            pl.BlockSpec((TM_PROJ, k), lambda j, i: (i, 0)),
            pl.BlockSpec((None, k, TN_B), lambda j, i: (layer, 0, j)),
            pl.BlockSpec((TM_PROJ, HEAD_DIM), lambda j, i: (i % tpb, 0)),
            pl.BlockSpec((TM_PROJ, HEAD_DIM), lambda j, i: (i % tpb, 0)),
        ],
        out_specs=pl.BlockSpec((hpt, TM_PROJ, HEAD_DIM), lambda j, i: (j, i, 0)),
        out_shape=jax.ShapeDtypeStruct((3 * B_HEADS, n, HEAD_DIM), bf16),
        compiler_params=_cparams(("parallel", "parallel")),
    )(xb, w_b, cos_b, sin_b)


def _gla_kernel(q_ref, k_ref, v_ref, r_ref, lr_ref, w2f_ref, bf_ref, w2b_ref, bb_ref, g_ref,
                o_ref, accf_ref, accb_ref, *, t):
    c_len = GLA_CHUNK
    sup = GLA_SUPER * c_len
    n_sup = t // sup
    row = lax.broadcasted_iota(jnp.int32, (sup, sup), 0)
    col = lax.broadcasted_iota(jnp.int32, (sup, sup), 1)
    same = (row // c_len) == (col // c_len)

    def half(base, fwd, s_t):
        tri = jnp.where(same & ((row >= col) if fwd else (row <= col)), 1.0, 0.0).astype(bf16)
        smask = same & ((col <= row) if fwd else (col > row))
        w2 = (w2f_ref if fwd else w2b_ref)[...].astype(bf16)
        bias = (bf_ref if fwd else bb_ref)[...]
        lo = 0 if fwd else GLA_RANK
        acc_ref = accf_ref if fwd else accb_ref
        sl = pl.ds(base, sup)
        q = q_ref[sl, :].astype(f32) * (GLA_DK ** -0.5)
        k = k_ref[sl, :].astype(f32)
        v = v_ref[sl, :]
        lr = lr_ref[sl, :][:, lo:lo + GLA_RANK]
        z = jnp.dot(lr, w2, preferred_element_type=f32) + bias
        g = (jnp.minimum(z, 0.0) - jnp.log1p(jnp.exp(-jnp.abs(z)))) * (1.0 / GLA_TAU)
        g_hi = g.astype(bf16)
        g_lo = (g - g_hi.astype(f32)).astype(bf16)
        b = (jnp.dot(tri, g_hi, preferred_element_type=f32)
             + jnp.dot(tri, g_lo, preferred_element_type=f32))
        edge = c_len - 1 if fwd else 0
        tot = jnp.concatenate(
            [jnp.broadcast_to(b[c * c_len + edge:c * c_len + edge + 1, :], (c_len, GLA_DK))
             for c in range(GLA_SUPER)], axis=0)
        q_t = (q * jnp.exp(b)).astype(bf16)
        k_t = (k * jnp.exp(-b)).astype(bf16)
        k_s = (k * jnp.exp(tot - b)).astype(bf16)
        dec = jnp.exp(tot)
        scores = jnp.where(smask, _nt_dot(q_t, k_t), 0.0)
        o_intra = jnp.dot(scores.astype(bf16), v, preferred_element_type=f32)
        order = range(GLA_SUPER) if fwd else range(GLA_SUPER - 1, -1, -1)
        for c in order:
            rs = slice(c * c_len, (c + 1) * c_len)
            acc_ref[pl.ds(base + c * c_len, c_len), :] = (
                o_intra[rs, :] + _nt_dot(q_t[rs, :], s_t.astype(bf16)))
            s_t = s_t * dec[c * c_len:c * c_len + 1, :] + _tn_dot(v[rs, :], k_s[rs, :])
        return s_t

    def body(i, carry):
        s_f, s_b = carry
        s_f = half(pl.multiple_of(i * sup, sup), True, s_f)
        s_b = half(pl.multiple_of((n_sup - 1 - i) * sup, sup), False, s_b)
        return s_f, s_b

    zero = jnp.zeros((GLA_DV, GLA_DK), f32)
    lax.fori_loop(0, n_sup, body, (zero, zero))

    rows = 256
    gain = g_ref[...]

    def finish(i, carry):
        sl = pl.ds(pl.multiple_of(i * rows, rows), rows)
        x = accf_ref[sl, :] + accb_ref[sl, :]
        ms = jnp.mean(x * x, axis=-1, keepdims=True)
        y = x * lax.rsqrt(ms + RMS_EPS) * gain
        r = r_ref[sl, :].astype(f32)
        o_ref[sl, :] = (y * (r * jax.nn.sigmoid(r))).astype(o_ref.dtype)
        return carry

    lax.fori_loop(0, t // rows, finish, 0)


def _gla(main, w2f, b_f, w2b, b_b, gain, layer, bsz, t):
    n = bsz * t
    qk_blk = lambda off: (lambda b, h: (b, off // GLA_DK + h))
    v_blk = lambda off: (lambda b, h: (b, off // GLA_DV + h))
    wspec = pl.BlockSpec((None, GLA_RANK, GLA_DK), lambda b, h: (layer, 0, h))
    bspec = pl.BlockSpec((None, 1, GLA_DK), lambda b, h: (layer, 0, h))
    return pl.pallas_call(
        functools.partial(_gla_kernel, t=t),
        grid=(bsz, GLA_HEADS),
        in_specs=[
            pl.BlockSpec((t, GLA_DK), qk_blk(OFF_AQ)),
            pl.BlockSpec((t, GLA_DK), qk_blk(OFF_AK)),
            pl.BlockSpec((t, GLA_DV), v_blk(OFF_AV)),
            pl.BlockSpec((t, GLA_DV), v_blk(OFF_AR)),
            pl.BlockSpec((t, LR_PAD), lambda b, h: (b, OFF_LR // LR_PAD)),
            wspec, bspec, wspec, bspec,
            pl.BlockSpec((None, 1, GLA_DV), lambda b, h: (layer, 0, h)),
        ],
        out_specs=pl.BlockSpec((t, GLA_DV), lambda b, h: (b, h)),
        out_shape=jax.ShapeDtypeStruct((n, A_V), bf16),
        scratch_shapes=[pltpu.VMEM((t, GLA_DV), f32), pltpu.VMEM((t, GLA_DV), f32)],
        compiler_params=_cparams(("parallel", "parallel")),
    )(main, main, main, main, main, w2f, b_f, w2b, b_b, gain)


def _dil_kernel(q_ref, k_ref, v_ref, o_ref, lse_ref, *scratch, t, dil):
    n_sub = t // dil
    bq = min(128, n_sub)
    win = min(bq + 2 * DIL_HALF, n_sub)
    n_blk = n_sub // bq
    qi = lax.broadcasted_iota(jnp.int32, (bq, win), 0)
    ki = lax.broadcasted_iota(jnp.int32, (bq, win), 1)
    cvt = 512

    if dil > 1:
        q32, k32, v32, o32 = scratch

        def widen(i, carry):
            sl = pl.ds(pl.multiple_of(i * cvt, cvt), cvt)
            q32[sl, :] = q_ref[0, sl, :].astype(f32)
            k32[sl, :] = k_ref[0, sl, :].astype(f32)
            v32[sl, :] = v_ref[0, sl, :].astype(f32)
            return carry

        lax.fori_loop(0, t // cvt, widen, 0)

    def body(idx, carry):
        r = idx // n_blk
        i = idx % n_blk
        q0 = i * bq
        k0 = jnp.clip(q0 - DIL_HALF, 0, n_sub - win)
        if dil == 1:
            qs = pl.ds(pl.multiple_of(q0, bq), bq)
            ks = pl.ds(pl.multiple_of(k0, DIL_HALF), win)
            q = q_ref[0, qs, :]
            k = k_ref[0, ks, :]
            v = v_ref[0, ks, :]
        else:
            qs = pl.ds(r + q0 * dil, bq, stride=dil)
            ks = pl.ds(r + k0 * dil, win, stride=dil)
            q = q32[qs, :].astype(bf16)
            k = k32[ks, :].astype(bf16)
            v = v32[ks, :].astype(bf16)
        s = _nt_dot(q, k)
        valid = jnp.abs((q0 + qi) - (k0 + ki)) <= DIL_HALF
        s = jnp.where(valid, s, NEG_BIG)
        m = jnp.max(s, axis=-1, keepdims=True)
        p = jnp.exp(s - m)
        l = jnp.sum(p, axis=-1, keepdims=True)
        o = jnp.dot(p.astype(bf16), v, preferred_element_type=f32) / l
        lse = jnp.broadcast_to(m + jnp.log(l), (bq, HEAD_DIM))
        if dil == 1:
            o_ref[0, qs, :] = o.astype(o_ref.dtype)
            lse_ref[0, qs, :] = lse
        else:
            o32[qs, :] = o
            lse_ref[0, qs, :] = lse
        return carry

    def body_group(j, carry):
        for u in range(DIL_UNROLL):
            body(j * DIL_UNROLL + u, carry)
        return carry

    lax.fori_loop(0, dil * n_blk // DIL_UNROLL, body_group, 0)

    if dil > 1:
        def narrow(i, carry):
            sl = pl.ds(pl.multiple_of(i * cvt, cvt), cvt)
            o_ref[0, sl, :] = o32[sl, :].astype(o_ref.dtype)
            return carry

        lax.fori_loop(0, t // cvt, narrow, 0)


def _dilated(qkv_b, group, bsz, t):
    _, dil = DIL_CONFIGS[group]
    n = bsz * t
    blk = (1, t, HEAD_DIM)
    head0 = group * DIL_HEADS
    scratch = [pltpu.VMEM((t, HEAD_DIM), f32)] * 4 if dil > 1 else []
    return pl.pallas_call(
        functools.partial(_dil_kernel, t=t, dil=dil),
        grid=(bsz, DIL_HEADS),
        in_specs=[
            pl.BlockSpec(blk, lambda b, h: (head0 + h, b, 0)),
            pl.BlockSpec(blk, lambda b, h: (B_HEADS + head0 + h, b, 0)),
            pl.BlockSpec(blk, lambda b, h: (2 * B_HEADS + head0 + h, b, 0)),
        ],
        out_specs=[
            pl.BlockSpec(blk, lambda b, h: (h, b, 0)),
            pl.BlockSpec(blk, lambda b, h: (h, b, 0)),
        ],
        out_shape=[
            jax.ShapeDtypeStruct((DIL_HEADS, n, HEAD_DIM), bf16),
            jax.ShapeDtypeStruct((DIL_HEADS, n, HEAD_DIM), f32),
        ],
        scratch_shapes=scratch,
        compiler_params=_cparams(("parallel", "parallel")),
    )(qkv_b, qkv_b, qkv_b)


def _axial_partner(x):
    lane = lax.broadcasted_iota(jnp.int32, x.shape, 1)
    quarter = HEAD_DIM // 4
    first = (lane % (2 * quarter)) < quarter
    return jnp.where(first, pltpu.roll(x, HEAD_DIM - quarter, 1), pltpu.roll(x, quarter, 1))


def _c_prep_kernel(q_ref, k_ref, cos_ref, sin_ref, qg_ref, kg_ref, qo_ref, ko_ref):
    cos = cos_ref[...]
    sin = sin_ref[...]

    def prep(x, gain, scale):
        x = x.astype(f32)
        ms = jnp.mean(x * x, axis=-1, keepdims=True)
        y = x * lax.rsqrt(ms + RMS_EPS) * gain
        return (y * cos + _axial_partner(y) * sin) * scale

    qg = qg_ref[...]
    kg = kg_ref[...]
    for h in range(GQA_Q_HEADS):
        ls = slice(h * HEAD_DIM, (h + 1) * HEAD_DIM)
        qo_ref[:, ls] = prep(q_ref[:, ls], qg, HEAD_DIM ** -0.5 * LOG2_E).astype(qo_ref.dtype)
    for h in range(GQA_KV_HEADS):
        ls = slice(h * HEAD_DIM, (h + 1) * HEAD_DIM)
        ko_ref[:, ls] = prep(k_ref[:, ls], kg, 1.0).astype(ko_ref.dtype)


def _c_prep(main, cos_c, sin_c, q_gain, k_gain, layer, t):
    n = main.shape[0]
    tm = TM_PROJ
    tpb = t // tm
    gspec = pl.BlockSpec((None, 1, HEAD_DIM), lambda i: (layer, 0, 0))
    return pl.pallas_call(
        _c_prep_kernel,
        grid=(n // tm,),
        in_specs=[
            pl.BlockSpec((tm, C_Q), lambda i: (i, OFF_CQ // C_Q)),
            pl.BlockSpec((tm, C_KV), lambda i: (i, OFF_CK // C_KV)),
            pl.BlockSpec((tm, HEAD_DIM), lambda i: (i % tpb, 0)),
            pl.BlockSpec((tm, HEAD_DIM), lambda i: (i % tpb, 0)),
            gspec, gspec,
        ],
        out_specs=[
            pl.BlockSpec((tm, C_Q), lambda i: (i, 0)),
            pl.BlockSpec((tm, C_KV), lambda i: (i, 0)),
        ],
        out_shape=[
            jax.ShapeDtypeStruct((n, C_Q), bf16),
            jax.ShapeDtypeStruct((n, C_KV), bf16),
        ],
        compiler_params=_cparams(("parallel",)),
    )(main, main, cos_c, sin_c, q_gain, k_gain)


def _gqa_kernel(q_ref, k_ref, v_ref, o_ref, s_scr, p_scr, m_scr, l_scr, acc_scr, *, t, tq, tk):
    grp = GQA_Q_HEADS // GQA_KV_HEADS
    q4 = q_ref[...]
    q = jnp.concatenate([q4[:, h * HEAD_DIM:(h + 1) * HEAD_DIM] for h in range(grp)], axis=0)
    rows = grp * tq
    n_strips = rows // GQA_STRIP
    m_scr[...] = jnp.full((rows, LANES), NEG_BIG, f32)
    l_scr[...] = jnp.zeros((rows, LANES), f32)
    acc_scr[...] = jnp.zeros((rows, HEAD_DIM), f32)

    def widen(col):
        return jnp.broadcast_to(col, (col.shape[0], LANES))

    def softmax_chunk(buf):
        mx = []
        for i in range(n_strips):
            rs = slice(i * GQA_STRIP, (i + 1) * GQA_STRIP)
            mx.append(widen(jnp.max(s_scr[buf, rs, :], axis=-1, keepdims=True)))
        m_old = m_scr[...]
        m_new = jnp.maximum(m_old, jnp.concatenate(mx, axis=0))
        a = jnp.exp2(m_old - m_new)
        m_scr[...] = m_new
        sums = []
        for i in range(n_strips):
            rs = slice(i * GQA_STRIP, (i + 1) * GQA_STRIP)
            m_wide = jnp.concatenate([m_new[rs, :]] * (tk // LANES), axis=1)
            p = jnp.exp2(s_scr[buf, rs, :] - m_wide)
            sums.append(widen(jnp.sum(p, axis=-1, keepdims=True)))
            p_scr[buf, rs, :] = p.astype(bf16)
        l_scr[...] = a * l_scr[...] + jnp.concatenate(sums, axis=0)
        return a

    def chunk_group(cg, carry):
        sls = []
        for u in range(GQA_GROUP):
            sl = pl.ds(pl.multiple_of((cg * GQA_GROUP + u) * tk, tk), tk)
            sls.append(sl)
            s_scr[u] = _nt_dot(q, k_ref[sl, :])
        for u in range(GQA_GROUP):
            a = softmax_chunk(u)
            acc_scr[...] = acc_scr[...] * a + jnp.dot(p_scr[u], v_ref[sls[u], :],
                                                      preferred_element_type=f32)
        return carry

    lax.fori_loop(0, t // (tk * GQA_GROUP), chunk_group, 0)
    o = (acc_scr[...] / l_scr[...]).astype(o_ref.dtype)
    o_ref[...] = jnp.concatenate([o[h * tq:(h + 1) * tq, :] for h in range(grp)], axis=1)


def _gqa(qc, kc, main, bsz, t):
    n = bsz * t
    tq = GQA_TQ
    tk = min(t, GQA_TK)
    grp = GQA_Q_HEADS // GQA_KV_HEADS
    grp_cols = grp * HEAD_DIM
    rows = grp * tq
    nq = t // tq
    return pl.pallas_call(
        functools.partial(_gqa_kernel, t=t, tq=tq, tk=tk),
        grid=(bsz, GQA_KV_HEADS, nq),
        in_specs=[
            pl.BlockSpec((tq, grp_cols), lambda b, j, i: (b * nq + i, j)),
            pl.BlockSpec((t, HEAD_DIM), lambda b, j, i: (b, j)),
            pl.BlockSpec((t, HEAD_DIM), lambda b, j, i: (b, OFF_CV // HEAD_DIM + j)),
        ],
        out_specs=pl.BlockSpec((tq, grp_cols), lambda b, j, i: (b * nq + i, j)),
        out_shape=jax.ShapeDtypeStruct((n, C_Q), bf16),
        scratch_shapes=[
            pltpu.VMEM((GQA_GROUP, rows, tk), f32),
            pltpu.VMEM((GQA_GROUP, rows, tk), bf16),
            pltpu.VMEM((rows, LANES), f32),
            pltpu.VMEM((rows, LANES), f32),
            pltpu.VMEM((rows, HEAD_DIM), f32),
        ],
        compiler_params=_cparams(("parallel", "parallel", "parallel")),
    )(qc, kc, main)


def _merge_kernel(oa_ref, ob0_ref, ob1_ref, ob2_ref, l0_ref, l1_ref, l2_ref, oc_ref,
                  ga_ref, gb_ref, gc_ref, wa_ref, wb_ref, wc_ref, o_ref):
    heads = []
    for h in range(DIL_HEADS):
        l0, l1, l2 = l0_ref[h], l1_ref[h], l2_ref[h]
        m = jnp.maximum(jnp.maximum(l0, l1), l2)
        e0, e1, e2 = jnp.exp(l0 - m), jnp.exp(l1 - m), jnp.exp(l2 - m)
        mix = (e0 * ob0_ref[h].astype(f32) + e1 * ob1_ref[h].astype(f32)
               + e2 * ob2_ref[h].astype(f32)) / (e0 + e1 + e2)
        heads.append(mix.astype(bf16))
    o_b = jnp.concatenate(heads, axis=1)
    up_a = jnp.dot(oa_ref[...], wa_ref[...], preferred_element_type=f32)
    up_b = jnp.dot(o_b, wb_ref[...], preferred_element_type=f32)
    up_c = jnp.dot(oc_ref[...], wc_ref[...], preferred_element_type=f32)
    merged = (jax.nn.sigmoid(ga_ref[...].astype(f32)) * up_a
              + jax.nn.sigmoid(gb_ref[...].astype(f32)) * up_b
              + jax.nn.sigmoid(gc_ref[...].astype(f32)) * up_c)
    o_ref[...] = merged.astype(o_ref.dtype)


def _merge(o_a, o_bs, lses, o_c, main, w_up_a, w_up_b, w_up_c, layer):
    n = o_a.shape[0]
    tm = TM_TOK
    gate_blk = OFF_GATE // D_MODEL
    hspec = pl.BlockSpec((DIL_HEADS, tm, HEAD_DIM), lambda i: (0, i, 0))
    gspec = lambda which: pl.BlockSpec((tm, D_MODEL), lambda i: (i, gate_blk + which))
    wspec = lambda rows: pl.BlockSpec((None, rows, D_MODEL), lambda i: (layer, 0, 0))
    return pl.pallas_call(
        _merge_kernel,
        grid=(n // tm,),
        in_specs=[
            pl.BlockSpec((tm, A_V), lambda i: (i, 0)),
            hspec, hspec, hspec, hspec, hspec, hspec,
            pl.BlockSpec((tm, C_Q), lambda i: (i, 0)),
            gspec(0), gspec(1), gspec(2),
            wspec(A_V), wspec(B_OUT), wspec(C_Q),
        ],
        out_specs=pl.BlockSpec((tm, D_MODEL), lambda i: (i, 0)),
        out_shape=jax.ShapeDtypeStruct((n, D_MODEL), bf16),
        compiler_params=_cparams(("parallel",)),
    )(o_a, o_bs[0], o_bs[1], o_bs[2], lses[0], lses[1], lses[2], o_c,
      main, main, main, w_up_a, w_up_b, w_up_c)


def _layer_norm_rows(y, g, b):
    mu = jnp.mean(y, axis=-1, keepdims=True)
    d = y - mu
    var = jnp.mean(d * d, axis=-1, keepdims=True)
    return d * lax.rsqrt(var + LN_EPS) * g + b


def _route_rows(logits_t, bias_col):
    scores = jax.nn.sigmoid(logits_t)
    sel = scores + bias_col
    rows = [sel[e:e + 1, :] for e in range(N_EXPERTS)]
    srow = [scores[e:e + 1, :] for e in range(N_EXPERTS)]
    best_val = None
    best_grp = None
    for g in range(N_EXPERT_GROUPS):
        mem = rows[g * EXPERTS_PER_GROUP:(g + 1) * EXPERTS_PER_GROUP]
        top2 = None
        for a in range(EXPERTS_PER_GROUP):
            for b in range(a + 1, EXPERTS_PER_GROUP):
                pair = mem[a] + mem[b]
                top2 = pair if top2 is None else jnp.maximum(top2, pair)
        if g == 0:
            best_val, best_grp = top2, jnp.zeros_like(top2, dtype=jnp.int32)
        else:
            upd = top2 > best_val
            best_val = jnp.where(upd, top2, best_val)
            best_grp = jnp.where(upd, g, best_grp)
    neg = jnp.full_like(best_val, -jnp.inf)
    cand = [jnp.where(best_grp == (e // EXPERTS_PER_GROUP), rows[e], neg) for e in range(N_EXPERTS)]

    def arg_top(vals):
        bv, bi = vals[0], jnp.zeros_like(best_grp)
        for e in range(1, N_EXPERTS):
            upd = vals[e] > bv
            bv = jnp.where(upd, vals[e], bv)
            bi = jnp.where(upd, e, bi)
        return bi

    idx1 = arg_top(cand)
    idx2 = arg_top([jnp.where(idx1 == e, neg, cand[e]) for e in range(N_EXPERTS)])
    zero = jnp.zeros_like(best_val)
    s1 = zero
    s2 = zero
    for e in range(N_EXPERTS):
        s1 = s1 + jnp.where(idx1 == e, srow[e], zero)
        s2 = s2 + jnp.where(idx2 == e, srow[e], zero)
    tot = s1 + s2
    return idx1, idx2, s1 / tot, s2 / tot


def _outproj_kernel(m_ref, w_ref, x_ref, g_ref, b_ref, wr_ref, x1_ref, x1b_ref, lg_ref):
    h = jnp.dot(m_ref[...], w_ref[...], preferred_element_type=f32)
    x1 = _layer_norm_rows(DEEPNORM_ALPHA * x_ref[...] + h, g_ref[...], b_ref[...])
    x1_ref[...] = x1
    x_hi = x1.astype(bf16)
    x1b_ref[...] = x_hi
    x_lo = (x1 - x_hi.astype(f32)).astype(bf16)
    wr = wr_ref[...]
    w_hi = wr.astype(bf16)
    w_lo = (wr - w_hi.astype(f32)).astype(bf16)
    lg_ref[...] = (jnp.dot(x_hi, w_hi, preferred_element_type=f32)
                   + jnp.dot(x_hi, w_lo, preferred_element_type=f32)
                   + jnp.dot(x_lo, w_hi, preferred_element_type=f32))


def _outproj_ln(merged, w_out, x, ln_g, ln_b, w_router_pad, layer):
    n = x.shape[0]
    tm = TM_TOK
    vspec = pl.BlockSpec((None, 1, D_MODEL), lambda i: (layer, 0, 0))
    row = pl.BlockSpec((tm, D_MODEL), lambda i: (i, 0))
    return pl.pallas_call(
        _outproj_kernel,
        grid=(n // tm,),
        in_specs=[
            row,
            pl.BlockSpec((None, D_MODEL, D_MODEL), lambda i: (layer, 0, 0)),
            row,
            vspec, vspec,
            pl.BlockSpec((D_MODEL, LANES), lambda i: (0, 0)),
        ],
        out_specs=[row, row, pl.BlockSpec((tm, LANES), lambda i: (i, 0))],
        out_shape=[
            jax.ShapeDtypeStruct((n, D_MODEL), f32),
            jax.ShapeDtypeStruct((n, D_MODEL), bf16),
            jax.ShapeDtypeStruct((n, LANES), f32),
        ],
        compiler_params=_cparams(("parallel",)),
    )(merged, w_out, x, ln_g, ln_b, w_router_pad)


def _route_kernel(lg_ref, rb_ref, idx_ref, w1_ref, w2_ref):
    logits_t = lg_ref[...].T[0:N_EXPERTS, :]
    i1, i2, w1, w2 = _route_rows(logits_t, rb_ref[...])
    idx_ref[0:1, :] = i1
    idx_ref[1:2, :] = i2
    tr = logits_t.shape[1]
    w1_ref[...] = jnp.broadcast_to(w1, (LANES, tr)).T
    w2_ref[...] = jnp.broadcast_to(w2, (LANES, tr)).T


def _route(logits, router_bias_col):
    n = logits.shape[0]
    tr = TM_ROUTE
    return pl.pallas_call(
        _route_kernel,
        grid=(n // tr,),
        in_specs=[
            pl.BlockSpec((tr, LANES), lambda i: (i, 0)),
            pl.BlockSpec((N_EXPERTS, 1), lambda i: (0, 0)),
        ],
        out_specs=[
            pl.BlockSpec((2, tr), lambda i: (0, i)),
            pl.BlockSpec((tr, LANES), lambda i: (i, 0)),
            pl.BlockSpec((tr, LANES), lambda i: (i, 0)),
        ],
        out_shape=[
            jax.ShapeDtypeStruct((2, n), jnp.int32),
            jax.ShapeDtypeStruct((n, LANES), f32),
            jax.ShapeDtypeStruct((n, LANES), f32),
        ],
        compiler_params=_cparams(("parallel",)),
    )(logits, router_bias_col)


def _moe_kernel(te_ref, nt_ref, x_ref, wg_ref, wu_ref, wd_ref, o_ref):
    del te_ref
    tile = pl.program_id(0)

    @pl.when(tile < nt_ref[0])
    def _():
        x = x_ref[...]
        gate = jnp.dot(x, wg_ref[...], preferred_element_type=f32)
        up = jnp.dot(x, wu_ref[...], preferred_element_type=f32)
        h = (gate * jax.nn.sigmoid(gate) * up).astype(bf16)
        o_ref[...] = jnp.dot(h, wd_ref[...], preferred_element_type=f32).astype(o_ref.dtype)

    @pl.when(tile >= nt_ref[0])
    def _():
        o_ref[...] = jnp.zeros_like(o_ref)


def _moe_grouped(x_sorted, tile_expert, n_tiles_used, w_gate, w_up, w_down, layer):
    m_pad = x_sorted.shape[0]
    n_tiles = m_pad // TM_MOE
    wspec_in = pl.BlockSpec((None, None, D_MODEL, D_FF_EXPERT), lambda t, te, nt: (layer, te[t], 0, 0))
    wspec_out = pl.BlockSpec((None, None, D_FF_EXPERT, D_MODEL), lambda t, te, nt: (layer, te[t], 0, 0))
    grid_spec = pltpu.PrefetchScalarGridSpec(
        num_scalar_prefetch=2,
        grid=(n_tiles,),
        in_specs=[
            pl.BlockSpec((TM_MOE, D_MODEL), lambda t, te, nt: (t, 0)),
            wspec_in, wspec_in, wspec_out,
        ],
        out_specs=pl.BlockSpec((TM_MOE, D_MODEL), lambda t, te, nt: (t, 0)),
    )
    return pl.pallas_call(
        _moe_kernel,
        grid_spec=grid_spec,
        out_shape=jax.ShapeDtypeStruct((m_pad, D_MODEL), bf16),
        compiler_params=_cparams(("arbitrary",)),
    )(tile_expert, n_tiles_used, x_sorted, w_gate, w_up, w_down)


def _combine_kernel(x_ref, y1_ref, y2_ref, w1_ref, w2_ref, g_ref, b_ref, o_ref, ob_ref):
    w1 = w1_ref[...]
    w2 = w2_ref[...]
    for c in range(D_MODEL // LANES):
        ls = slice(c * LANES, (c + 1) * LANES)
        o_ref[:, ls] = (DEEPNORM_ALPHA * x_ref[:, ls] + w1 * y1_ref[:, ls].astype(f32)
                        + w2 * y2_ref[:, ls].astype(f32))
    x2 = _layer_norm_rows(o_ref[...], g_ref[...], b_ref[...])
    o_ref[...] = x2
    ob_ref[...] = x2.astype(bf16)


def _combine_ln(x1, y1, y2, w1b, w2b, ln_g, ln_b, layer):
    n = x1.shape[0]
    tm = TM_TOK
    row = pl.BlockSpec((tm, D_MODEL), lambda i: (i, 0))
    wsp = pl.BlockSpec((tm, LANES), lambda i: (i, 0))
    vspec = pl.BlockSpec((None, 1, D_MODEL), lambda i: (layer, 0, 0))
    return pl.pallas_call(
        _combine_kernel,
        grid=(n // tm,),
        in_specs=[row, row, row, wsp, wsp, vspec, vspec],
        out_specs=[row, row],
        out_shape=[jax.ShapeDtypeStruct((n, D_MODEL), f32), jax.ShapeDtypeStruct((n, D_MODEL), bf16)],
        compiler_params=_cparams(("parallel",)),
    )(x1, y1, y2, w1b, w2b, ln_g, ln_b)


def _dispatch_plan(idx):
    n = idx.shape[1]
    m_pad = 2 * n + N_EXPERTS * TM_MOE
    n_tiles = m_pad // TM_MOE
    e_flat = idx.reshape(-1)
    onehot = (e_flat[:, None] == jnp.arange(N_EXPERTS, dtype=jnp.int32)[None, :]).astype(jnp.int32)
    rank = jnp.sum((jnp.cumsum(onehot, axis=0) - onehot) * onehot, axis=1)
    counts = jnp.sum(onehot, axis=0)
    tiles_per = (counts + TM_MOE - 1) // TM_MOE
    tile_end = jnp.cumsum(tiles_per)
    start_row = (tile_end - tiles_per) * TM_MOE
    dest = start_row.at[e_flat].get(mode="promise_in_bounds") + rank
    tok = jnp.tile(jnp.arange(n, dtype=jnp.int32), 2)
    src = jnp.zeros((m_pad,), jnp.int32).at[dest].set(tok, mode="promise_in_bounds", unique_indices=True)
    n_used = tile_end[-1]
    tile_ids = jnp.minimum(jnp.arange(n_tiles, dtype=jnp.int32), n_used - 1)
    tile_expert = jnp.sum((tile_ids[:, None] >= tile_end[None, :]).astype(jnp.int32), axis=1)
    tile_expert = jnp.minimum(tile_expert, N_EXPERTS - 1)
    return src, dest, tile_expert, n_used.reshape(1).astype(jnp.int32)


def _rope_tables(pos, dim):
    inv_freq = ROPE_THETA ** (-jnp.arange(0, dim, 2, dtype=f32) / dim)
    ang = pos.astype(f32)[:, None] * inv_freq[None, :]
    return jnp.cos(ang), jnp.sin(ang)


def _tables(t):
    cos, sin = _rope_tables(jnp.arange(t), HEAD_DIM)
    cos_b = jnp.concatenate([cos, cos], axis=-1)
    sin_b = jnp.concatenate([-sin, sin], axis=-1)
    tok = jnp.arange(t)
    cr, sr = _rope_tables(tok // GRID_W, HEAD_DIM // 2)
    cc, sc = _rope_tables(tok % GRID_W, HEAD_DIM // 2)
    cos_c = jnp.concatenate([cr, cr, cc, cc], axis=-1)
    sin_c = jnp.concatenate([-sr, sr, -sc, sc], axis=-1)
    return cos_b, sin_b, cos_c, sin_c


def _repack_w_in(w_in):
    offs = [0]
    for s in IN_SPLITS:
        offs.append(offs[-1] + s)
    seg = [w_in[:, :, offs[i]:offs[i + 1]] for i in range(len(IN_SPLITS))]
    a_q, a_k, a_v, a_r, a_lr, b_qkv, c_q, c_k, c_v, gate = seg
    lr_pad = jnp.pad(a_lr, ((0, 0), (0, 0), (0, LR_PAD - A_LR)))
    w_main = jnp.concatenate([gate, a_q, a_k, a_v, a_r, c_q, c_k, c_v, lr_pad], axis=-1)
    w_main = jnp.pad(w_main, ((0, 0), (0, 0), (0, MAIN_COLS - MAIN_USED))).astype(bf16)
    return w_main, b_qkv.astype(bf16)


def _mixer(xb, wts, layer, bsz, t, tables):
    cos_b, sin_b, cos_c, sin_c = tables
    main = _proj_main(xb, wts["w_main"], layer)
    qkv_b = _proj_b(xb, wts["w_b"], cos_b, sin_b, layer, t)
    o_a = _gla(main, wts["gla_w2_f"], wts["gla_b_f"], wts["gla_w2_b"], wts["gla_b_b"],
               wts["gla_norm_g"], layer, bsz, t)
    o_bs, lses = [], []
    for group in range(len(DIL_CONFIGS)):
        o_g, lse_g = _dilated(qkv_b, group, bsz, t)
        o_bs.append(o_g)
        lses.append(lse_g)
    qc, kc = _c_prep(main, cos_c, sin_c, wts["q_norm_g"], wts["k_norm_g"], layer, t)
    o_c = _gqa(qc, kc, main, bsz, t)
    return _merge(o_a, o_bs, lses, o_c, main, wts["w_up_a"], wts["w_up_b"], wts["w_up_c"], layer)


def _gather_rows(a, rows):
    return a.at[rows].get(mode="promise_in_bounds")


def _moe(x1, x1b, idx, w1b, w2b, wts, layer):
    n = x1.shape[0]
    src, dest, tile_expert, n_used = _dispatch_plan(idx)
    x_sorted = _gather_rows(x1b, src)
    y_sorted = _moe_grouped(x_sorted, tile_expert, n_used, wts["moe_w_gate"], wts["moe_w_up"],
                            wts["moe_w_down"], layer)
    y1 = _gather_rows(y_sorted, dest[:n])
    y2 = _gather_rows(y_sorted, dest[n:])
    return _combine_ln(x1, y1, y2, w1b, w2b, wts["ln2_g"], wts["ln2_b"], layer)


def _trunk(x3, wts, depth=DEPTH):
    bsz, t, _ = x3.shape
    n = bsz * t
    x = x3.reshape(n, D_MODEL)
    xb = x.astype(bf16)
    tables = _tables(t)
    for layer in range(depth):
        merged = _mixer(xb, wts, layer, bsz, t, tables)
        x1, x1b, logits = _outproj_ln(merged, wts["w_out"], x, wts["ln1_g"], wts["ln1_b"],
                                      wts["w_router_pad"], layer)
        idx, w1b, w2b = _route(logits, wts["router_bias_col"])
        x, xb = _moe(x1, x1b, idx, w1b, w2b, wts, layer)
    return x.reshape(bsz, t, D_MODEL)


def _prepare_weights(w_in, gla_w2_f, gla_b_f, gla_w2_b, gla_b_b, gla_norm_g, q_norm_g, k_norm_g,
                     w_up_a, w_up_b, w_up_c, w_out, ln1_g, ln1_b, w_router, router_bias,
                     moe_w_gate, moe_w_up, moe_w_down, ln2_g, ln2_b):
    w_main, w_b = _repack_w_in(w_in)
    row = lambda a: a.reshape(DEPTH, 1, a.shape[-1]).astype(f32)
    return {
        "w_main": w_main,
        "w_b": w_b,
        "gla_w2_f": gla_w2_f.astype(f32),
        "gla_b_f": row(gla_b_f),
        "gla_w2_b": gla_w2_b.astype(f32),
        "gla_b_b": row(gla_b_b),
        "gla_norm_g": row(gla_norm_g),
        "q_norm_g": row(q_norm_g),
        "k_norm_g": row(k_norm_g),
        "w_up_a": w_up_a.astype(bf16),
        "w_up_b": w_up_b.astype(bf16),
        "w_up_c": w_up_c.astype(bf16),
        "w_out": w_out.astype(bf16),
        "ln1_g": row(ln1_g),
        "ln1_b": row(ln1_b),
        "w_router_pad": jnp.pad(w_router.astype(f32), ((0, 0), (0, LANES - N_EXPERTS))),
        "router_bias_col": router_bias.astype(f32).reshape(N_EXPERTS, 1),
        "moe_w_gate": moe_w_gate.astype(bf16),
        "moe_w_up": moe_w_up.astype(bf16),
        "moe_w_down": moe_w_down.astype(bf16),
        "ln2_g": row(ln2_g),
        "ln2_b": row(ln2_b),
    }


def kernel(x_prompt, x_sample, w_in, gla_w2_f, gla_b_f, gla_w2_b, gla_b_b, gla_norm_g, q_norm_g, k_norm_g, w_up_a, w_up_b, w_up_c, w_out, ln1_g, ln1_b, w_router, router_bias, moe_w_gate, moe_w_up, moe_w_down, ln2_g, ln2_b):
    wts = _prepare_weights(w_in, gla_w2_f, gla_b_f, gla_w2_b, gla_b_b, gla_norm_g, q_norm_g, k_norm_g,
                           w_up_a, w_up_b, w_up_c, w_out, ln1_g, ln1_b, w_router, router_bias,
                           moe_w_gate, moe_w_up, moe_w_down, ln2_g, ln2_b)
    return _trunk(x_prompt, wts), _trunk(x_sample, wts)
```

```python
import functools

import jax
import jax.numpy as jnp
from jax import lax
from jax.experimental import pallas as pl
from jax.experimental.pallas import tpu as pltpu

D_MODEL = 2048
DEPTH = 4
HEAD_DIM = 128
GRID_W = 64
ROPE_THETA = 10000.0
LN_EPS = 1e-5
RMS_EPS = 1e-6
GLA_HEADS = 4
GLA_DK = 128
GLA_DV = 256
GLA_RANK = 16
GLA_TAU = 16.0
GLA_CHUNK = 64
DIL_CONFIGS = ((128, 1), (512, 4), (2048, 16))
DIL_HEADS = 4
GQA_Q_HEADS = 8
GQA_KV_HEADS = 2
N_EXPERTS = 16
N_EXPERT_GROUPS = 4
EXPERTS_PER_GROUP = N_EXPERTS // N_EXPERT_GROUPS
D_FF_EXPERT = 1024
DEEPNORM_ALPHA = (2.0 * DEPTH) ** 0.25

A_QK = GLA_HEADS * GLA_DK
A_V = GLA_HEADS * GLA_DV
A_LR = 2 * GLA_RANK
B_HEADS = len(DIL_CONFIGS) * DIL_HEADS
B_QKV = 3 * B_HEADS * HEAD_DIM
B_OUT = DIL_HEADS * HEAD_DIM
C_Q = GQA_Q_HEADS * HEAD_DIM
C_KV = GQA_KV_HEADS * HEAD_DIM
GATE_COLS = 3 * D_MODEL
IN_SPLITS = (A_QK, A_QK, A_V, A_V, A_LR, B_QKV, C_Q, C_KV, C_KV, GATE_COLS)

LANES = 128
VMEM_LIMIT = 56 * 1024 * 1024
LR_PAD = LANES

OFF_GATE = 0
OFF_AQ = OFF_GATE + GATE_COLS
OFF_AK = OFF_AQ + A_QK
OFF_AV = OFF_AK + A_QK
OFF_AR = OFF_AV + A_V
OFF_CQ = OFF_AR + A_V
OFF_CK = OFF_CQ + C_Q
OFF_CV = OFF_CK + C_KV
OFF_LR = OFF_CV + C_KV
MAIN_USED = OFF_LR + LR_PAD

TM_PROJ = 1024
TN_MAIN = 1024
TM_ROUTE = 2048
MAIN_COLS = -(-MAIN_USED // TN_MAIN) * TN_MAIN
TN_B = B_HEADS * HEAD_DIM
TM_TOK = 256
TM_MOE = 512
MOE_GATHER_ROWS = 24576
TM_MAIN = 2048
GQA_TQ = 512
GQA_TK = 1024
GQA_GROUP = 2
GQA_STRIP = 32
LOG2_E = 1.4426950408889634
GLA_SUPER = 8
DIL_UNROLL = 8
DIL_HALF = 64
NEG_BIG = -1e30

f32 = jnp.float32
bf16 = jnp.bfloat16


def _cparams(sem):
    return pltpu.CompilerParams(dimension_semantics=sem, vmem_limit_bytes=VMEM_LIMIT)


def _nt_dot(a, b):
    return lax.dot_general(a, b, (((1,), (1,)), ((), ())), preferred_element_type=f32)


def _tn_dot(a, b):
    return lax.dot_general(a, b, (((0,), (0,)), ((), ())), preferred_element_type=f32)


def _mm_kernel(x_ref, w_ref, o_ref):
    o_ref[...] = jnp.dot(x_ref[...], w_ref[...], preferred_element_type=f32).astype(o_ref.dtype)


def _proj_main(xb, w_main, layer):
    n, k = xb.shape
    grid = (MAIN_COLS // TN_MAIN, n // TM_MAIN)
    return pl.pallas_call(
        _mm_kernel,
        grid=grid,
        in_specs=[
            pl.BlockSpec((TM_MAIN, k), lambda j, i: (i, 0)),
            pl.BlockSpec((None, k, TN_MAIN), lambda j, i: (layer, 0, j)),
        ],
        out_specs=pl.BlockSpec((TM_MAIN, TN_MAIN), lambda j, i: (i, j)),
        out_shape=jax.ShapeDtypeStruct((n, MAIN_COLS), bf16),
        compiler_params=_cparams(("parallel", "parallel")),
    )(xb, w_main)


def _proj_b_kernel(x_ref, w_ref, cos_ref, sin_ref, o_ref):
    j = pl.program_id(0)
    pair = 2 * HEAD_DIM

    def emit(rotary, scale):
        x = x_ref[...]
        if rotary:
            cos = cos_ref[...] * scale
            sin = sin_ref[...] * scale
        for c in range(TN_B // pair):
            acc = jnp.dot(x, w_ref[:, c * pair:(c + 1) * pair], preferred_element_type=f32)
            for h in range(2):
                a = acc[:, h * HEAD_DIM:(h + 1) * HEAD_DIM]
                if rotary:
                    a = a * cos + pltpu.roll(a, HEAD_DIM // 2, 1) * sin
                o_ref[2 * c + h] = a.astype(o_ref.dtype)

    @pl.when(j == 0)
    def _():
        emit(True, HEAD_DIM ** -0.5)

    @pl.when(j == 1)
    def _():
        emit(True, 1.0)

    @pl.when(j == 2)
    def _():
        emit(False, 1.0)


def _proj_b(xb, w_b, cos_b, sin_b, layer, t):
    n, k = xb.shape
    tpb = t // TM_PROJ
    grid = (B_QKV // TN_B, n // TM_PROJ)
    hpt = TN_B // HEAD_DIM
    return pl.pallas_call(
        _proj_b_kernel,
        grid=grid,
        in_specs=[
            pl.BlockSpec((TM_PROJ, k), lambda j, i: (i, 0)),
            pl.BlockSpec((None, k, TN_B), lambda j, i: (layer, 0, j)),
            pl.BlockSpec((TM_PROJ, HEAD_DIM), lambda j, i: (i % tpb, 0)),
            pl.BlockSpec((TM_PROJ, HEAD_DIM), lambda j, i: (i % tpb, 0)),
        ],
        out_specs=pl.BlockSpec((hpt, TM_PROJ, HEAD_DIM), lambda j, i: (j, i, 0)),
        out_shape=jax.ShapeDtypeStruct((3 * B_HEADS, n, HEAD_DIM), bf16),
        compiler_params=_cparams(("parallel", "parallel")),
    )(xb, w_b, cos_b, sin_b)


def _gla_kernel(q_ref, k_ref, v_ref, r_ref, lr_ref, w2f_ref, bf_ref, w2b_ref, bb_ref, g_ref,
                o_ref, accf_ref, accb_ref, *, t):
    c_len = GLA_CHUNK
    sup = GLA_SUPER * c_len
    n_sup = t // sup
    row = lax.broadcasted_iota(jnp.int32, (sup, sup), 0)
    col = lax.broadcasted_iota(jnp.int32, (sup, sup), 1)
    same = (row // c_len) == (col // c_len)

    def half(base, fwd, s_t):
        tri = jnp.where(same & ((row >= col) if fwd else (row <= col)), 1.0, 0.0).astype(bf16)
        smask = same & ((col <= row) if fwd else (col > row))
        w2 = (w2f_ref if fwd else w2b_ref)[...].astype(bf16)
        bias = (bf_ref if fwd else bb_ref)[...]
        lo = 0 if fwd else GLA_RANK
        acc_ref = accf_ref if fwd else accb_ref
        sl = pl.ds(base, sup)
        q = q_ref[sl, :].astype(f32) * (GLA_DK ** -0.5)
        k = k_ref[sl, :].astype(f32)
        v = v_ref[sl, :]
        lr = lr_ref[sl, :][:, lo:lo + GLA_RANK]
        z = jnp.dot(lr, w2, preferred_element_type=f32) + bias
        g = (jnp.minimum(z, 0.0) - jnp.log1p(jnp.exp(-jnp.abs(z)))) * (1.0 / GLA_TAU)
        g_hi = g.astype(bf16)
        g_lo = (g - g_hi.astype(f32)).astype(bf16)
        b = (jnp.dot(tri, g_hi, preferred_element_type=f32)
             + jnp.dot(tri, g_lo, preferred_element_type=f32))
        edge = c_len - 1 if fwd else 0
        tot = jnp.concatenate(
            [jnp.broadcast_to(b[c * c_len + edge:c * c_len + edge + 1, :], (c_len, GLA_DK))
             for c in range(GLA_SUPER)], axis=0)
        q_t = (q * jnp.exp(b)).astype(bf16)
        k_t = (k * jnp.exp(-b)).astype(bf16)
        k_s = (k * jnp.exp(tot - b)).astype(bf16)
        dec = jnp.exp(tot)
        scores = jnp.where(smask, _nt_dot(q_t, k_t), 0.0)
        o_intra = jnp.dot(scores.astype(bf16), v, preferred_element_type=f32)
        order = range(GLA_SUPER) if fwd else range(GLA_SUPER - 1, -1, -1)
        for c in order:
            rs = slice(c * c_len, (c + 1) * c_len)
            acc_ref[pl.ds(base + c * c_len, c_len), :] = (
                o_intra[rs, :] + _nt_dot(q_t[rs, :], s_t.astype(bf16)))
            s_t = s_t * dec[c * c_len:c * c_len + 1, :] + _tn_dot(v[rs, :], k_s[rs, :])
        return s_t

    def body(i, carry):
        s_f, s_b = carry
        s_f = half(pl.multiple_of(i * sup, sup), True, s_f)
        s_b = half(pl.multiple_of((n_sup - 1 - i) * sup, sup), False, s_b)
        return s_f, s_b

    zero = jnp.zeros((GLA_DV, GLA_DK), f32)
    lax.fori_loop(0, n_sup, body, (zero, zero))

    rows = 256
    gain = g_ref[...]

    def finish(i, carry):
        sl = pl.ds(pl.multiple_of(i * rows, rows), rows)
        x = accf_ref[sl, :] + accb_ref[sl, :]
        ms = jnp.mean(x * x, axis=-1, keepdims=True)
        y = x * lax.rsqrt(ms + RMS_EPS) * gain
        r = r_ref[sl, :].astype(f32)
        o_ref[sl, :] = (y * (r * jax.nn.sigmoid(r))).astype(o_ref.dtype)
        return carry

    lax.fori_loop(0, t // rows, finish, 0)


def _gla(main, w2f, b_f, w2b, b_b, gain, layer, bsz, t):
    n = bsz * t
    qk_blk = lambda off: (lambda b, h: (b, off // GLA_DK + h))
    v_blk = lambda off: (lambda b, h: (b, off // GLA_DV + h))
    wspec = pl.BlockSpec((None, GLA_RANK, GLA_DK), lambda b, h: (layer, 0, h))
    bspec = pl.BlockSpec((None, 1, GLA_DK), lambda b, h: (layer, 0, h))
    return pl.pallas_call(
        functools.partial(_gla_kernel, t=t),
        grid=(bsz, GLA_HEADS),
        in_specs=[
            pl.BlockSpec((t, GLA_DK), qk_blk(OFF_AQ)),
            pl.BlockSpec((t, GLA_DK), qk_blk(OFF_AK)),
            pl.BlockSpec((t, GLA_DV), v_blk(OFF_AV)),
            pl.BlockSpec((t, GLA_DV), v_blk(OFF_AR)),
            pl.BlockSpec((t, LR_PAD), lambda b, h: (b, OFF_LR // LR_PAD)),
            wspec, bspec, wspec, bspec,
            pl.BlockSpec((None, 1, GLA_DV), lambda b, h: (layer, 0, h)),
        ],
        out_specs=pl.BlockSpec((t, GLA_DV), lambda b, h: (b, h)),
        out_shape=jax.ShapeDtypeStruct((n, A_V), bf16),
        scratch_shapes=[pltpu.VMEM((t, GLA_DV), f32), pltpu.VMEM((t, GLA_DV), f32)],
        compiler_params=_cparams(("parallel", "parallel")),
    )(main, main, main, main, main, w2f, b_f, w2b, b_b, gain)


def _dil_kernel(q_ref, k_ref, v_ref, o_ref, lse_ref, *scratch, t, dil):
    n_sub = t // dil
    bq = min(128, n_sub)
    win = min(bq + 2 * DIL_HALF, n_sub)
    n_blk = n_sub // bq
    qi = lax.broadcasted_iota(jnp.int32, (bq, win), 0)
    ki = lax.broadcasted_iota(jnp.int32, (bq, win), 1)
    cvt = 512

    if dil > 1:
        q32, k32, v32, o32 = scratch

        def widen(i, carry):
            sl = pl.ds(pl.multiple_of(i * cvt, cvt), cvt)
            q32[sl, :] = q_ref[0, sl, :].astype(f32)
            k32[sl, :] = k_ref[0, sl, :].astype(f32)
            v32[sl, :] = v_ref[0, sl, :].astype(f32)
            return carry

        lax.fori_loop(0, t // cvt, widen, 0)

    def body(idx, carry):
        r = idx // n_blk
        i = idx % n_blk
        q0 = i * bq
        k0 = jnp.clip(q0 - DIL_HALF, 0, n_sub - win)
        if dil == 1:
            qs = pl.ds(pl.multiple_of(q0, bq), bq)
            ks = pl.ds(pl.multiple_of(k0, DIL_HALF), win)
            q = q_ref[0, qs, :]
            k = k_ref[0, ks, :]
            v = v_ref[0, ks, :]
        else:
            qs = pl.ds(r + q0 * dil, bq, stride=dil)
            ks = pl.ds(r + k0 * dil, win, stride=dil)
            q = q32[qs, :].astype(bf16)
            k = k32[ks, :].astype(bf16)
            v = v32[ks, :].astype(bf16)
        s = _nt_dot(q, k)
        valid = jnp.abs((q0 + qi) - (k0 + ki)) <= DIL_HALF
        s = jnp.where(valid, s, NEG_BIG)
        m = jnp.max(s, axis=-1, keepdims=True)
        p = jnp.exp(s - m)
        l = jnp.sum(p, axis=-1, keepdims=True)
        o = jnp.dot(p.astype(bf16), v, preferred_element_type=f32) / l
        lse = jnp.broadcast_to(m + jnp.log(l), (bq, HEAD_DIM))
        if dil == 1:
            o_ref[0, qs, :] = o.astype(o_ref.dtype)
            lse_ref[0, qs, :] = lse
        else:
            o32[qs, :] = o
            lse_ref[0, qs, :] = lse
        return carry

    def body_group(j, carry):
        for u in range(DIL_UNROLL):
            body(j * DIL_UNROLL + u, carry)
        return carry

    lax.fori_loop(0, dil * n_blk // DIL_UNROLL, body_group, 0)

    if dil > 1:
        def narrow(i, carry):
            sl = pl.ds(pl.multiple_of(i * cvt, cvt), cvt)
            o_ref[0, sl, :] = o32[sl, :].astype(o_ref.dtype)
            return carry

        lax.fori_loop(0, t // cvt, narrow, 0)


def _dilated(qkv_b, group, bsz, t):
    _, dil = DIL_CONFIGS[group]
    n = bsz * t
    blk = (1, t, HEAD_DIM)
    head0 = group * DIL_HEADS
    scratch = [pltpu.VMEM((t, HEAD_DIM), f32)] * 4 if dil > 1 else []
    return pl.pallas_call(
        functools.partial(_dil_kernel, t=t, dil=dil),
        grid=(bsz, DIL_HEADS),
        in_specs=[
            pl.BlockSpec(blk, lambda b, h: (head0 + h, b, 0)),
            pl.BlockSpec(blk, lambda b, h: (B_HEADS + head0 + h, b, 0)),
            pl.BlockSpec(blk, lambda b, h: (2 * B_HEADS + head0 + h, b, 0)),
        ],
        out_specs=[
            pl.BlockSpec(blk, lambda b, h: (h, b, 0)),
            pl.BlockSpec(blk, lambda b, h: (h, b, 0)),
        ],
        out_shape=[
            jax.ShapeDtypeStruct((DIL_HEADS, n, HEAD_DIM), bf16),
            jax.ShapeDtypeStruct((DIL_HEADS, n, HEAD_DIM), f32),
        ],
        scratch_shapes=scratch,
        compiler_params=_cparams(("parallel", "parallel")),
    )(qkv_b, qkv_b, qkv_b)


def _axial_partner(x):
    lane = lax.broadcasted_iota(jnp.int32, x.shape, 1)
    quarter = HEAD_DIM // 4
    first = (lane % (2 * quarter)) < quarter
    return jnp.where(first, pltpu.roll(x, HEAD_DIM - quarter, 1), pltpu.roll(x, quarter, 1))


def _c_prep_kernel(q_ref, k_ref, cos_ref, sin_ref, qg_ref, kg_ref, qo_ref, ko_ref):
    cos = cos_ref[...]
    sin = sin_ref[...]

    def prep(x, gain, scale):
        x = x.astype(f32)
        ms = jnp.mean(x * x, axis=-1, keepdims=True)
        y = x * lax.rsqrt(ms + RMS_EPS) * gain
        return (y * cos + _axial_partner(y) * sin) * scale

    qg = qg_ref[...]
    kg = kg_ref[...]
    for h in range(GQA_Q_HEADS):
        ls = slice(h * HEAD_DIM, (h + 1) * HEAD_DIM)
        qo_ref[:, ls] = prep(q_ref[:, ls], qg, HEAD_DIM ** -0.5 * LOG2_E).astype(qo_ref.dtype)
    for h in range(GQA_KV_HEADS):
        ls = slice(h * HEAD_DIM, (h + 1) * HEAD_DIM)
        ko_ref[:, ls] = prep(k_ref[:, ls], kg, 1.0).astype(ko_ref.dtype)


def _c_prep(main, cos_c, sin_c, q_gain, k_gain, layer, t):
    n = main.shape[0]
    tm = TM_PROJ
    tpb = t // tm
    gspec = pl.BlockSpec((None, 1, HEAD_DIM), lambda i: (layer, 0, 0))
    return pl.pallas_call(
        _c_prep_kernel,
        grid=(n // tm,),
        in_specs=[
            pl.BlockSpec((tm, C_Q), lambda i: (i, OFF_CQ // C_Q)),
            pl.BlockSpec((tm, C_KV), lambda i: (i, OFF_CK // C_KV)),
            pl.BlockSpec((tm, HEAD_DIM), lambda i: (i % tpb, 0)),
            pl.BlockSpec((tm, HEAD_DIM), lambda i: (i % tpb, 0)),
            gspec, gspec,
        ],
        out_specs=[
            pl.BlockSpec((tm, C_Q), lambda i: (i, 0)),
            pl.BlockSpec((tm, C_KV), lambda i: (i, 0)),
        ],
        out_shape=[
            jax.ShapeDtypeStruct((n, C_Q), bf16),
            jax.ShapeDtypeStruct((n, C_KV), bf16),
        ],
        compiler_params=_cparams(("parallel",)),
    )(main, main, cos_c, sin_c, q_gain, k_gain)


def _gqa_kernel(q_ref, k_ref, v_ref, o_ref, s_scr, p_scr, m_scr, l_scr, acc_scr, *, t, tq, tk):
    grp = GQA_Q_HEADS // GQA_KV_HEADS
    q4 = q_ref[...]
    q = jnp.concatenate([q4[:, h * HEAD_DIM:(h + 1) * HEAD_DIM] for h in range(grp)], axis=0)
    rows = grp * tq
    n_strips = rows // GQA_STRIP
    m_scr[...] = jnp.full((rows, LANES), NEG_BIG, f32)
    l_scr[...] = jnp.zeros((rows, LANES), f32)
    acc_scr[...] = jnp.zeros((rows, HEAD_DIM), f32)

    def widen(col):
        return jnp.broadcast_to(col, (col.shape[0], LANES))

    def softmax_chunk(buf):
        mx = []
        for i in range(n_strips):
            rs = slice(i * GQA_STRIP, (i + 1) * GQA_STRIP)
            mx.append(widen(jnp.max(s_scr[buf, rs, :], axis=-1, keepdims=True)))
        m_old = m_scr[...]
        m_new = jnp.maximum(m_old, jnp.concatenate(mx, axis=0))
        a = jnp.exp2(m_old - m_new)
        m_scr[...] = m_new
        sums = []
        for i in range(n_strips):
            rs = slice(i * GQA_STRIP, (i + 1) * GQA_STRIP)
            m_wide = jnp.concatenate([m_new[rs, :]] * (tk // LANES), axis=1)
            p = jnp.exp2(s_scr[buf, rs, :] - m_wide)
            sums.append(widen(jnp.sum(p, axis=-1, keepdims=True)))
            p_scr[buf, rs, :] = p.astype(bf16)
        l_scr[...] = a * l_scr[...] + jnp.concatenate(sums, axis=0)
        return a

    def chunk_group(cg, carry):
        sls = []
        for u in range(GQA_GROUP):
            sl = pl.ds(pl.multiple_of((cg * GQA_GROUP + u) * tk, tk), tk)
            sls.append(sl)
            s_scr[u] = _nt_dot(q, k_ref[sl, :])
        for u in range(GQA_GROUP):
            a = softmax_chunk(u)
            acc_scr[...] = acc_scr[...] * a + jnp.dot(p_scr[u], v_ref[sls[u], :],
                                                      preferred_element_type=f32)
        return carry

    lax.fori_loop(0, t // (tk * GQA_GROUP), chunk_group, 0)
    o = (acc_scr[...] / l_scr[...]).astype(o_ref.dtype)
    o_ref[...] = jnp.concatenate([o[h * tq:(h + 1) * tq, :] for h in range(grp)], axis=1)


def _gqa(qc, kc, main, bsz, t):
    n = bsz * t
    tq = GQA_TQ
    tk = min(t, GQA_TK)
    grp = GQA_Q_HEADS // GQA_KV_HEADS
    grp_cols = grp * HEAD_DIM
    rows = grp * tq
    nq = t // tq
    return pl.pallas_call(
        functools.partial(_gqa_kernel, t=t, tq=tq, tk=tk),
        grid=(bsz, GQA_KV_HEADS, nq),
        in_specs=[
            pl.BlockSpec((tq, grp_cols), lambda b, j, i: (b * nq + i, j)),
            pl.BlockSpec((t, HEAD_DIM), lambda b, j, i: (b, j)),
            pl.BlockSpec((t, HEAD_DIM), lambda b, j, i: (b, OFF_CV // HEAD_DIM + j)),
        ],
        out_specs=pl.BlockSpec((tq, grp_cols), lambda b, j, i: (b * nq + i, j)),
        out_shape=jax.ShapeDtypeStruct((n, C_Q), bf16),
        scratch_shapes=[
            pltpu.VMEM((GQA_GROUP, rows, tk), f32),
            pltpu.VMEM((GQA_GROUP, rows, tk), bf16),
            pltpu.VMEM((rows, LANES), f32),
            pltpu.VMEM((rows, LANES), f32),
            pltpu.VMEM((rows, HEAD_DIM), f32),
        ],
        compiler_params=_cparams(("parallel", "parallel", "parallel")),
    )(qc, kc, main)


def _merge_kernel(oa_ref, ob0_ref, ob1_ref, ob2_ref, l0_ref, l1_ref, l2_ref, oc_ref,
                  ga_ref, gb_ref, gc_ref, wa_ref, wb_ref, wc_ref, o_ref):
    heads = []
    for h in range(DIL_HEADS):
        l0, l1, l2 = l0_ref[h], l1_ref[h], l2_ref[h]
        m = jnp.maximum(jnp.maximum(l0, l1), l2)
        e0, e1, e2 = jnp.exp(l0 - m), jnp.exp(l1 - m), jnp.exp(l2 - m)
        mix = (e0 * ob0_ref[h].astype(f32) + e1 * ob1_ref[h].astype(f32)
               + e2 * ob2_ref[h].astype(f32)) / (e0 + e1 + e2)
        heads.append(mix.astype(bf16))
    o_b = jnp.concatenate(heads, axis=1)
    up_a = jnp.dot(oa_ref[...], wa_ref[...], preferred_element_type=f32)
    up_b = jnp.dot(o_b, wb_ref[...], preferred_element_type=f32)
    up_c = jnp.dot(oc_ref[...], wc_ref[...], preferred_element_type=f32)
    merged = (jax.nn.sigmoid(ga_ref[...].astype(f32)) * up_a
              + jax.nn.sigmoid(gb_ref[...].astype(f32)) * up_b
              + jax.nn.sigmoid(gc_ref[...].astype(f32)) * up_c)
    o_ref[...] = merged.astype(o_ref.dtype)


def _merge(o_a, o_bs, lses, o_c, main, w_up_a, w_up_b, w_up_c, layer):
    n = o_a.shape[0]
    tm = TM_TOK
    gate_blk = OFF_GATE // D_MODEL
    hspec = pl.BlockSpec((DIL_HEADS, tm, HEAD_DIM), lambda i: (0, i, 0))
    gspec = lambda which: pl.BlockSpec((tm, D_MODEL), lambda i: (i, gate_blk + which))
    wspec = lambda rows: pl.BlockSpec((None, rows, D_MODEL), lambda i: (layer, 0, 0))
    return pl.pallas_call(
        _merge_kernel,
        grid=(n // tm,),
        in_specs=[
            pl.BlockSpec((tm, A_V), lambda i: (i, 0)),
            hspec, hspec, hspec, hspec, hspec, hspec,
            pl.BlockSpec((tm, C_Q), lambda i: (i, 0)),
            gspec(0), gspec(1), gspec(2),
            wspec(A_V), wspec(B_OUT), wspec(C_Q),
        ],
        out_specs=pl.BlockSpec((tm, D_MODEL), lambda i: (i, 0)),
        out_shape=jax.ShapeDtypeStruct((n, D_MODEL), bf16),
        compiler_params=_cparams(("parallel",)),
    )(o_a, o_bs[0], o_bs[1], o_bs[2], lses[0], lses[1], lses[2], o_c,
      main, main, main, w_up_a, w_up_b, w_up_c)


def _layer_norm_rows(y, g, b):
    mu = jnp.mean(y, axis=-1, keepdims=True)
    d = y - mu
    var = jnp.mean(d * d, axis=-1, keepdims=True)
    return d * lax.rsqrt(var + LN_EPS) * g + b


def _route_rows(logits_t, bias_col):
    scores = jax.nn.sigmoid(logits_t)
    sel = scores + bias_col
    rows = [sel[e:e + 1, :] for e in range(N_EXPERTS)]
    srow = [scores[e:e + 1, :] for e in range(N_EXPERTS)]
    best_val = None
    best_grp = None
    for g in range(N_EXPERT_GROUPS):
        mem = rows[g * EXPERTS_PER_GROUP:(g + 1) * EXPERTS_PER_GROUP]
        top2 = None
        for a in range(EXPERTS_PER_GROUP):
            for b in range(a + 1, EXPERTS_PER_GROUP):
                pair = mem[a] + mem[b]
                top2 = pair if top2 is None else jnp.maximum(top2, pair)
        if g == 0:
            best_val, best_grp = top2, jnp.zeros_like(top2, dtype=jnp.int32)
        else:
            upd = top2 > best_val
            best_val = jnp.where(upd, top2, best_val)
            best_grp = jnp.where(upd, g, best_grp)
    neg = jnp.full_like(best_val, -jnp.inf)
    cand = [jnp.where(best_grp == (e // EXPERTS_PER_GROUP), rows[e], neg) for e in range(N_EXPERTS)]

    def arg_top(vals):
        bv, bi = vals[0], jnp.zeros_like(best_grp)
        for e in range(1, N_EXPERTS):
            upd = vals[e] > bv
            bv = jnp.where(upd, vals[e], bv)
            bi = jnp.where(upd, e, bi)
        return bi

    idx1 = arg_top(cand)
    idx2 = arg_top([jnp.where(idx1 == e, neg, cand[e]) for e in range(N_EXPERTS)])
    zero = jnp.zeros_like(best_val)
    s1 = zero
    s2 = zero
    for e in range(N_EXPERTS):
        s1 = s1 + jnp.where(idx1 == e, srow[e], zero)
        s2 = s2 + jnp.where(idx2 == e, srow[e], zero)
    tot = s1 + s2
    return idx1, idx2, s1 / tot, s2 / tot


def _outproj_kernel(m_ref, w_ref, x_ref, g_ref, b_ref, wr_ref, x1_ref, x1b_ref, lg_ref):
    h = jnp.dot(m_ref[...], w_ref[...], preferred_element_type=f32)
    x1 = _layer_norm_rows(DEEPNORM_ALPHA * x_ref[...] + h, g_ref[...], b_ref[...])
    x1_ref[...] = x1
    x_hi = x1.astype(bf16)
    x1b_ref[...] = x_hi
    x_lo = (x1 - x_hi.astype(f32)).astype(bf16)
    wr = wr_ref[...]
    w_hi = wr.astype(bf16)
    w_lo = (wr - w_hi.astype(f32)).astype(bf16)
    lg_ref[...] = (jnp.dot(x_hi, w_hi, preferred_element_type=f32)
                   + jnp.dot(x_hi, w_lo, preferred_element_type=f32)
                   + jnp.dot(x_lo, w_hi, preferred_element_type=f32))


def _outproj_ln(merged, w_out, x, ln_g, ln_b, w_router_pad, layer):
    n = x.shape[0]
    tm = TM_TOK
    vspec = pl.BlockSpec((None, 1, D_MODEL), lambda i: (layer, 0, 0))
    row = pl.BlockSpec((tm, D_MODEL), lambda i: (i, 0))
    return pl.pallas_call(
        _outproj_kernel,
        grid=(n // tm,),
        in_specs=[
            row,
            pl.BlockSpec((None, D_MODEL, D_MODEL), lambda i: (layer, 0, 0)),
            row,
            vspec, vspec,
            pl.BlockSpec((D_MODEL, LANES), lambda i: (0, 0)),
        ],
        out_specs=[row, row, pl.BlockSpec((tm, LANES), lambda i: (i, 0))],
        out_shape=[
            jax.ShapeDtypeStruct((n, D_MODEL), f32),
            jax.ShapeDtypeStruct((n, D_MODEL), bf16),
            jax.ShapeDtypeStruct((n, LANES), f32),
        ],
        compiler_params=_cparams(("parallel",)),
    )(merged, w_out, x, ln_g, ln_b, w_router_pad)


def _route_kernel(lg_ref, rb_ref, idx_ref, w1_ref, w2_ref):
    logits_t = lg_ref[...].T[0:N_EXPERTS, :]
    i1, i2, w1, w2 = _route_rows(logits_t, rb_ref[...])
    idx_ref[0:1, :] = i1
    idx_ref[1:2, :] = i2
    tr = logits_t.shape[1]
    w1_ref[...] = jnp.broadcast_to(w1, (LANES, tr)).T
    w2_ref[...] = jnp.broadcast_to(w2, (LANES, tr)).T


def _route(logits, router_bias_col):
    n = logits.shape[0]
    tr = TM_ROUTE
    return pl.pallas_call(
        _route_kernel,
        grid=(n // tr,),
        in_specs=[
            pl.BlockSpec((tr, LANES), lambda i: (i, 0)),
            pl.BlockSpec((N_EXPERTS, 1), lambda i: (0, 0)),
        ],
        out_specs=[
            pl.BlockSpec((2, tr), lambda i: (0, i)),
            pl.BlockSpec((tr, LANES), lambda i: (i, 0)),
            pl.BlockSpec((tr, LANES), lambda i: (i, 0)),
        ],
        out_shape=[
            jax.ShapeDtypeStruct((2, n), jnp.int32),
            jax.ShapeDtypeStruct((n, LANES), f32),
            jax.ShapeDtypeStruct((n, LANES), f32),
        ],
        compiler_params=_cparams(("parallel",)),
    )(logits, router_bias_col)


def _moe_kernel(te_ref, nt_ref, *refs, tiles_per_piece):
    del te_ref
    x_refs, (wg_ref, wu_ref, wd_ref, o_ref) = refs[:-4], refs[-4:]
    tile = pl.program_id(0)

    @pl.when(tile < nt_ref[0])
    def _():
        x = x_refs[0][...]
        for p in range(1, len(x_refs)):
            x = jnp.where(tile >= p * tiles_per_piece, x_refs[p][...], x)
        gate = jnp.dot(x, wg_ref[...], preferred_element_type=f32)
        up = jnp.dot(x, wu_ref[...], preferred_element_type=f32)
        h = (gate * jax.nn.sigmoid(gate) * up).astype(bf16)
        o_ref[...] = jnp.dot(h, wd_ref[...], preferred_element_type=f32).astype(o_ref.dtype)

    @pl.when(tile >= nt_ref[0])
    def _():
        o_ref[...] = jnp.zeros_like(o_ref)


def _moe_grouped(x_pieces, tile_expert, n_tiles_used, w_gate, w_up, w_down, layer):
    tpp = x_pieces[0].shape[0] // TM_MOE
    n_tiles = tpp * len(x_pieces)
    wspec_in = pl.BlockSpec((None, None, D_MODEL, D_FF_EXPERT), lambda t, te, nt: (layer, te[t], 0, 0))
    wspec_out = pl.BlockSpec((None, None, D_FF_EXPERT, D_MODEL), lambda t, te, nt: (layer, te[t], 0, 0))

    def piece_spec(p):
        return pl.BlockSpec((TM_MOE, D_MODEL),
                            lambda t, te, nt: (jnp.clip(t - p * tpp, 0, tpp - 1), 0))

    grid_spec = pltpu.PrefetchScalarGridSpec(
        num_scalar_prefetch=2,
        grid=(n_tiles,),
        in_specs=[piece_spec(p) for p in range(len(x_pieces))] + [wspec_in, wspec_in, wspec_out],
        out_specs=pl.BlockSpec((TM_MOE, D_MODEL), lambda t, te, nt: (t, 0)),
    )
    return pl.pallas_call(
        functools.partial(_moe_kernel, tiles_per_piece=tpp),
        grid_spec=grid_spec,
        out_shape=jax.ShapeDtypeStruct((n_tiles * TM_MOE, D_MODEL), bf16),
        compiler_params=_cparams(("arbitrary",)),
    )(tile_expert, n_tiles_used, *x_pieces, w_gate, w_up, w_down)


def _combine_kernel(x_ref, y1_ref, y2_ref, w1_ref, w2_ref, g_ref, b_ref, o_ref, ob_ref):
    w1 = w1_ref[...]
    w2 = w2_ref[...]
    for c in range(D_MODEL // LANES):
        ls = slice(c * LANES, (c + 1) * LANES)
        o_ref[:, ls] = (DEEPNORM_ALPHA * x_ref[:, ls] + w1 * y1_ref[:, ls].astype(f32)
                        + w2 * y2_ref[:, ls].astype(f32))
    x2 = _layer_norm_rows(o_ref[...], g_ref[...], b_ref[...])
    o_ref[...] = x2
    ob_ref[...] = x2.astype(bf16)


def _combine_ln(x1, y1, y2, w1b, w2b, ln_g, ln_b, layer):
    n = x1.shape[0]
    tm = TM_TOK
    row = pl.BlockSpec((tm, D_MODEL), lambda i: (i, 0))
    wsp = pl.BlockSpec((tm, LANES), lambda i: (i, 0))
    vspec = pl.BlockSpec((None, 1, D_MODEL), lambda i: (layer, 0, 0))
    return pl.pallas_call(
        _combine_kernel,
        grid=(n // tm,),
        in_specs=[row, row, row, wsp, wsp, vspec, vspec],
        out_specs=[row, row],
        out_shape=[jax.ShapeDtypeStruct((n, D_MODEL), f32), jax.ShapeDtypeStruct((n, D_MODEL), bf16)],
        compiler_params=_cparams(("parallel",)),
    )(x1, y1, y2, w1b, w2b, ln_g, ln_b)


def _dispatch_plan(idx):
    n = idx.shape[1]
    m_pad = 2 * n + N_EXPERTS * TM_MOE
    n_tiles = m_pad // TM_MOE
    e_flat = idx.reshape(-1)
    onehot = (e_flat[:, None] == jnp.arange(N_EXPERTS, dtype=jnp.int32)[None, :]).astype(jnp.int32)
    rank = jnp.sum((jnp.cumsum(onehot, axis=0) - onehot) * onehot, axis=1)
    counts = jnp.sum(onehot, axis=0)
    tiles_per = (counts + TM_MOE - 1) // TM_MOE
    tile_end = jnp.cumsum(tiles_per)
    start_row = (tile_end - tiles_per) * TM_MOE
    dest = start_row.at[e_flat].get(mode="promise_in_bounds") + rank
    tok = jnp.tile(jnp.arange(n, dtype=jnp.int32), 2)
    src = jnp.zeros((m_pad,), jnp.int32).at[dest].set(tok, mode="promise_in_bounds", unique_indices=True)
    n_used = tile_end[-1]
    tile_ids = jnp.minimum(jnp.arange(n_tiles, dtype=jnp.int32), n_used - 1)
    tile_expert = jnp.sum((tile_ids[:, None] >= tile_end[None, :]).astype(jnp.int32), axis=1)
    tile_expert = jnp.minimum(tile_expert, N_EXPERTS - 1)
    return src, dest, tile_expert, n_used.reshape(1).astype(jnp.int32)


def _rope_tables(pos, dim):
    inv_freq = ROPE_THETA ** (-jnp.arange(0, dim, 2, dtype=f32) / dim)
    ang = pos.astype(f32)[:, None] * inv_freq[None, :]
    return jnp.cos(ang), jnp.sin(ang)


def _tables(t):
    cos, sin = _rope_tables(jnp.arange(t), HEAD_DIM)
    cos_b = jnp.concatenate([cos, cos], axis=-1)
    sin_b = jnp.concatenate([-sin, sin], axis=-1)
    tok = jnp.arange(t)
    cr, sr = _rope_tables(tok // GRID_W, HEAD_DIM // 2)
    cc, sc = _rope_tables(tok % GRID_W, HEAD_DIM // 2)
    cos_c = jnp.concatenate([cr, cr, cc, cc], axis=-1)
    sin_c = jnp.concatenate([-sr, sr, -sc, sc], axis=-1)
    return cos_b, sin_b, cos_c, sin_c


def _repack_w_in(w_in):
    offs = [0]
    for s in IN_SPLITS:
        offs.append(offs[-1] + s)
    seg = [w_in[:, :, offs[i]:offs[i + 1]] for i in range(len(IN_SPLITS))]
    a_q, a_k, a_v, a_r, a_lr, b_qkv, c_q, c_k, c_v, gate = seg
    lr_pad = jnp.pad(a_lr, ((0, 0), (0, 0), (0, LR_PAD - A_LR)))
    w_main = jnp.concatenate([gate, a_q, a_k, a_v, a_r, c_q, c_k, c_v, lr_pad], axis=-1)
    w_main = jnp.pad(w_main, ((0, 0), (0, 0), (0, MAIN_COLS - MAIN_USED))).astype(bf16)
    return w_main, b_qkv.astype(bf16)


def _mixer(xb, wts, layer, bsz, t, tables):
    cos_b, sin_b, cos_c, sin_c = tables
    main = _proj_main(xb, wts["w_main"], layer)
    qkv_b = _proj_b(xb, wts["w_b"], cos_b, sin_b, layer, t)
    o_a = _gla(main, wts["gla_w2_f"], wts["gla_b_f"], wts["gla_w2_b"], wts["gla_b_b"],
               wts["gla_norm_g"], layer, bsz, t)
    o_bs, lses = [], []
    for group in range(len(DIL_CONFIGS)):
        o_g, lse_g = _dilated(qkv_b, group, bsz, t)
        o_bs.append(o_g)
        lses.append(lse_g)
    qc, kc = _c_prep(main, cos_c, sin_c, wts["q_norm_g"], wts["k_norm_g"], layer, t)
    o_c = _gqa(qc, kc, main, bsz, t)
    return _merge(o_a, o_bs, lses, o_c, main, wts["w_up_a"], wts["w_up_b"], wts["w_up_c"], layer)


def _gather_rows(a, rows):
    return a.at[rows].get(mode="promise_in_bounds")


def _moe(x1, x1b, idx, w1b, w2b, wts, layer):
    n = x1.shape[0]
    src, dest, tile_expert, n_used = _dispatch_plan(idx)
    n_pieces = -(-src.shape[0] // MOE_GATHER_ROWS)
    x_pieces = [_gather_rows(x1b, piece) for piece in jnp.split(src, n_pieces)]
    y_sorted = _moe_grouped(x_pieces, tile_expert, n_used, wts["moe_w_gate"], wts["moe_w_up"],
                            wts["moe_w_down"], layer)
    y1 = _gather_rows(y_sorted, dest[:n])
    y2 = _gather_rows(y_sorted, dest[n:])
    return _combine_ln(x1, y1, y2, w1b, w2b, wts["ln2_g"], wts["ln2_b"], layer)


def _trunk(x3, wts, depth=DEPTH):
    bsz, t, _ = x3.shape
    n = bsz * t
    x = x3.reshape(n, D_MODEL)
    xb = x.astype(bf16)
    tables = _tables(t)
    for layer in range(depth):
        merged = _mixer(xb, wts, layer, bsz, t, tables)
        x1, x1b, logits = _outproj_ln(merged, wts["w_out"], x, wts["ln1_g"], wts["ln1_b"],
                                      wts["w_router_pad"], layer)
        idx, w1b, w2b = _route(logits, wts["router_bias_col"])
        x, xb = _moe(x1, x1b, idx, w1b, w2b, wts, layer)
    return x.reshape(bsz, t, D_MODEL)


def _prepare_weights(w_in, gla_w2_f, gla_b_f, gla_w2_b, gla_b_b, gla_norm_g, q_norm_g, k_norm_g,
                     w_up_a, w_up_b, w_up_c, w_out, ln1_g, ln1_b, w_router, router_bias,
                     moe_w_gate, moe_w_up, moe_w_down, ln2_g, ln2_b):
    w_main, w_b = _repack_w_in(w_in)
    row = lambda a: a.reshape(DEPTH, 1, a.shape[-1]).astype(f32)
    return {
        "w_main": w_main,
        "w_b": w_b,
        "gla_w2_f": gla_w2_f.astype(f32),
        "gla_b_f": row(gla_b_f),
        "gla_w2_b": gla_w2_b.astype(f32),
        "gla_b_b": row(gla_b_b),
        "gla_norm_g": row(gla_norm_g),
        "q_norm_g": row(q_norm_g),
        "k_norm_g": row(k_norm_g),
        "w_up_a": w_up_a.astype(bf16),
        "w_up_b": w_up_b.astype(bf16),
        "w_up_c": w_up_c.astype(bf16),
        "w_out": w_out.astype(bf16),
        "ln1_g": row(ln1_g),
        "ln1_b": row(ln1_b),
        "w_router_pad": jnp.pad(w_router.astype(f32), ((0, 0), (0, LANES - N_EXPERTS))),
        "router_bias_col": router_bias.astype(f32).reshape(N_EXPERTS, 1),
        "moe_w_gate": moe_w_gate.astype(bf16),
        "moe_w_up": moe_w_up.astype(bf16),
        "moe_w_down": moe_w_down.astype(bf16),
        "ln2_g": row(ln2_g),
        "ln2_b": row(ln2_b),
    }


def kernel(x_prompt, x_sample, w_in, gla_w2_f, gla_b_f, gla_w2_b, gla_b_b, gla_norm_g, q_norm_g, k_norm_g, w_up_a, w_up_b, w_up_c, w_out, ln1_g, ln1_b, w_router, router_bias, moe_w_gate, moe_w_up, moe_w_down, ln2_g, ln2_b):
    wts = _prepare_weights(w_in, gla_w2_f, gla_b_f, gla_w2_b, gla_b_b, gla_norm_g, q_norm_g, k_norm_g,
                           w_up_a, w_up_b, w_up_c, w_out, ln1_g, ln1_b, w_router, router_bias,
                           moe_w_gate, moe_w_up, moe_w_down, ln2_g, ln2_b)
    return _trunk(x_prompt, wts), _trunk(x_sample, wts)
```

```python
import functools

import jax
import jax.numpy as jnp
from jax import lax
from jax.experimental import pallas as pl
from jax.experimental.pallas import tpu as pltpu

D_MODEL = 2048
DEPTH = 4
HEAD_DIM = 128
GRID_W = 64
ROPE_THETA = 10000.0
LN_EPS = 1e-5
RMS_EPS = 1e-6
GLA_HEADS = 4
GLA_DK = 128
GLA_DV = 256
GLA_RANK = 16
GLA_TAU = 16.0
GLA_CHUNK = 64
DIL_CONFIGS = ((128, 1), (512, 4), (2048, 16))
DIL_HEADS = 4
GQA_Q_HEADS = 8
GQA_KV_HEADS = 2
N_EXPERTS = 16
N_EXPERT_GROUPS = 4
EXPERTS_PER_GROUP = N_EXPERTS // N_EXPERT_GROUPS
D_FF_EXPERT = 1024
DEEPNORM_ALPHA = (2.0 * DEPTH) ** 0.25

A_QK = GLA_HEADS * GLA_DK
A_V = GLA_HEADS * GLA_DV
A_LR = 2 * GLA_RANK
B_HEADS = len(DIL_CONFIGS) * DIL_HEADS
B_QKV = 3 * B_HEADS * HEAD_DIM
B_OUT = DIL_HEADS * HEAD_DIM
C_Q = GQA_Q_HEADS * HEAD_DIM
C_KV = GQA_KV_HEADS * HEAD_DIM
GATE_COLS = 3 * D_MODEL
IN_SPLITS = (A_QK, A_QK, A_V, A_V, A_LR, B_QKV, C_Q, C_KV, C_KV, GATE_COLS)

LANES = 128
VMEM_LIMIT = 56 * 1024 * 1024
LR_PAD = LANES

OFF_GATE = 0
OFF_AQ = OFF_GATE + GATE_COLS
OFF_AK = OFF_AQ + A_QK
OFF_AV = OFF_AK + A_QK
OFF_AR = OFF_AV + A_V
OFF_CQ = OFF_AR + A_V
OFF_CK = OFF_CQ + C_Q
OFF_CV = OFF_CK + C_KV
OFF_LR = OFF_CV + C_KV
MAIN_USED = OFF_LR + LR_PAD

TM_PROJ = 1024
TN_MAIN = 1024
TM_ROUTE = 2048
MAIN_COLS = -(-MAIN_USED // TN_MAIN) * TN_MAIN
TN_B = B_HEADS * HEAD_DIM
TM_TOK = 256
TM_MOE = 512
MOE_GATHER_ROWS = 24576
COMBINE_GATHER_ROWS = 16384
TM_MAIN = 2048
GQA_TQ = 512
GQA_TK = 1024
GQA_GROUP = 2
GQA_STRIP = 32
LOG2_E = 1.4426950408889634
GLA_SUPER = 8
DIL_UNROLL = 8
DIL_HALF = 64
NEG_BIG = -1e30

f32 = jnp.float32
bf16 = jnp.bfloat16


def _cparams(sem):
    return pltpu.CompilerParams(dimension_semantics=sem, vmem_limit_bytes=VMEM_LIMIT)


def _nt_dot(a, b):
    return lax.dot_general(a, b, (((1,), (1,)), ((), ())), preferred_element_type=f32)


def _tn_dot(a, b):
    return lax.dot_general(a, b, (((0,), (0,)), ((), ())), preferred_element_type=f32)


def _mm_kernel(x_ref, w_ref, o_ref):
    o_ref[...] = jnp.dot(x_ref[...], w_ref[...], preferred_element_type=f32).astype(o_ref.dtype)


def _proj_main(xb, w_main, layer):
    n, k = xb.shape
    grid = (MAIN_COLS // TN_MAIN, n // TM_MAIN)
    return pl.pallas_call(
        _mm_kernel,
        grid=grid,
        in_specs=[
            pl.BlockSpec((TM_MAIN, k), lambda j, i: (i, 0)),
            pl.BlockSpec((None, k, TN_MAIN), lambda j, i: (layer, 0, j)),
        ],
        out_specs=pl.BlockSpec((TM_MAIN, TN_MAIN), lambda j, i: (i, j)),
        out_shape=jax.ShapeDtypeStruct((n, MAIN_COLS), bf16),
        compiler_params=_cparams(("parallel", "parallel")),
    )(xb, w_main)


def _proj_b_kernel(x_ref, w_ref, cos_ref, sin_ref, o_ref):
    j = pl.program_id(0)
    pair = 2 * HEAD_DIM

    def emit(rotary, scale):
        x = x_ref[...]
        if rotary:
            cos = cos_ref[...] * scale
            sin = sin_ref[...] * scale
        for c in range(TN_B // pair):
            acc = jnp.dot(x, w_ref[:, c * pair:(c + 1) * pair], preferred_element_type=f32)
            for h in range(2):
                a = acc[:, h * HEAD_DIM:(h + 1) * HEAD_DIM]
                if rotary:
                    a = a * cos + pltpu.roll(a, HEAD_DIM // 2, 1) * sin
                o_ref[2 * c + h] = a.astype(o_ref.dtype)

    @pl.when(j == 0)
    def _():
        emit(True, HEAD_DIM ** -0.5)

    @pl.when(j == 1)
    def _():
        emit(True, 1.0)

    @pl.when(j == 2)
    def _():
        emit(False, 1.0)


def _proj_b(xb, w_b, cos_b, sin_b, layer, t):
    n, k = xb.shape
    tpb = t // TM_PROJ
    grid = (B_QKV // TN_B, n // TM_PROJ)
    hpt = TN_B // HEAD_DIM
    return pl.pallas_call(
        _proj_b_kernel,
        grid=grid,
        in_specs=[
            pl.BlockSpec((TM_PROJ, k), lambda j, i: (i, 0)),
            pl.BlockSpec((None, k, TN_B), lambda j, i: (layer, 0, j)),
            pl.BlockSpec((TM_PROJ, HEAD_DIM), lambda j, i: (i % tpb, 0)),
            pl.BlockSpec((TM_PROJ, HEAD_DIM), lambda j, i: (i % tpb, 0)),
        ],
        out_specs=pl.BlockSpec((hpt, TM_PROJ, HEAD_DIM), lambda j, i: (j, i, 0)),
        out_shape=jax.ShapeDtypeStruct((3 * B_HEADS, n, HEAD_DIM), bf16),
        compiler_params=_cparams(("parallel", "parallel")),
    )(xb, w_b, cos_b, sin_b)


def _gla_kernel(q_ref, k_ref, v_ref, r_ref, lr_ref, w2f_ref, bf_ref, w2b_ref, bb_ref, g_ref,
                o_ref, accf_ref, accb_ref, *, t):
    c_len = GLA_CHUNK
    sup = GLA_SUPER * c_len
    n_sup = t // sup
    row = lax.broadcasted_iota(jnp.int32, (sup, sup), 0)
    col = lax.broadcasted_iota(jnp.int32, (sup, sup), 1)
    same = (row // c_len) == (col // c_len)

    def stage_decay(base, fwd):
        w2 = (w2f_ref if fwd else w2b_ref)[...].astype(bf16)
        bias = (bf_ref if fwd else bb_ref)[...]
        lo = 0 if fwd else GLA_RANK
        lr = lr_ref[pl.ds(base, sup), :][:, lo:lo + GLA_RANK]
        z = jnp.dot(lr, w2, preferred_element_type=f32) + bias
        g = (jnp.minimum(z, 0.0) - jnp.log(1.0 + jnp.exp(-jnp.abs(z)))) * (1.0 / GLA_TAU)
        g_hi = g.astype(bf16)
        g_lo = (g - g_hi.astype(f32)).astype(bf16)
        return g_hi, g_lo

    def stage_cumsum(parts, fwd):
        g_hi, g_lo = parts
        tri = jnp.where(same & ((row >= col) if fwd else (row <= col)), 1.0, 0.0).astype(bf16)
        b = (jnp.dot(tri, g_hi, preferred_element_type=f32)
             + jnp.dot(tri, g_lo, preferred_element_type=f32))
        edge = c_len - 1 if fwd else 0
        tot = jnp.concatenate(
            [jnp.broadcast_to(b[c * c_len + edge:c * c_len + edge + 1, :], (c_len, GLA_DK))
             for c in range(GLA_SUPER)], axis=0)
        return b, tot

    def stage_scale(base, b, tot):
        sl = pl.ds(base, sup)
        q = q_ref[sl, :].astype(f32) * (GLA_DK ** -0.5)
        k = k_ref[sl, :].astype(f32)
        q_t = (q * jnp.exp(b)).astype(bf16)
        k_t = (k * jnp.exp(-b)).astype(bf16)
        k_s = (k * jnp.exp(tot - b)).astype(bf16)
        return q_t, k_t, k_s, jnp.exp(tot)

    def stage_intra(base, q_t, k_t, fwd):
        smask = same & ((col <= row) if fwd else (col > row))
        scores = jnp.where(smask, _nt_dot(q_t, k_t), 0.0)
        return jnp.dot(scores.astype(bf16), v_ref[pl.ds(base, sup), :], preferred_element_type=f32)

    def chunk_step(base, c, fwd, q_t, k_s, dec, o_intra, s_t):
        acc_ref = accf_ref if fwd else accb_ref
        rs = slice(c * c_len, (c + 1) * c_len)
        rows_c = pl.ds(base + c * c_len, c_len)
        acc_ref[rows_c, :] = o_intra[rs, :] + _nt_dot(q_t[rs, :], s_t.astype(bf16))
        return s_t * dec[c * c_len:c * c_len + 1, :] + _tn_dot(v_ref[rows_c, :], k_s[rs, :])

    def body(i, carry):
        s_f, s_b = carry
        base_f = pl.multiple_of(i * sup, sup)
        base_b = pl.multiple_of((n_sup - 1 - i) * sup, sup)
        parts_f = stage_decay(base_f, True)
        parts_b = stage_decay(base_b, False)
        b_f, tot_f = stage_cumsum(parts_f, True)
        b_b, tot_b = stage_cumsum(parts_b, False)
        qt_f, kt_f, ks_f, dec_f = stage_scale(base_f, b_f, tot_f)
        qt_b, kt_b, ks_b, dec_b = stage_scale(base_b, b_b, tot_b)
        oi_f = stage_intra(base_f, qt_f, kt_f, True)
        oi_b = stage_intra(base_b, qt_b, kt_b, False)
        for c in range(GLA_SUPER):
            s_f = chunk_step(base_f, c, True, qt_f, ks_f, dec_f, oi_f, s_f)
            s_b = chunk_step(base_b, GLA_SUPER - 1 - c, False, qt_b, ks_b, dec_b, oi_b, s_b)
        return s_f, s_b

    zero = jnp.zeros((GLA_DV, GLA_DK), f32)
    lax.fori_loop(0, n_sup, body, (zero, zero))

    rows = 256
    gain = g_ref[...]

    def finish(i, carry):
        sl = pl.ds(pl.multiple_of(i * rows, rows), rows)
        x = accf_ref[sl, :] + accb_ref[sl, :]
        ms = jnp.mean(x * x, axis=-1, keepdims=True)
        y = x * lax.rsqrt(ms + RMS_EPS) * gain
        r = r_ref[sl, :].astype(f32)
        o_ref[sl, :] = (y * (r * jax.nn.sigmoid(r))).astype(o_ref.dtype)
        return carry

    lax.fori_loop(0, t // rows, finish, 0)


def _gla(main, w2f, b_f, w2b, b_b, gain, layer, bsz, t):
    n = bsz * t
    qk_blk = lambda off: (lambda b, h: (b, off // GLA_DK + h))
    v_blk = lambda off: (lambda b, h: (b, off // GLA_DV + h))
    wspec = pl.BlockSpec((None, GLA_RANK, GLA_DK), lambda b, h: (layer, 0, h))
    bspec = pl.BlockSpec((None, 1, GLA_DK), lambda b, h: (layer, 0, h))
    return pl.pallas_call(
        functools.partial(_gla_kernel, t=t),
        grid=(bsz, GLA_HEADS),
        in_specs=[
            pl.BlockSpec((t, GLA_DK), qk_blk(OFF_AQ)),
            pl.BlockSpec((t, GLA_DK), qk_blk(OFF_AK)),
            pl.BlockSpec((t, GLA_DV), v_blk(OFF_AV)),
            pl.BlockSpec((t, GLA_DV), v_blk(OFF_AR)),
            pl.BlockSpec((t, LR_PAD), lambda b, h: (b, OFF_LR // LR_PAD)),
            wspec, bspec, wspec, bspec,
            pl.BlockSpec((None, 1, GLA_DV), lambda b, h: (layer, 0, h)),
        ],
        out_specs=pl.BlockSpec((t, GLA_DV), lambda b, h: (b, h)),
        out_shape=jax.ShapeDtypeStruct((n, A_V), bf16),
        scratch_shapes=[pltpu.VMEM((t, GLA_DV), f32), pltpu.VMEM((t, GLA_DV), f32)],
        compiler_params=_cparams(("parallel", "parallel")),
    )(main, main, main, main, main, w2f, b_f, w2b, b_b, gain)


def _dil_kernel(q_ref, k_ref, v_ref, o_ref, lse_ref, *scratch, t, dil):
    n_sub = t // dil
    bq = min(128, n_sub)
    win = min(bq + 2 * DIL_HALF, n_sub)
    n_blk = n_sub // bq
    qi = lax.broadcasted_iota(jnp.int32, (bq, win), 0)
    ki = lax.broadcasted_iota(jnp.int32, (bq, win), 1)
    cvt = 512

    if dil > 1:
        q32, k32, v32, o32 = scratch

        def widen(i, carry):
            sl = pl.ds(pl.multiple_of(i * cvt, cvt), cvt)
            q32[sl, :] = q_ref[0, sl, :].astype(f32)
            k32[sl, :] = k_ref[0, sl, :].astype(f32)
            v32[sl, :] = v_ref[0, sl, :].astype(f32)
            return carry

        lax.fori_loop(0, t // cvt, widen, 0)

    def body(idx, carry):
        r = idx // n_blk
        i = idx % n_blk
        q0 = i * bq
        k0 = jnp.clip(q0 - DIL_HALF, 0, n_sub - win)
        if dil == 1:
            qs = pl.ds(pl.multiple_of(q0, bq), bq)
            ks = pl.ds(pl.multiple_of(k0, DIL_HALF), win)
            q = q_ref[0, qs, :]
            k = k_ref[0, ks, :]
            v = v_ref[0, ks, :]
        else:
            qs = pl.ds(r + q0 * dil, bq, stride=dil)
            ks = pl.ds(r + k0 * dil, win, stride=dil)
            q = q32[qs, :].astype(bf16)
            k = k32[ks, :].astype(bf16)
            v = v32[ks, :].astype(bf16)
        s = _nt_dot(q, k)
        valid = jnp.abs((q0 + qi) - (k0 + ki)) <= DIL_HALF
        s = jnp.where(valid, s, NEG_BIG)
        m = jnp.max(s, axis=-1, keepdims=True)
        p = jnp.exp(s - m)
        l = jnp.sum(p, axis=-1, keepdims=True)
        o = jnp.dot(p.astype(bf16), v, preferred_element_type=f32) / l
        lse = jnp.broadcast_to(m + jnp.log(l), (bq, HEAD_DIM))
        if dil == 1:
            o_ref[0, qs, :] = o.astype(o_ref.dtype)
            lse_ref[0, qs, :] = lse
        else:
            o32[qs, :] = o
            lse_ref[0, qs, :] = lse
        return carry

    def body_group(j, carry):
        for u in range(DIL_UNROLL):
            body(j * DIL_UNROLL + u, carry)
        return carry

    lax.fori_loop(0, dil * n_blk // DIL_UNROLL, body_group, 0)

    if dil > 1:
        def narrow(i, carry):
            sl = pl.ds(pl.multiple_of(i * cvt, cvt), cvt)
            o_ref[0, sl, :] = o32[sl, :].astype(o_ref.dtype)
            return carry

        lax.fori_loop(0, t // cvt, narrow, 0)


def _dilated(qkv_b, group, bsz, t):
    _, dil = DIL_CONFIGS[group]
    n = bsz * t
    blk = (1, t, HEAD_DIM)
    head0 = group * DIL_HEADS
    scratch = [pltpu.VMEM((t, HEAD_DIM), f32)] * 4 if dil > 1 else []
    return pl.pallas_call(
        functools.partial(_dil_kernel, t=t, dil=dil),
        grid=(bsz, DIL_HEADS),
        in_specs=[
            pl.BlockSpec(blk, lambda b, h: (head0 + h, b, 0)),
            pl.BlockSpec(blk, lambda b, h: (B_HEADS + head0 + h, b, 0)),
            pl.BlockSpec(blk, lambda b, h: (2 * B_HEADS + head0 + h, b, 0)),
        ],
        out_specs=[
            pl.BlockSpec(blk, lambda b, h: (h, b, 0)),
            pl.BlockSpec(blk, lambda b, h: (h, b, 0)),
        ],
        out_shape=[
            jax.ShapeDtypeStruct((DIL_HEADS, n, HEAD_DIM), bf16),
            jax.ShapeDtypeStruct((DIL_HEADS, n, HEAD_DIM), f32),
        ],
        scratch_shapes=scratch,
        compiler_params=_cparams(("parallel", "parallel")),
    )(qkv_b, qkv_b, qkv_b)


def _axial_partner(x):
    lane = lax.broadcasted_iota(jnp.int32, x.shape, 1)
    quarter = HEAD_DIM // 4
    first = (lane % (2 * quarter)) < quarter
    return jnp.where(first, pltpu.roll(x, HEAD_DIM - quarter, 1), pltpu.roll(x, quarter, 1))


def _c_prep_kernel(q_ref, k_ref, cos_ref, sin_ref, qg_ref, kg_ref, qo_ref, ko_ref):
    cos = cos_ref[...]
    sin = sin_ref[...]

    def prep(x, gain, scale):
        x = x.astype(f32)
        ms = jnp.mean(x * x, axis=-1, keepdims=True)
        y = x * lax.rsqrt(ms + RMS_EPS) * gain
        return (y * cos + _axial_partner(y) * sin) * scale

    qg = qg_ref[...]
    kg = kg_ref[...]
    for h in range(GQA_Q_HEADS):
        ls = slice(h * HEAD_DIM, (h + 1) * HEAD_DIM)
        qo_ref[:, ls] = prep(q_ref[:, ls], qg, HEAD_DIM ** -0.5 * LOG2_E).astype(qo_ref.dtype)
    for h in range(GQA_KV_HEADS):
        ls = slice(h * HEAD_DIM, (h + 1) * HEAD_DIM)
        ko_ref[:, ls] = prep(k_ref[:, ls], kg, 1.0).astype(ko_ref.dtype)


def _c_prep(main, cos_c, sin_c, q_gain, k_gain, layer, t):
    n = main.shape[0]
    tm = TM_PROJ
    tpb = t // tm
    gspec = pl.BlockSpec((None, 1, HEAD_DIM), lambda i: (layer, 0, 0))
    return pl.pallas_call(
        _c_prep_kernel,
        grid=(n // tm,),
        in_specs=[
            pl.BlockSpec((tm, C_Q), lambda i: (i, OFF_CQ // C_Q)),
            pl.BlockSpec((tm, C_KV), lambda i: (i, OFF_CK // C_KV)),
            pl.BlockSpec((tm, HEAD_DIM), lambda i: (i % tpb, 0)),
            pl.BlockSpec((tm, HEAD_DIM), lambda i: (i % tpb, 0)),
            gspec, gspec,
        ],
        out_specs=[
            pl.BlockSpec((tm, C_Q), lambda i: (i, 0)),
            pl.BlockSpec((tm, C_KV), lambda i: (i, 0)),
        ],
        out_shape=[
            jax.ShapeDtypeStruct((n, C_Q), bf16),
            jax.ShapeDtypeStruct((n, C_KV), bf16),
        ],
        compiler_params=_cparams(("parallel",)),
    )(main, main, cos_c, sin_c, q_gain, k_gain)


def _gqa_kernel(q_ref, k_ref, v_ref, o_ref, s_scr, p_scr, m_scr, l_scr, acc_scr, *, t, tq, tk):
    grp = GQA_Q_HEADS // GQA_KV_HEADS
    q4 = q_ref[...]
    q = jnp.concatenate([q4[:, h * HEAD_DIM:(h + 1) * HEAD_DIM] for h in range(grp)], axis=0)
    rows = grp * tq
    n_strips = rows // GQA_STRIP
    m_scr[...] = jnp.full((rows, LANES), NEG_BIG, f32)
    l_scr[...] = jnp.zeros((rows, LANES), f32)
    acc_scr[...] = jnp.zeros((rows, HEAD_DIM), f32)

    def widen(col):
        return jnp.broadcast_to(col, (col.shape[0], LANES))

    def softmax_chunk(buf):
        mx = []
        for i in range(n_strips):
            rs = slice(i * GQA_STRIP, (i + 1) * GQA_STRIP)
            mx.append(widen(jnp.max(s_scr[buf, rs, :], axis=-1, keepdims=True)))
        m_old = m_scr[...]
        m_new = jnp.maximum(m_old, jnp.concatenate(mx, axis=0))
        a = jnp.exp2(m_old - m_new)
        m_scr[...] = m_new
        sums = []
        for i in range(n_strips):
            rs = slice(i * GQA_STRIP, (i + 1) * GQA_STRIP)
            m_wide = jnp.concatenate([m_new[rs, :]] * (tk // LANES), axis=1)
            p = jnp.exp2(s_scr[buf, rs, :] - m_wide)
            sums.append(widen(jnp.sum(p, axis=-1, keepdims=True)))
            p_scr[buf, rs, :] = p.astype(bf16)
        l_scr[...] = a * l_scr[...] + jnp.concatenate(sums, axis=0)
        return a

    def chunk_group(cg, carry):
        sls = []
        for u in range(GQA_GROUP):
            sl = pl.ds(pl.multiple_of((cg * GQA_GROUP + u) * tk, tk), tk)
            sls.append(sl)
            s_scr[u] = _nt_dot(q, k_ref[sl, :])
        for u in range(GQA_GROUP):
            a = softmax_chunk(u)
            acc_scr[...] = acc_scr[...] * a + jnp.dot(p_scr[u], v_ref[sls[u], :],
                                                      preferred_element_type=f32)
        return carry

    lax.fori_loop(0, t // (tk * GQA_GROUP), chunk_group, 0)
    o = (acc_scr[...] / l_scr[...]).astype(o_ref.dtype)
    o_ref[...] = jnp.concatenate([o[h * tq:(h + 1) * tq, :] for h in range(grp)], axis=1)


def _gqa(qc, kc, main, bsz, t):
    n = bsz * t
    tq = GQA_TQ
    tk = min(t, GQA_TK)
    grp = GQA_Q_HEADS // GQA_KV_HEADS
    grp_cols = grp * HEAD_DIM
    rows = grp * tq
    nq = t // tq
    return pl.pallas_call(
        functools.partial(_gqa_kernel, t=t, tq=tq, tk=tk),
        grid=(bsz, GQA_KV_HEADS, nq),
        in_specs=[
            pl.BlockSpec((tq, grp_cols), lambda b, j, i: (b * nq + i, j)),
            pl.BlockSpec((t, HEAD_DIM), lambda b, j, i: (b, j)),
            pl.BlockSpec((t, HEAD_DIM), lambda b, j, i: (b, OFF_CV // HEAD_DIM + j)),
        ],
        out_specs=pl.BlockSpec((tq, grp_cols), lambda b, j, i: (b * nq + i, j)),
        out_shape=jax.ShapeDtypeStruct((n, C_Q), bf16),
        scratch_shapes=[
            pltpu.VMEM((GQA_GROUP, rows, tk), f32),
            pltpu.VMEM((GQA_GROUP, rows, tk), bf16),
            pltpu.VMEM((rows, LANES), f32),
            pltpu.VMEM((rows, LANES), f32),
            pltpu.VMEM((rows, HEAD_DIM), f32),
        ],
        compiler_params=_cparams(("parallel", "parallel", "parallel")),
    )(qc, kc, main)


def _merge_kernel(oa_ref, ob0_ref, ob1_ref, ob2_ref, l0_ref, l1_ref, l2_ref, oc_ref,
                  ga_ref, gb_ref, gc_ref, wa_ref, wb_ref, wc_ref, o_ref):
    heads = []
    for h in range(DIL_HEADS):
        l0, l1, l2 = l0_ref[h], l1_ref[h], l2_ref[h]
        m = jnp.maximum(jnp.maximum(l0, l1), l2)
        e0, e1, e2 = jnp.exp(l0 - m), jnp.exp(l1 - m), jnp.exp(l2 - m)
        mix = (e0 * ob0_ref[h].astype(f32) + e1 * ob1_ref[h].astype(f32)
               + e2 * ob2_ref[h].astype(f32)) / (e0 + e1 + e2)
        heads.append(mix.astype(bf16))
    o_b = jnp.concatenate(heads, axis=1)
    up_a = jnp.dot(oa_ref[...], wa_ref[...], preferred_element_type=f32)
    up_b = jnp.dot(o_b, wb_ref[...], preferred_element_type=f32)
    up_c = jnp.dot(oc_ref[...], wc_ref[...], preferred_element_type=f32)
    merged = (jax.nn.sigmoid(ga_ref[...].astype(f32)) * up_a
              + jax.nn.sigmoid(gb_ref[...].astype(f32)) * up_b
              + jax.nn.sigmoid(gc_ref[...].astype(f32)) * up_c)
    o_ref[...] = merged.astype(o_ref.dtype)


def _merge(o_a, o_bs, lses, o_c, main, w_up_a, w_up_b, w_up_c, layer):
    n = o_a.shape[0]
    tm = TM_TOK
    gate_blk = OFF_GATE // D_MODEL
    hspec = pl.BlockSpec((DIL_HEADS, tm, HEAD_DIM), lambda i: (0, i, 0))
    gspec = lambda which: pl.BlockSpec((tm, D_MODEL), lambda i: (i, gate_blk + which))
    wspec = lambda rows: pl.BlockSpec((None, rows, D_MODEL), lambda i: (layer, 0, 0))
    return pl.pallas_call(
        _merge_kernel,
        grid=(n // tm,),
        in_specs=[
            pl.BlockSpec((tm, A_V), lambda i: (i, 0)),
            hspec, hspec, hspec, hspec, hspec, hspec,
            pl.BlockSpec((tm, C_Q), lambda i: (i, 0)),
            gspec(0), gspec(1), gspec(2),
            wspec(A_V), wspec(B_OUT), wspec(C_Q),
        ],
        out_specs=pl.BlockSpec((tm, D_MODEL), lambda i: (i, 0)),
        out_shape=jax.ShapeDtypeStruct((n, D_MODEL), bf16),
        compiler_params=_cparams(("parallel",)),
    )(o_a, o_bs[0], o_bs[1], o_bs[2], lses[0], lses[1], lses[2], o_c,
      main, main, main, w_up_a, w_up_b, w_up_c)


def _layer_norm_rows(y, g, b):
    mu = jnp.mean(y, axis=-1, keepdims=True)
    d = y - mu
    var = jnp.mean(d * d, axis=-1, keepdims=True)
    return d * lax.rsqrt(var + LN_EPS) * g + b


def _route_rows(logits_t, bias_col):
    scores = jax.nn.sigmoid(logits_t)
    sel = scores + bias_col
    rows = [sel[e:e + 1, :] for e in range(N_EXPERTS)]
    srow = [scores[e:e + 1, :] for e in range(N_EXPERTS)]
    best_val = None
    best_grp = None
    for g in range(N_EXPERT_GROUPS):
        mem = rows[g * EXPERTS_PER_GROUP:(g + 1) * EXPERTS_PER_GROUP]
        top2 = None
        for a in range(EXPERTS_PER_GROUP):
            for b in range(a + 1, EXPERTS_PER_GROUP):
                pair = mem[a] + mem[b]
                top2 = pair if top2 is None else jnp.maximum(top2, pair)
        if g == 0:
            best_val, best_grp = top2, jnp.zeros_like(top2, dtype=jnp.int32)
        else:
            upd = top2 > best_val
            best_val = jnp.where(upd, top2, best_val)
            best_grp = jnp.where(upd, g, best_grp)
    neg = jnp.full_like(best_val, -jnp.inf)
    cand = [jnp.where(best_grp == (e // EXPERTS_PER_GROUP), rows[e], neg) for e in range(N_EXPERTS)]

    def arg_top(vals):
        bv, bi = vals[0], jnp.zeros_like(best_grp)
        for e in range(1, N_EXPERTS):
            upd = vals[e] > bv
            bv = jnp.where(upd, vals[e], bv)
            bi = jnp.where(upd, e, bi)
        return bi

    idx1 = arg_top(cand)
    idx2 = arg_top([jnp.where(idx1 == e, neg, cand[e]) for e in range(N_EXPERTS)])
    zero = jnp.zeros_like(best_val)
    s1 = zero
    s2 = zero
    for e in range(N_EXPERTS):
        s1 = s1 + jnp.where(idx1 == e, srow[e], zero)
        s2 = s2 + jnp.where(idx2 == e, srow[e], zero)
    tot = s1 + s2
    return idx1, idx2, s1 / tot, s2 / tot


def _outproj_kernel(m_ref, w_ref, x_ref, g_ref, b_ref, wr_ref, x1_ref, x1b_ref, lg_ref):
    h = jnp.dot(m_ref[...], w_ref[...], preferred_element_type=f32)
    x1 = _layer_norm_rows(DEEPNORM_ALPHA * x_ref[...] + h, g_ref[...], b_ref[...])
    x1_ref[...] = x1
    x_hi = x1.astype(bf16)
    x1b_ref[...] = x_hi
    x_lo = (x1 - x_hi.astype(f32)).astype(bf16)
    wr = wr_ref[...]
    w_hi = wr.astype(bf16)
    w_lo = (wr - w_hi.astype(f32)).astype(bf16)
    lg_ref[...] = (jnp.dot(x_hi, w_hi, preferred_element_type=f32)
                   + jnp.dot(x_hi, w_lo, preferred_element_type=f32)
                   + jnp.dot(x_lo, w_hi, preferred_element_type=f32))


def _outproj_ln(merged, w_out, x, ln_g, ln_b, w_router_pad, layer):
    n = x.shape[0]
    tm = TM_TOK
    vspec = pl.BlockSpec((None, 1, D_MODEL), lambda i: (layer, 0, 0))
    row = pl.BlockSpec((tm, D_MODEL), lambda i: (i, 0))
    return pl.pallas_call(
        _outproj_kernel,
        grid=(n // tm,),
        in_specs=[
            row,
            pl.BlockSpec((None, D_MODEL, D_MODEL), lambda i: (layer, 0, 0)),
            row,
            vspec, vspec,
            pl.BlockSpec((D_MODEL, LANES), lambda i: (0, 0)),
        ],
        out_specs=[row, row, pl.BlockSpec((tm, LANES), lambda i: (i, 0))],
        out_shape=[
            jax.ShapeDtypeStruct((n, D_MODEL), f32),
            jax.ShapeDtypeStruct((n, D_MODEL), bf16),
            jax.ShapeDtypeStruct((n, LANES), f32),
        ],
        compiler_params=_cparams(("parallel",)),
    )(merged, w_out, x, ln_g, ln_b, w_router_pad)


def _route_kernel(lg_ref, rb_ref, idx_ref, w1_ref, w2_ref):
    logits_t = lg_ref[...].T[0:N_EXPERTS, :]
    i1, i2, w1, w2 = _route_rows(logits_t, rb_ref[...])
    idx_ref[0:1, :] = i1
    idx_ref[1:2, :] = i2
    tr = logits_t.shape[1]
    w1_ref[...] = jnp.broadcast_to(w1, (LANES, tr)).T
    w2_ref[...] = jnp.broadcast_to(w2, (LANES, tr)).T


def _route(logits, router_bias_col):
    n = logits.shape[0]
    tr = TM_ROUTE
    return pl.pallas_call(
        _route_kernel,
        grid=(n // tr,),
        in_specs=[
            pl.BlockSpec((tr, LANES), lambda i: (i, 0)),
            pl.BlockSpec((N_EXPERTS, 1), lambda i: (0, 0)),
        ],
        out_specs=[
            pl.BlockSpec((2, tr), lambda i: (0, i)),
            pl.BlockSpec((tr, LANES), lambda i: (i, 0)),
            pl.BlockSpec((tr, LANES), lambda i: (i, 0)),
        ],
        out_shape=[
            jax.ShapeDtypeStruct((2, n), jnp.int32),
            jax.ShapeDtypeStruct((n, LANES), f32),
            jax.ShapeDtypeStruct((n, LANES), f32),
        ],
        compiler_params=_cparams(("parallel",)),
    )(logits, router_bias_col)


def _moe_kernel(te_ref, nt_ref, *refs, tiles_per_piece):
    del te_ref
    x_refs, (wg_ref, wu_ref, wd_ref, o_ref) = refs[:-4], refs[-4:]
    tile = pl.program_id(0)

    @pl.when(tile < nt_ref[0])
    def _():
        x = x_refs[0][...]
        for p in range(1, len(x_refs)):
            x = jnp.where(tile >= p * tiles_per_piece, x_refs[p][...], x)
        gate = jnp.dot(x, wg_ref[...], preferred_element_type=f32)
        up = jnp.dot(x, wu_ref[...], preferred_element_type=f32)
        h = (gate * jax.nn.sigmoid(gate) * up).astype(bf16)
        o_ref[...] = jnp.dot(h, wd_ref[...], preferred_element_type=f32).astype(o_ref.dtype)

    @pl.when(tile >= nt_ref[0])
    def _():
        o_ref[...] = jnp.zeros_like(o_ref)


def _moe_grouped(x_pieces, tile_expert, n_tiles_used, w_gate, w_up, w_down, layer):
    tpp = x_pieces[0].shape[0] // TM_MOE
    n_tiles = tpp * len(x_pieces)
    wspec_in = pl.BlockSpec((None, None, D_MODEL, D_FF_EXPERT), lambda t, te, nt: (layer, te[t], 0, 0))
    wspec_out = pl.BlockSpec((None, None, D_FF_EXPERT, D_MODEL), lambda t, te, nt: (layer, te[t], 0, 0))

    def piece_spec(p):
        return pl.BlockSpec((TM_MOE, D_MODEL),
                            lambda t, te, nt: (jnp.clip(t - p * tpp, 0, tpp - 1), 0))

    grid_spec = pltpu.PrefetchScalarGridSpec(
        num_scalar_prefetch=2,
        grid=(n_tiles,),
        in_specs=[piece_spec(p) for p in range(len(x_pieces))] + [wspec_in, wspec_in, wspec_out],
        out_specs=pl.BlockSpec((TM_MOE, D_MODEL), lambda t, te, nt: (t, 0)),
    )
    return pl.pallas_call(
        functools.partial(_moe_kernel, tiles_per_piece=tpp),
        grid_spec=grid_spec,
        out_shape=jax.ShapeDtypeStruct((n_tiles * TM_MOE, D_MODEL), bf16),
        compiler_params=_cparams(("arbitrary",)),
    )(tile_expert, n_tiles_used, *x_pieces, w_gate, w_up, w_down)


def _combine_kernel(x_ref, *refs, n_pieces, tiles_per_piece):
    y1_refs, y2_refs = refs[:n_pieces], refs[n_pieces:2 * n_pieces]
    w1_ref, w2_ref, g_ref, b_ref, o_ref, ob_ref = refs[2 * n_pieces:]
    tile = pl.program_id(0)

    def pick(piece_refs):
        y = piece_refs[0][...]
        for p in range(1, n_pieces):
            y = jnp.where(tile >= p * tiles_per_piece, piece_refs[p][...], y)
        return y

    y1 = pick(y1_refs)
    y2 = pick(y2_refs)
    w1 = w1_ref[...]
    w2 = w2_ref[...]
    for c in range(D_MODEL // LANES):
        ls = slice(c * LANES, (c + 1) * LANES)
        o_ref[:, ls] = (DEEPNORM_ALPHA * x_ref[:, ls] + w1 * y1[:, ls].astype(f32)
                        + w2 * y2[:, ls].astype(f32))
    x2 = _layer_norm_rows(o_ref[...], g_ref[...], b_ref[...])
    o_ref[...] = x2
    ob_ref[...] = x2.astype(bf16)


def _combine_ln(x1, y1_pieces, y2_pieces, w1b, w2b, ln_g, ln_b, layer):
    n = x1.shape[0]
    tm = TM_TOK
    n_pieces = len(y1_pieces)
    tpp = y1_pieces[0].shape[0] // tm
    row = pl.BlockSpec((tm, D_MODEL), lambda i: (i, 0))
    wsp = pl.BlockSpec((tm, LANES), lambda i: (i, 0))
    vspec = pl.BlockSpec((None, 1, D_MODEL), lambda i: (layer, 0, 0))

    def piece_spec(p):
        return pl.BlockSpec((tm, D_MODEL), lambda i: (jnp.clip(i - p * tpp, 0, tpp - 1), 0))

    pieces = [piece_spec(p) for p in range(n_pieces)]
    return pl.pallas_call(
        functools.partial(_combine_kernel, n_pieces=n_pieces, tiles_per_piece=tpp),
        grid=(n // tm,),
        in_specs=[row] + pieces + pieces + [wsp, wsp, vspec, vspec],
        out_specs=[row, row],
        out_shape=[jax.ShapeDtypeStruct((n, D_MODEL), f32), jax.ShapeDtypeStruct((n, D_MODEL), bf16)],
        compiler_params=_cparams(("parallel",)),
    )(x1, *y1_pieces, *y2_pieces, w1b, w2b, ln_g, ln_b)


def _dispatch_plan(idx):
    n = idx.shape[1]
    m_pad = 2 * n + N_EXPERTS * TM_MOE
    n_tiles = m_pad // TM_MOE
    e_flat = idx.reshape(-1)
    onehot = (e_flat[:, None] == jnp.arange(N_EXPERTS, dtype=jnp.int32)[None, :]).astype(jnp.int32)
    rank = jnp.sum((jnp.cumsum(onehot, axis=0) - onehot) * onehot, axis=1)
    counts = jnp.sum(onehot, axis=0)
    tiles_per = (counts + TM_MOE - 1) // TM_MOE
    tile_end = jnp.cumsum(tiles_per)
    start_row = (tile_end - tiles_per) * TM_MOE
    dest = start_row.at[e_flat].get(mode="promise_in_bounds") + rank
    tok = jnp.tile(jnp.arange(n, dtype=jnp.int32), 2)
    src = jnp.zeros((m_pad,), jnp.int32).at[dest].set(tok, mode="promise_in_bounds", unique_indices=True)
    n_used = tile_end[-1]
    tile_ids = jnp.minimum(jnp.arange(n_tiles, dtype=jnp.int32), n_used - 1)
    tile_expert = jnp.sum((tile_ids[:, None] >= tile_end[None, :]).astype(jnp.int32), axis=1)
    tile_expert = jnp.minimum(tile_expert, N_EXPERTS - 1)
    return src, dest, tile_expert, n_used.reshape(1).astype(jnp.int32)


def _rope_tables(pos, dim):
    inv_freq = ROPE_THETA ** (-jnp.arange(0, dim, 2, dtype=f32) / dim)
    ang = pos.astype(f32)[:, None] * inv_freq[None, :]
    return jnp.cos(ang), jnp.sin(ang)


def _tables(t):
    cos, sin = _rope_tables(jnp.arange(t), HEAD_DIM)
    cos_b = jnp.concatenate([cos, cos], axis=-1)
    sin_b = jnp.concatenate([-sin, sin], axis=-1)
    tok = jnp.arange(t)
    cr, sr = _rope_tables(tok // GRID_W, HEAD_DIM // 2)
    cc, sc = _rope_tables(tok % GRID_W, HEAD_DIM // 2)
    cos_c = jnp.concatenate([cr, cr, cc, cc], axis=-1)
    sin_c = jnp.concatenate([-sr, sr, -sc, sc], axis=-1)
    return cos_b, sin_b, cos_c, sin_c


def _repack_w_in(w_in):
    offs = [0]
    for s in IN_SPLITS:
        offs.append(offs[-1] + s)
    seg = [w_in[:, :, offs[i]:offs[i + 1]] for i in range(len(IN_SPLITS))]
    a_q, a_k, a_v, a_r, a_lr, b_qkv, c_q, c_k, c_v, gate = seg
    lr_pad = jnp.pad(a_lr, ((0, 0), (0, 0), (0, LR_PAD - A_LR)))
    w_main = jnp.concatenate([gate, a_q, a_k, a_v, a_r, c_q, c_k, c_v, lr_pad], axis=-1)
    w_main = jnp.pad(w_main, ((0, 0), (0, 0), (0, MAIN_COLS - MAIN_USED))).astype(bf16)
    return w_main, b_qkv.astype(bf16)


def _mixer(xb, wts, layer, bsz, t, tables):
    cos_b, sin_b, cos_c, sin_c = tables
    main = _proj_main(xb, wts["w_main"], layer)
    qkv_b = _proj_b(xb, wts["w_b"], cos_b, sin_b, layer, t)
    o_a = _gla(main, wts["gla_w2_f"], wts["gla_b_f"], wts["gla_w2_b"], wts["gla_b_b"],
               wts["gla_norm_g"], layer, bsz, t)
    o_bs, lses = [], []
    for group in range(len(DIL_CONFIGS)):
        o_g, lse_g = _dilated(qkv_b, group, bsz, t)
        o_bs.append(o_g)
        lses.append(lse_g)
    qc, kc = _c_prep(main, cos_c, sin_c, wts["q_norm_g"], wts["k_norm_g"], layer, t)
    o_c = _gqa(qc, kc, main, bsz, t)
    return _merge(o_a, o_bs, lses, o_c, main, wts["w_up_a"], wts["w_up_b"], wts["w_up_c"], layer)


def _gather_rows(a, rows):
    return a.at[rows].get(mode="promise_in_bounds")


def _moe(x1, x1b, idx, w1b, w2b, wts, layer):
    n = x1.shape[0]
    src, dest, tile_expert, n_used = _dispatch_plan(idx)
    n_pieces = -(-src.shape[0] // MOE_GATHER_ROWS)
    x_pieces = [_gather_rows(x1b, piece) for piece in jnp.split(src, n_pieces)]
    y_sorted = _moe_grouped(x_pieces, tile_expert, n_used, wts["moe_w_gate"], wts["moe_w_up"],
                            wts["moe_w_down"], layer)
    c_pieces = -(-n // COMBINE_GATHER_ROWS)
    y1 = [_gather_rows(y_sorted, piece) for piece in jnp.split(dest[:n], c_pieces)]
    y2 = [_gather_rows(y_sorted, piece) for piece in jnp.split(dest[n:], c_pieces)]
    return _combine_ln(x1, y1, y2, w1b, w2b, wts["ln2_g"], wts["ln2_b"], layer)


def _trunk(x3, wts, depth=DEPTH):
    bsz, t, _ = x3.shape
    n = bsz * t
    x = x3.reshape(n, D_MODEL)
    xb = x.astype(bf16)
    tables = _tables(t)
    for layer in range(depth):
        merged = _mixer(xb, wts, layer, bsz, t, tables)
        x1, x1b, logits = _outproj_ln(merged, wts["w_out"], x, wts["ln1_g"], wts["ln1_b"],
                                      wts["w_router_pad"], layer)
        idx, w1b, w2b = _route(logits, wts["router_bias_col"])
        x, xb = _moe(x1, x1b, idx, w1b, w2b, wts, layer)
    return x.reshape(bsz, t, D_MODEL)


def _cast_kernel(x_ref, o_ref):
    o_ref[...] = x_ref[...].astype(o_ref.dtype)


def _expert_weights_bf16(w):
    n_layers, n_exp, rows, cols = w.shape
    spec = pl.BlockSpec((None, None, rows, cols), lambda i, j: (i, j, 0, 0))
    return pl.pallas_call(
        _cast_kernel,
        grid=(n_layers, n_exp),
        in_specs=[spec],
        out_specs=spec,
        out_shape=jax.ShapeDtypeStruct(w.shape, bf16),
        compiler_params=_cparams(("parallel", "parallel")),
    )(w)


def _prepare_weights(w_in, gla_w2_f, gla_b_f, gla_w2_b, gla_b_b, gla_norm_g, q_norm_g, k_norm_g,
                     w_up_a, w_up_b, w_up_c, w_out, ln1_g, ln1_b, w_router, router_bias,
                     moe_w_gate, moe_w_up, moe_w_down, ln2_g, ln2_b):
    w_main, w_b = _repack_w_in(w_in)
    row = lambda a: a.reshape(DEPTH, 1, a.shape[-1]).astype(f32)
    return {
        "w_main": w_main,
        "w_b": w_b,
        "gla_w2_f": gla_w2_f.astype(f32),
        "gla_b_f": row(gla_b_f),
        "gla_w2_b": gla_w2_b.astype(f32),
        "gla_b_b": row(gla_b_b),
        "gla_norm_g": row(gla_norm_g),
        "q_norm_g": row(q_norm_g),
        "k_norm_g": row(k_norm_g),
        "w_up_a": w_up_a.astype(bf16),
        "w_up_b": w_up_b.astype(bf16),
        "w_up_c": w_up_c.astype(bf16),
        "w_out": w_out.astype(bf16),
        "ln1_g": row(ln1_g),
        "ln1_b": row(ln1_b),
        "w_router_pad": jnp.pad(w_router.astype(f32), ((0, 0), (0, LANES - N_EXPERTS))),
        "router_bias_col": router_bias.astype(f32).reshape(N_EXPERTS, 1),
        "moe_w_gate": _expert_weights_bf16(moe_w_gate.astype(f32)),
        "moe_w_up": _expert_weights_bf16(moe_w_up.astype(f32)),
        "moe_w_down": _expert_weights_bf16(moe_w_down.astype(f32)),
        "ln2_g": row(ln2_g),
        "ln2_b": row(ln2_b),
    }


def kernel(x_prompt, x_sample, w_in, gla_w2_f, gla_b_f, gla_w2_b, gla_b_b, gla_norm_g, q_norm_g, k_norm_g, w_up_a, w_up_b, w_up_c, w_out, ln1_g, ln1_b, w_router, router_bias, moe_w_gate, moe_w_up, moe_w_down, ln2_g, ln2_b):
    wts = _prepare_weights(w_in, gla_w2_f, gla_b_f, gla_w2_b, gla_b_b, gla_norm_g, q_norm_g, k_norm_g,
                           w_up_a, w_up_b, w_up_c, w_out, ln1_g, ln1_b, w_router, router_bias,
                           moe_w_gate, moe_w_up, moe_w_down, ln2_g, ln2_b)
    return _trunk(x_prompt, wts), _trunk(x_sample, wts)
```

```python
import functools

import jax
import jax.numpy as jnp
from jax import lax
from jax.experimental import pallas as pl
from jax.experimental.pallas import tpu as pltpu

D_MODEL = 2048
DEPTH = 4
HEAD_DIM = 128
GRID_W = 64
ROPE_THETA = 10000.0
LN_EPS = 1e-5
RMS_EPS = 1e-6
GLA_HEADS = 4
GLA_DK = 128
GLA_DV = 256
GLA_RANK = 16
GLA_TAU = 16.0
GLA_CHUNK = 64
DIL_CONFIGS = ((128, 1), (512, 4), (2048, 16))
DIL_HEADS = 4
GQA_Q_HEADS = 8
GQA_KV_HEADS = 2
N_EXPERTS = 16
N_EXPERT_GROUPS = 4
EXPERTS_PER_GROUP = N_EXPERTS // N_EXPERT_GROUPS
D_FF_EXPERT = 1024
DEEPNORM_ALPHA = (2.0 * DEPTH) ** 0.25

A_QK = GLA_HEADS * GLA_DK
A_V = GLA_HEADS * GLA_DV
A_LR = 2 * GLA_RANK
B_HEADS = len(DIL_CONFIGS) * DIL_HEADS
B_QKV = 3 * B_HEADS * HEAD_DIM
B_OUT = DIL_HEADS * HEAD_DIM
C_Q = GQA_Q_HEADS * HEAD_DIM
C_KV = GQA_KV_HEADS * HEAD_DIM
GATE_COLS = 3 * D_MODEL
IN_SPLITS = (A_QK, A_QK, A_V, A_V, A_LR, B_QKV, C_Q, C_KV, C_KV, GATE_COLS)

LANES = 128
VMEM_LIMIT = 56 * 1024 * 1024
LR_PAD = LANES

OFF_GATE = 0
OFF_AQ = OFF_GATE + GATE_COLS
OFF_AK = OFF_AQ + A_QK
OFF_AV = OFF_AK + A_QK
OFF_AR = OFF_AV + A_V
OFF_CQ = OFF_AR + A_V
OFF_CK = OFF_CQ + C_Q
OFF_CV = OFF_CK + C_KV
OFF_LR = OFF_CV + C_KV
MAIN_USED = OFF_LR + LR_PAD

TM_PROJ = 1024
TN_MAIN = 1024
TM_ROUTE = 2048
MAIN_COLS = -(-MAIN_USED // TN_MAIN) * TN_MAIN
TN_B = B_HEADS * HEAD_DIM
TM_TOK = 256
TM_MOE = 512
MOE_GATHER_ROWS = 24576
COMBINE_GATHER_ROWS = 16384
TM_MAIN = 2048
GQA_TQ = 512
GQA_TK = 1024
GQA_GROUP = 2
GQA_STRIP = 32
LOG2_E = 1.4426950408889634
GLA_SUPER = 8
DIL_UNROLL = 8
DIL_HALF = 64
NEG_BIG = -1e30

f32 = jnp.float32
bf16 = jnp.bfloat16


def _cparams(sem):
    return pltpu.CompilerParams(dimension_semantics=sem, vmem_limit_bytes=VMEM_LIMIT)


def _nt_dot(a, b):
    return lax.dot_general(a, b, (((1,), (1,)), ((), ())), preferred_element_type=f32)


def _tn_dot(a, b):
    return lax.dot_general(a, b, (((0,), (0,)), ((), ())), preferred_element_type=f32)


def _axial_partner(x):
    lane = lax.broadcasted_iota(jnp.int32, x.shape, 1)
    quarter = HEAD_DIM // 4
    first = (lane % (2 * quarter)) < quarter
    return jnp.where(first, pltpu.roll(x, HEAD_DIM - quarter, 1), pltpu.roll(x, quarter, 1))


def _mm_kernel(x_ref, w_ref, o_ref):
    o_ref[...] = jnp.dot(x_ref[...], w_ref[...], preferred_element_type=f32).astype(o_ref.dtype)


def _proj_main(xb, w_main, layer):
    n, k = xb.shape
    grid = (MAIN_COLS // TN_MAIN, n // TM_MAIN)
    return pl.pallas_call(
        _mm_kernel,
        grid=grid,
        in_specs=[
            pl.BlockSpec((TM_MAIN, k), lambda j, i: (i, 0)),
            pl.BlockSpec((None, k, TN_MAIN), lambda j, i: (layer, 0, j)),
        ],
        out_specs=pl.BlockSpec((TM_MAIN, TN_MAIN), lambda j, i: (i, j)),
        out_shape=jax.ShapeDtypeStruct((n, MAIN_COLS), bf16),
        compiler_params=_cparams(("parallel", "parallel")),
    )(xb, w_main)


def _proj_b_kernel(x_ref, w_ref, cos_ref, sin_ref, o_ref):
    j = pl.program_id(0)
    pair = 2 * HEAD_DIM

    def emit(rotary, scale):
        x = x_ref[...]
        if rotary:
            cos = cos_ref[...] * scale
            sin = sin_ref[...] * scale
        for c in range(TN_B // pair):
            acc = jnp.dot(x, w_ref[:, c * pair:(c + 1) * pair], preferred_element_type=f32)
            for h in range(2):
                a = acc[:, h * HEAD_DIM:(h + 1) * HEAD_DIM]
                if rotary:
                    a = a * cos + pltpu.roll(a, HEAD_DIM // 2, 1) * sin
                o_ref[2 * c + h] = a.astype(o_ref.dtype)

    @pl.when(j == 0)
    def _():
        emit(True, HEAD_DIM ** -0.5)

    @pl.when(j == 1)
    def _():
        emit(True, 1.0)

    @pl.when(j == 2)
    def _():
        emit(False, 1.0)


def _proj_b(xb, w_b, cos_b, sin_b, layer, t):
    n, k = xb.shape
    tpb = t // TM_PROJ
    grid = (B_QKV // TN_B, n // TM_PROJ)
    hpt = TN_B // HEAD_DIM
    return pl.pallas_call(
        _proj_b_kernel,
        grid=grid,
        in_specs=[
            pl.BlockSpec((TM_PROJ, k), lambda j, i: (i, 0)),
            pl.BlockSpec((None, k, TN_B), lambda j, i: (layer, 0, j)),
            pl.BlockSpec((TM_PROJ, HEAD_DIM), lambda j, i: (i % tpb, 0)),
            pl.BlockSpec((TM_PROJ, HEAD_DIM), lambda j, i: (i % tpb, 0)),
        ],
        out_specs=pl.BlockSpec((hpt, TM_PROJ, HEAD_DIM), lambda j, i: (j, i, 0)),
        out_shape=jax.ShapeDtypeStruct((3 * B_HEADS, n, HEAD_DIM), bf16),
        compiler_params=_cparams(("parallel", "parallel")),
    )(xb, w_b, cos_b, sin_b)


def _gla_kernel(q_ref, k_ref, v_ref, r_ref, lr_ref, w2f_ref, bf_ref, w2b_ref, bb_ref, g_ref,
                o_ref, accf_ref, accb_ref, *, t):
    c_len = GLA_CHUNK
    sup = GLA_SUPER * c_len
    n_sup = t // sup
    row = lax.broadcasted_iota(jnp.int32, (sup, sup), 0)
    col = lax.broadcasted_iota(jnp.int32, (sup, sup), 1)
    same = (row // c_len) == (col // c_len)

    def stage_decay(base, fwd):
        w2 = (w2f_ref if fwd else w2b_ref)[...].astype(bf16)
        bias = (bf_ref if fwd else bb_ref)[...]
        lo = 0 if fwd else GLA_RANK
        lr = lr_ref[pl.ds(base, sup), :][:, lo:lo + GLA_RANK]
        z = jnp.dot(lr, w2, preferred_element_type=f32) + bias
        g = (jnp.minimum(z, 0.0) - jnp.log(1.0 + jnp.exp(-jnp.abs(z)))) * (1.0 / GLA_TAU)
        g_hi = g.astype(bf16)
        g_lo = (g - g_hi.astype(f32)).astype(bf16)
        return g_hi, g_lo

    def stage_cumsum(parts, fwd):
        g_hi, g_lo = parts
        tri = jnp.where(same & ((row >= col) if fwd else (row <= col)), 1.0, 0.0).astype(bf16)
        b = (jnp.dot(tri, g_hi, preferred_element_type=f32)
             + jnp.dot(tri, g_lo, preferred_element_type=f32))
        edge = c_len - 1 if fwd else 0
        tot = jnp.concatenate(
            [jnp.broadcast_to(b[c * c_len + edge:c * c_len + edge + 1, :], (c_len, GLA_DK))
             for c in range(GLA_SUPER)], axis=0)
        return b, tot

    def stage_scale(base, b, tot):
        sl = pl.ds(base, sup)
        q = q_ref[sl, :].astype(f32) * (GLA_DK ** -0.5)
        k = k_ref[sl, :].astype(f32)
        q_t = (q * jnp.exp(b)).astype(bf16)
        k_t = (k * jnp.exp(-b)).astype(bf16)
        k_s = (k * jnp.exp(tot - b)).astype(bf16)
        return q_t, k_t, k_s, jnp.exp(tot)

    def stage_intra(base, q_t, k_t, fwd):
        smask = same & ((col <= row) if fwd else (col > row))
        scores = jnp.where(smask, _nt_dot(q_t, k_t), 0.0)
        return jnp.dot(scores.astype(bf16), v_ref[pl.ds(base, sup), :], preferred_element_type=f32)

    def chunk_step(base, c, fwd, q_t, k_s, dec, o_intra, s_t):
        acc_ref = accf_ref if fwd else accb_ref
        rs = slice(c * c_len, (c + 1) * c_len)
        rows_c = pl.ds(base + c * c_len, c_len)
        acc_ref[rows_c, :] = o_intra[rs, :] + _nt_dot(q_t[rs, :], s_t.astype(bf16))
        return s_t * dec[c * c_len:c * c_len + 1, :] + _tn_dot(v_ref[rows_c, :], k_s[rs, :])

    def body(i, carry):
        s_f, s_b = carry
        base_f = pl.multiple_of(i * sup, sup)
        base_b = pl.multiple_of((n_sup - 1 - i) * sup, sup)
        parts_f = stage_decay(base_f, True)
        parts_b = stage_decay(base_b, False)
        b_f, tot_f = stage_cumsum(parts_f, True)
        b_b, tot_b = stage_cumsum(parts_b, False)
        qt_f, kt_f, ks_f, dec_f = stage_scale(base_f, b_f, tot_f)
        qt_b, kt_b, ks_b, dec_b = stage_scale(base_b, b_b, tot_b)
        oi_f = stage_intra(base_f, qt_f, kt_f, True)
        oi_b = stage_intra(base_b, qt_b, kt_b, False)
        for c in range(GLA_SUPER):
            s_f = chunk_step(base_f, c, True, qt_f, ks_f, dec_f, oi_f, s_f)
            s_b = chunk_step(base_b, GLA_SUPER - 1 - c, False, qt_b, ks_b, dec_b, oi_b, s_b)
        return s_f, s_b

    zero = jnp.zeros((GLA_DV, GLA_DK), f32)
    lax.fori_loop(0, n_sup, body, (zero, zero))

    rows = 256
    gain = g_ref[...]

    def finish(i, carry):
        sl = pl.ds(pl.multiple_of(i * rows, rows), rows)
        x = accf_ref[sl, :] + accb_ref[sl, :]
        ms = jnp.mean(x * x, axis=-1, keepdims=True)
        y = x * lax.rsqrt(ms + RMS_EPS) * gain
        r = r_ref[sl, :].astype(f32)
        o_ref[sl, :] = (y * (r * jax.nn.sigmoid(r))).astype(o_ref.dtype)
        return carry

    lax.fori_loop(0, t // rows, finish, 0)


def _gla(main, w2f, b_f, w2b, b_b, gain, layer, bsz, t):
    n = bsz * t
    qk_blk = lambda off: (lambda b, h: (b, off // GLA_DK + h))
    v_blk = lambda off: (lambda b, h: (b, off // GLA_DV + h))
    wspec = pl.BlockSpec((None, GLA_RANK, GLA_DK), lambda b, h: (layer, 0, h))
    bspec = pl.BlockSpec((None, 1, GLA_DK), lambda b, h: (layer, 0, h))
    return pl.pallas_call(
        functools.partial(_gla_kernel, t=t),
        grid=(bsz, GLA_HEADS),
        in_specs=[
            pl.BlockSpec((t, GLA_DK), qk_blk(OFF_AQ)),
            pl.BlockSpec((t, GLA_DK), qk_blk(OFF_AK)),
            pl.BlockSpec((t, GLA_DV), v_blk(OFF_AV)),
            pl.BlockSpec((t, GLA_DV), v_blk(OFF_AR)),
            pl.BlockSpec((t, LR_PAD), lambda b, h: (b, OFF_LR // LR_PAD)),
            wspec, bspec, wspec, bspec,
            pl.BlockSpec((None, 1, GLA_DV), lambda b, h: (layer, 0, h)),
        ],
        out_specs=pl.BlockSpec((t, GLA_DV), lambda b, h: (b, h)),
        out_shape=jax.ShapeDtypeStruct((n, A_V), bf16),
        scratch_shapes=[pltpu.VMEM((t, GLA_DV), f32), pltpu.VMEM((t, GLA_DV), f32)],
        compiler_params=_cparams(("parallel", "parallel")),
    )(main, main, main, main, main, w2f, b_f, w2b, b_b, gain)


def _dil_kernel(q_ref, k_ref, v_ref, o_ref, lse_ref, *scratch, t, dil):
    n_sub = t // dil
    bq = min(128, n_sub)
    win = min(bq + 2 * DIL_HALF, n_sub)
    n_blk = n_sub // bq
    qi = lax.broadcasted_iota(jnp.int32, (bq, win), 0)
    ki = lax.broadcasted_iota(jnp.int32, (bq, win), 1)
    cvt = 512

    if dil > 1:
        q32, k32, v32, o32 = scratch

        def widen(i, carry):
            sl = pl.ds(pl.multiple_of(i * cvt, cvt), cvt)
            q32[sl, :] = q_ref[0, sl, :].astype(f32)
            k32[sl, :] = k_ref[0, sl, :].astype(f32)
            v32[sl, :] = v_ref[0, sl, :].astype(f32)
            return carry

        lax.fori_loop(0, t // cvt, widen, 0)

    def body(idx, carry):
        r = idx // n_blk
        i = idx % n_blk
        q0 = i * bq
        k0 = jnp.clip(q0 - DIL_HALF, 0, n_sub - win)
        if dil == 1:
            qs = pl.ds(pl.multiple_of(q0, bq), bq)
            ks = pl.ds(pl.multiple_of(k0, DIL_HALF), win)
            q = q_ref[0, qs, :]
            k = k_ref[0, ks, :]
            v = v_ref[0, ks, :]
        else:
            qs = pl.ds(r + q0 * dil, bq, stride=dil)
            ks = pl.ds(r + k0 * dil, win, stride=dil)
            q = q32[qs, :].astype(bf16)
            k = k32[ks, :].astype(bf16)
            v = v32[ks, :].astype(bf16)
        s = _nt_dot(q, k)
        valid = jnp.abs((q0 + qi) - (k0 + ki)) <= DIL_HALF
        s = jnp.where(valid, s, NEG_BIG)
        m = jnp.max(s, axis=-1, keepdims=True)
        p = jnp.exp(s - m)
        l = jnp.sum(p, axis=-1, keepdims=True)
        o = jnp.dot(p.astype(bf16), v, preferred_element_type=f32) / l
        lse = jnp.broadcast_to(m + jnp.log(l), (bq, HEAD_DIM))
        if dil == 1:
            o_ref[0, qs, :] = o.astype(o_ref.dtype)
            lse_ref[0, qs, :] = lse
        else:
            o32[qs, :] = o
            lse_ref[0, qs, :] = lse
        return carry

    def body_group(j, carry):
        for u in range(DIL_UNROLL):
            body(j * DIL_UNROLL + u, carry)
        return carry

    lax.fori_loop(0, dil * n_blk // DIL_UNROLL, body_group, 0)

    if dil > 1:
        def narrow(i, carry):
            sl = pl.ds(pl.multiple_of(i * cvt, cvt), cvt)
            o_ref[0, sl, :] = o32[sl, :].astype(o_ref.dtype)
            return carry

        lax.fori_loop(0, t // cvt, narrow, 0)


def _dilated(qkv_b, group, bsz, t):
    _, dil = DIL_CONFIGS[group]
    n = bsz * t
    blk = (1, t, HEAD_DIM)
    head0 = group * DIL_HEADS
    scratch = [pltpu.VMEM((t, HEAD_DIM), f32)] * 4 if dil > 1 else []
    return pl.pallas_call(
        functools.partial(_dil_kernel, t=t, dil=dil),
        grid=(bsz, DIL_HEADS),
        in_specs=[
            pl.BlockSpec(blk, lambda b, h: (head0 + h, b, 0)),
            pl.BlockSpec(blk, lambda b, h: (B_HEADS + head0 + h, b, 0)),
            pl.BlockSpec(blk, lambda b, h: (2 * B_HEADS + head0 + h, b, 0)),
        ],
        out_specs=[
            pl.BlockSpec(blk, lambda b, h: (h, b, 0)),
            pl.BlockSpec(blk, lambda b, h: (h, b, 0)),
        ],
        out_shape=[
            jax.ShapeDtypeStruct((DIL_HEADS, n, HEAD_DIM), bf16),
            jax.ShapeDtypeStruct((DIL_HEADS, n, HEAD_DIM), f32),
        ],
        scratch_shapes=scratch,
        compiler_params=_cparams(("parallel", "parallel")),
    )(qkv_b, qkv_b, qkv_b)


def _c_prep_kernel(q_ref, k_ref, cos_ref, sin_ref, qg_ref, kg_ref, qo_ref, ko_ref):
    cos = cos_ref[...]
    sin = sin_ref[...]

    def prep(x, gain, scale):
        x = x.astype(f32)
        ms = jnp.mean(x * x, axis=-1, keepdims=True)
        y = x * lax.rsqrt(ms + RMS_EPS) * gain
        return (y * cos + _axial_partner(y) * sin) * scale

    qg = qg_ref[...]
    kg = kg_ref[...]
    for h in range(GQA_Q_HEADS):
        ls = slice(h * HEAD_DIM, (h + 1) * HEAD_DIM)
        qo_ref[:, ls] = prep(q_ref[:, ls], qg, HEAD_DIM ** -0.5 * LOG2_E).astype(qo_ref.dtype)
    for h in range(GQA_KV_HEADS):
        ls = slice(h * HEAD_DIM, (h + 1) * HEAD_DIM)
        ko_ref[:, ls] = prep(k_ref[:, ls], kg, 1.0).astype(ko_ref.dtype)


def _c_prep(main, cos_c, sin_c, q_gain, k_gain, layer, t):
    n = main.shape[0]
    tm = TM_PROJ
    tpb = t // tm
    gspec = pl.BlockSpec((None, 1, HEAD_DIM), lambda i: (layer, 0, 0))
    return pl.pallas_call(
        _c_prep_kernel,
        grid=(n // tm,),
        in_specs=[
            pl.BlockSpec((tm, C_Q), lambda i: (i, OFF_CQ // C_Q)),
            pl.BlockSpec((tm, C_KV), lambda i: (i, OFF_CK // C_KV)),
            pl.BlockSpec((tm, HEAD_DIM), lambda i: (i % tpb, 0)),
            pl.BlockSpec((tm, HEAD_DIM), lambda i: (i % tpb, 0)),
            gspec, gspec,
        ],
        out_specs=[
            pl.BlockSpec((tm, C_Q), lambda i: (i, 0)),
            pl.BlockSpec((tm, C_KV), lambda i: (i, 0)),
        ],
        out_shape=[
            jax.ShapeDtypeStruct((n, C_Q), bf16),
            jax.ShapeDtypeStruct((n, C_KV), bf16),
        ],
        compiler_params=_cparams(("parallel",)),
    )(main, main, cos_c, sin_c, q_gain, k_gain)


def _gqa_kernel(q_ref, k_ref, v_ref, o_ref, s_scr, p_scr, m_scr, l_scr, acc_scr, *, t, tq, tk):
    grp = GQA_Q_HEADS // GQA_KV_HEADS
    q4 = q_ref[...]
    q = jnp.concatenate([q4[:, h * HEAD_DIM:(h + 1) * HEAD_DIM] for h in range(grp)], axis=0)
    rows = grp * tq
    n_strips = rows // GQA_STRIP
    m_scr[...] = jnp.full((rows, LANES), NEG_BIG, f32)
    l_scr[...] = jnp.zeros((rows, LANES), f32)
    acc_scr[...] = jnp.zeros((rows, HEAD_DIM), f32)

    def widen(col):
        return jnp.broadcast_to(col, (col.shape[0], LANES))

    def softmax_chunk(buf):
        mx = []
        for i in range(n_strips):
            rs = slice(i * GQA_STRIP, (i + 1) * GQA_STRIP)
            mx.append(widen(jnp.max(s_scr[buf, rs, :], axis=-1, keepdims=True)))
        m_old = m_scr[...]
        m_new = jnp.maximum(m_old, jnp.concatenate(mx, axis=0))
        a = jnp.exp2(m_old - m_new)
        m_scr[...] = m_new
        sums = []
        for i in range(n_strips):
            rs = slice(i * GQA_STRIP, (i + 1) * GQA_STRIP)
            m_wide = jnp.concatenate([m_new[rs, :]] * (tk // LANES), axis=1)
            p = jnp.exp2(s_scr[buf, rs, :] - m_wide)
            sums.append(widen(jnp.sum(p, axis=-1, keepdims=True)))
            p_scr[buf, rs, :] = p.astype(bf16)
        l_scr[...] = a * l_scr[...] + jnp.concatenate(sums, axis=0)
        return a

    def chunk_group(cg, carry):
        sls = []
        for u in range(GQA_GROUP):
            sl = pl.ds(pl.multiple_of((cg * GQA_GROUP + u) * tk, tk), tk)
            sls.append(sl)
            s_scr[u] = _nt_dot(q, k_ref[sl, :])
        for u in range(GQA_GROUP):
            a = softmax_chunk(u)
            acc_scr[...] = acc_scr[...] * a + jnp.dot(p_scr[u], v_ref[sls[u], :],
                                                      preferred_element_type=f32)
        return carry

    lax.fori_loop(0, t // (tk * GQA_GROUP), chunk_group, 0)
    o = (acc_scr[...] / l_scr[...]).astype(o_ref.dtype)
    o_ref[...] = jnp.concatenate([o[h * tq:(h + 1) * tq, :] for h in range(grp)], axis=1)


def _gqa(qc, kc, main, bsz, t):
    n = bsz * t
    tq = GQA_TQ
    tk = min(t, GQA_TK)
    grp = GQA_Q_HEADS // GQA_KV_HEADS
    grp_cols = grp * HEAD_DIM
    rows = grp * tq
    nq = t // tq
    return pl.pallas_call(
        functools.partial(_gqa_kernel, t=t, tq=tq, tk=tk),
        grid=(bsz, GQA_KV_HEADS, nq),
        in_specs=[
            pl.BlockSpec((tq, grp_cols), lambda b, j, i: (b * nq + i, j)),
            pl.BlockSpec((t, HEAD_DIM), lambda b, j, i: (b, j)),
            pl.BlockSpec((t, HEAD_DIM), lambda b, j, i: (b, OFF_CV // HEAD_DIM + j)),
        ],
        out_specs=pl.BlockSpec((tq, grp_cols), lambda b, j, i: (b * nq + i, j)),
        out_shape=jax.ShapeDtypeStruct((n, C_Q), bf16),
        scratch_shapes=[
            pltpu.VMEM((GQA_GROUP, rows, tk), f32),
            pltpu.VMEM((GQA_GROUP, rows, tk), bf16),
            pltpu.VMEM((rows, LANES), f32),
            pltpu.VMEM((rows, LANES), f32),
            pltpu.VMEM((rows, HEAD_DIM), f32),
        ],
        compiler_params=_cparams(("parallel", "parallel", "parallel")),
    )(qc, kc, main)


def _merge_kernel(oa_ref, ob0_ref, ob1_ref, ob2_ref, l0_ref, l1_ref, l2_ref, oc_ref,
                  ga_ref, gb_ref, gc_ref, wa_ref, wb_ref, wc_ref, o_ref):
    heads = []
    for h in range(DIL_HEADS):
        l0, l1, l2 = l0_ref[h], l1_ref[h], l2_ref[h]
        m = jnp.maximum(jnp.maximum(l0, l1), l2)
        e0, e1, e2 = jnp.exp(l0 - m), jnp.exp(l1 - m), jnp.exp(l2 - m)
        mix = (e0 * ob0_ref[h].astype(f32) + e1 * ob1_ref[h].astype(f32)
               + e2 * ob2_ref[h].astype(f32)) / (e0 + e1 + e2)
        heads.append(mix.astype(bf16))
    o_b = jnp.concatenate(heads, axis=1)
    up_a = jnp.dot(oa_ref[...], wa_ref[...], preferred_element_type=f32)
    up_b = jnp.dot(o_b, wb_ref[...], preferred_element_type=f32)
    up_c = jnp.dot(oc_ref[...], wc_ref[...], preferred_element_type=f32)
    merged = (jax.nn.sigmoid(ga_ref[...].astype(f32)) * up_a
              + jax.nn.sigmoid(gb_ref[...].astype(f32)) * up_b
              + jax.nn.sigmoid(gc_ref[...].astype(f32)) * up_c)
    o_ref[...] = merged.astype(o_ref.dtype)


def _merge(o_a, o_bs, lses, o_c, main, w_up_a, w_up_b, w_up_c, layer):
    n = o_a.shape[0]
    tm = TM_TOK
    gate_blk = OFF_GATE // D_MODEL
    hspec = pl.BlockSpec((DIL_HEADS, tm, HEAD_DIM), lambda i: (0, i, 0))
    gspec = lambda which: pl.BlockSpec((tm, D_MODEL), lambda i: (i, gate_blk + which))
    wspec = lambda rows: pl.BlockSpec((None, rows, D_MODEL), lambda i: (layer, 0, 0))
    return pl.pallas_call(
        _merge_kernel,
        grid=(n // tm,),
        in_specs=[
            pl.BlockSpec((tm, A_V), lambda i: (i, 0)),
            hspec, hspec, hspec, hspec, hspec, hspec,
            pl.BlockSpec((tm, C_Q), lambda i: (i, 0)),
            gspec(0), gspec(1), gspec(2),
            wspec(A_V), wspec(B_OUT), wspec(C_Q),
        ],
        out_specs=pl.BlockSpec((tm, D_MODEL), lambda i: (i, 0)),
        out_shape=jax.ShapeDtypeStruct((n, D_MODEL), bf16),
        compiler_params=_cparams(("parallel",)),
    )(o_a, o_bs[0], o_bs[1], o_bs[2], lses[0], lses[1], lses[2], o_c,
      main, main, main, w_up_a, w_up_b, w_up_c)


def _layer_norm_rows(y, g, b):
    mu = jnp.mean(y, axis=-1, keepdims=True)
    d = y - mu
    var = jnp.mean(d * d, axis=-1, keepdims=True)
    return d * lax.rsqrt(var + LN_EPS) * g + b


def _route_rows(logits_t, bias_col):
    scores = jax.nn.sigmoid(logits_t)
    sel = scores + bias_col
    rows = [sel[e:e + 1, :] for e in range(N_EXPERTS)]
    srow = [scores[e:e + 1, :] for e in range(N_EXPERTS)]
    best_val = None
    best_grp = None
    for g in range(N_EXPERT_GROUPS):
        mem = rows[g * EXPERTS_PER_GROUP:(g + 1) * EXPERTS_PER_GROUP]
        top2 = None
        for a in range(EXPERTS_PER_GROUP):
            for b in range(a + 1, EXPERTS_PER_GROUP):
                pair = mem[a] + mem[b]
                top2 = pair if top2 is None else jnp.maximum(top2, pair)
        if g == 0:
            best_val, best_grp = top2, jnp.zeros_like(top2, dtype=jnp.int32)
        else:
            upd = top2 > best_val
            best_val = jnp.where(upd, top2, best_val)
            best_grp = jnp.where(upd, g, best_grp)
    neg = jnp.full_like(best_val, -jnp.inf)
    cand = [jnp.where(best_grp == (e // EXPERTS_PER_GROUP), rows[e], neg) for e in range(N_EXPERTS)]

    def arg_top(vals):
        bv, bi = vals[0], jnp.zeros_like(best_grp)
        for e in range(1, N_EXPERTS):
            upd = vals[e] > bv
            bv = jnp.where(upd, vals[e], bv)
            bi = jnp.where(upd, e, bi)
        return bi

    idx1 = arg_top(cand)
    idx2 = arg_top([jnp.where(idx1 == e, neg, cand[e]) for e in range(N_EXPERTS)])
    zero = jnp.zeros_like(best_val)
    s1 = zero
    s2 = zero
    for e in range(N_EXPERTS):
        s1 = s1 + jnp.where(idx1 == e, srow[e], zero)
        s2 = s2 + jnp.where(idx2 == e, srow[e], zero)
    tot = s1 + s2
    return idx1, idx2, s1 / tot, s2 / tot


def _outproj_kernel(m_ref, w_ref, x_ref, g_ref, b_ref, wr_ref, x1_ref, x1b_ref, lg_ref):
    h = jnp.dot(m_ref[...], w_ref[...], preferred_element_type=f32)
    x1 = _layer_norm_rows(DEEPNORM_ALPHA * x_ref[...] + h, g_ref[...], b_ref[...])
    x1_ref[...] = x1
    x_hi = x1.astype(bf16)
    x1b_ref[...] = x_hi
    x_lo = (x1 - x_hi.astype(f32)).astype(bf16)
    wr = wr_ref[...]
    w_hi = wr.astype(bf16)
    w_lo = (wr - w_hi.astype(f32)).astype(bf16)
    lg_ref[...] = (jnp.dot(x_hi, w_hi, preferred_element_type=f32)
                   + jnp.dot(x_hi, w_lo, preferred_element_type=f32)
                   + jnp.dot(x_lo, w_hi, preferred_element_type=f32))


def _outproj_ln(merged, w_out, x, ln_g, ln_b, w_router_pad, layer):
    n = x.shape[0]
    tm = TM_TOK
    vspec = pl.BlockSpec((None, 1, D_MODEL), lambda i: (layer, 0, 0))
    row = pl.BlockSpec((tm, D_MODEL), lambda i: (i, 0))
    return pl.pallas_call(
        _outproj_kernel,
        grid=(n // tm,),
        in_specs=[
            row,
            pl.BlockSpec((None, D_MODEL, D_MODEL), lambda i: (layer, 0, 0)),
            row,
            vspec, vspec,
            pl.BlockSpec((D_MODEL, LANES), lambda i: (0, 0)),
        ],
        out_specs=[row, row, pl.BlockSpec((tm, LANES), lambda i: (i, 0))],
        out_shape=[
            jax.ShapeDtypeStruct((n, D_MODEL), f32),
            jax.ShapeDtypeStruct((n, D_MODEL), bf16),
            jax.ShapeDtypeStruct((n, LANES), f32),
        ],
        compiler_params=_cparams(("parallel",)),
    )(merged, w_out, x, ln_g, ln_b, w_router_pad)


def _route_kernel(lg_ref, rb_ref, idx_ref, w1_ref, w2_ref):
    logits_t = lg_ref[...].T[0:N_EXPERTS, :]
    i1, i2, w1, w2 = _route_rows(logits_t, rb_ref[...])
    idx_ref[0:1, :] = i1
    idx_ref[1:2, :] = i2
    tr = logits_t.shape[1]
    w1_ref[...] = jnp.broadcast_to(w1, (LANES, tr)).T
    w2_ref[...] = jnp.broadcast_to(w2, (LANES, tr)).T


def _route(logits, router_bias_col):
    n = logits.shape[0]
    tr = TM_ROUTE
    return pl.pallas_call(
        _route_kernel,
        grid=(n // tr,),
        in_specs=[
            pl.BlockSpec((tr, LANES), lambda i: (i, 0)),
            pl.BlockSpec((N_EXPERTS, 1), lambda i: (0, 0)),
        ],
        out_specs=[
            pl.BlockSpec((2, tr), lambda i: (0, i)),
            pl.BlockSpec((tr, LANES), lambda i: (i, 0)),
            pl.BlockSpec((tr, LANES), lambda i: (i, 0)),
        ],
        out_shape=[
            jax.ShapeDtypeStruct((2, n), jnp.int32),
            jax.ShapeDtypeStruct((n, LANES), f32),
            jax.ShapeDtypeStruct((n, LANES), f32),
        ],
        compiler_params=_cparams(("parallel",)),
    )(logits, router_bias_col)


def _moe_kernel(te_ref, nt_ref, *refs, tiles_per_piece):
    del te_ref
    x_refs, (wg_ref, wu_ref, wd_ref, o_ref) = refs[:-4], refs[-4:]
    tile = pl.program_id(0)

    @pl.when(tile < nt_ref[0])
    def _():
        x = x_refs[0][...]
        for p in range(1, len(x_refs)):
            x = jnp.where(tile >= p * tiles_per_piece, x_refs[p][...], x)
        gate = jnp.dot(x, wg_ref[...], preferred_element_type=f32)
        up = jnp.dot(x, wu_ref[...], preferred_element_type=f32)
        h = (gate * jax.nn.sigmoid(gate) * up).astype(bf16)
        o_ref[...] = jnp.dot(h, wd_ref[...], preferred_element_type=f32).astype(o_ref.dtype)

    @pl.when(tile >= nt_ref[0])
    def _():
        o_ref[...] = jnp.zeros_like(o_ref)


def _moe_grouped(x_pieces, tile_expert, n_tiles_used, w_gate, w_up, w_down, layer):
    tpp = x_pieces[0].shape[0] // TM_MOE
    n_tiles = tpp * len(x_pieces)
    wspec_in = pl.BlockSpec((None, None, D_MODEL, D_FF_EXPERT), lambda t, te, nt: (layer, te[t], 0, 0))
    wspec_out = pl.BlockSpec((None, None, D_FF_EXPERT, D_MODEL), lambda t, te, nt: (layer, te[t], 0, 0))

    def piece_spec(p):
        return pl.BlockSpec((TM_MOE, D_MODEL),
                            lambda t, te, nt: (jnp.clip(t - p * tpp, 0, tpp - 1), 0))

    grid_spec = pltpu.PrefetchScalarGridSpec(
        num_scalar_prefetch=2,
        grid=(n_tiles,),
        in_specs=[piece_spec(p) for p in range(len(x_pieces))] + [wspec_in, wspec_in, wspec_out],
        out_specs=pl.BlockSpec((TM_MOE, D_MODEL), lambda t, te, nt: (t, 0)),
    )
    return pl.pallas_call(
        functools.partial(_moe_kernel, tiles_per_piece=tpp),
        grid_spec=grid_spec,
        out_shape=jax.ShapeDtypeStruct((n_tiles * TM_MOE, D_MODEL), bf16),
        compiler_params=_cparams(("arbitrary",)),
    )(tile_expert, n_tiles_used, *x_pieces, w_gate, w_up, w_down)


def _combine_kernel(x_ref, *refs, n_pieces, tiles_per_piece):
    y1_refs, y2_refs = refs[:n_pieces], refs[n_pieces:2 * n_pieces]
    w1_ref, w2_ref, g_ref, b_ref, o_ref, ob_ref = refs[2 * n_pieces:]
    tile = pl.program_id(0)

    def pick(piece_refs):
        y = piece_refs[0][...]
        for p in range(1, n_pieces):
            y = jnp.where(tile >= p * tiles_per_piece, piece_refs[p][...], y)
        return y

    y1 = pick(y1_refs)
    y2 = pick(y2_refs)
    w1 = w1_ref[...]
    w2 = w2_ref[...]
    for c in range(D_MODEL // LANES):
        ls = slice(c * LANES, (c + 1) * LANES)
        o_ref[:, ls] = (DEEPNORM_ALPHA * x_ref[:, ls] + w1 * y1[:, ls].astype(f32)
                        + w2 * y2[:, ls].astype(f32))
    x2 = _layer_norm_rows(o_ref[...], g_ref[...], b_ref[...])
    o_ref[...] = x2
    ob_ref[...] = x2.astype(bf16)


def _combine_ln(x1, y1_pieces, y2_pieces, w1b, w2b, ln_g, ln_b, layer):
    n = x1.shape[0]
    tm = TM_TOK
    n_pieces = len(y1_pieces)
    tpp = y1_pieces[0].shape[0] // tm
    row = pl.BlockSpec((tm, D_MODEL), lambda i: (i, 0))
    wsp = pl.BlockSpec((tm, LANES), lambda i: (i, 0))
    vspec = pl.BlockSpec((None, 1, D_MODEL), lambda i: (layer, 0, 0))

    def piece_spec(p):
        return pl.BlockSpec((tm, D_MODEL), lambda i: (jnp.clip(i - p * tpp, 0, tpp - 1), 0))

    pieces = [piece_spec(p) for p in range(n_pieces)]
    return pl.pallas_call(
        functools.partial(_combine_kernel, n_pieces=n_pieces, tiles_per_piece=tpp),
        grid=(n // tm,),
        in_specs=[row] + pieces + pieces + [wsp, wsp, vspec, vspec],
        out_specs=[row, row],
        out_shape=[jax.ShapeDtypeStruct((n, D_MODEL), f32), jax.ShapeDtypeStruct((n, D_MODEL), bf16)],
        compiler_params=_cparams(("parallel",)),
    )(x1, *y1_pieces, *y2_pieces, w1b, w2b, ln_g, ln_b)


def _dispatch_plan(idx):
    n = idx.shape[1]
    m_pad = 2 * n + N_EXPERTS * TM_MOE
    n_tiles = m_pad // TM_MOE
    e_flat = idx.reshape(-1)
    onehot = (e_flat[:, None] == jnp.arange(N_EXPERTS, dtype=jnp.int32)[None, :]).astype(jnp.int32)
    rank = jnp.sum((jnp.cumsum(onehot, axis=0) - onehot) * onehot, axis=1)
    counts = jnp.sum(onehot, axis=0)
    tiles_per = (counts + TM_MOE - 1) // TM_MOE
    tile_end = jnp.cumsum(tiles_per)
    start_row = (tile_end - tiles_per) * TM_MOE
    dest = jnp.sum(onehot * start_row[None, :], axis=1) + rank
    tok = jnp.tile(jnp.arange(n, dtype=jnp.int32), 2)
    src = jnp.zeros((m_pad,), jnp.int32).at[dest].set(tok, mode="promise_in_bounds", unique_indices=True)
    n_used = tile_end[-1]
    tile_ids = jnp.minimum(jnp.arange(n_tiles, dtype=jnp.int32), n_used - 1)
    tile_expert = jnp.sum((tile_ids[:, None] >= tile_end[None, :]).astype(jnp.int32), axis=1)
    tile_expert = jnp.minimum(tile_expert, N_EXPERTS - 1)
    return src, dest, tile_expert, n_used.reshape(1).astype(jnp.int32)


def _rope_tables(pos, dim):
    inv_freq = ROPE_THETA ** (-jnp.arange(0, dim, 2, dtype=f32) / dim)
    ang = pos.astype(f32)[:, None] * inv_freq[None, :]
    return jnp.cos(ang), jnp.sin(ang)


def _tables(t):
    cos, sin = _rope_tables(jnp.arange(t), HEAD_DIM)
    cos_b = jnp.concatenate([cos, cos], axis=-1)
    sin_b = jnp.concatenate([-sin, sin], axis=-1)
    tok = jnp.arange(t)
    cr, sr = _rope_tables(tok // GRID_W, HEAD_DIM // 2)
    cc, sc = _rope_tables(tok % GRID_W, HEAD_DIM // 2)
    cos_c = jnp.concatenate([cr, cr, cc, cc], axis=-1)
    sin_c = jnp.concatenate([-sr, sr, -sc, sc], axis=-1)
    return cos_b, sin_b, cos_c, sin_c


def _repack_w_in(w_in):
    offs = [0]
    for s in IN_SPLITS:
        offs.append(offs[-1] + s)
    seg = [w_in[:, :, offs[i]:offs[i + 1]] for i in range(len(IN_SPLITS))]
    a_q, a_k, a_v, a_r, a_lr, b_qkv, c_q, c_k, c_v, gate = seg
    lr_pad = jnp.pad(a_lr, ((0, 0), (0, 0), (0, LR_PAD - A_LR)))
    w_main = jnp.concatenate([gate, a_q, a_k, a_v, a_r, c_q, c_k, c_v, lr_pad], axis=-1)
    w_main = jnp.pad(w_main, ((0, 0), (0, 0), (0, MAIN_COLS - MAIN_USED))).astype(bf16)
    return w_main, b_qkv.astype(bf16)


def _mixer(xb, wts, layer, bsz, t, tables):
    cos_b, sin_b, cos_c, sin_c = tables
    main = _proj_main(xb, wts["w_main"], layer)
    qkv_b = _proj_b(xb, wts["w_b"], cos_b, sin_b, layer, t)
    o_a = _gla(main, wts["gla_w2_f"], wts["gla_b_f"], wts["gla_w2_b"], wts["gla_b_b"],
               wts["gla_norm_g"], layer, bsz, t)
    o_bs, lses = [], []
    for group in range(len(DIL_CONFIGS)):
        o_g, lse_g = _dilated(qkv_b, group, bsz, t)
        o_bs.append(o_g)
        lses.append(lse_g)
    qc, kc = _c_prep(main, cos_c, sin_c, wts["q_norm_g"], wts["k_norm_g"], layer, t)
    o_c = _gqa(qc, kc, main, bsz, t)
    return _merge(o_a, o_bs, lses, o_c, main, wts["w_up_a"], wts["w_up_b"], wts["w_up_c"], layer)


def _gather_rows(a, rows):
    return a.at[rows].get(mode="promise_in_bounds")


class _Trunk:
    def __init__(self, x3, wts):
        self.bsz, self.t, _ = x3.shape
        self.wts = wts
        self.x = x3.reshape(self.bsz * self.t, D_MODEL)
        self.xb = self.x.astype(bf16)
        self.tables = _tables(self.t)

    def mix_and_route(self, layer):
        wts = self.wts
        merged = _mixer(self.xb, wts, layer, self.bsz, self.t, self.tables)
        self.x1, x1b, logits = _outproj_ln(merged, wts["w_out"], self.x, wts["ln1_g"], wts["ln1_b"],
                                           wts["w_router_pad"], layer)
        idx, self.w1b, self.w2b = _route(logits, wts["router_bias_col"])
        src, self.dest, self.tile_expert, self.n_used = _dispatch_plan(idx)
        n_pieces = -(-src.shape[0] // MOE_GATHER_ROWS)
        self.x_pieces = [_gather_rows(x1b, piece) for piece in jnp.split(src, n_pieces)]

    def experts(self, layer):
        wts = self.wts
        n = self.x1.shape[0]
        y_sorted = _moe_grouped(self.x_pieces, self.tile_expert, self.n_used, wts["moe_w_gate"],
                                wts["moe_w_up"], wts["moe_w_down"], layer)
        c_pieces = -(-n // COMBINE_GATHER_ROWS)
        self.y1 = [_gather_rows(y_sorted, piece) for piece in jnp.split(self.dest[:n], c_pieces)]
        self.y2 = [_gather_rows(y_sorted, piece) for piece in jnp.split(self.dest[n:], c_pieces)]

    def combine(self, layer):
        self.x, self.xb = _combine_ln(self.x1, self.y1, self.y2, self.w1b, self.w2b,
                                      self.wts["ln2_g"], self.wts["ln2_b"], layer)

    def result(self):
        return self.x.reshape(self.bsz, self.t, D_MODEL)


def _run_trunks(inputs, wts, depth=DEPTH):
    trunks = [_Trunk(x3, wts) for x3 in inputs]
    for layer in range(depth):
        for stage in (_Trunk.mix_and_route, _Trunk.experts, _Trunk.combine):
            for trunk in trunks:
                stage(trunk, layer)
    return tuple(trunk.result() for trunk in trunks)


def _cast_kernel(x_ref, o_ref):
    o_ref[...] = x_ref[...].astype(o_ref.dtype)


def _expert_weights_bf16(w):
    n_layers, n_exp, rows, cols = w.shape
    spec = pl.BlockSpec((None, None, rows, cols), lambda i, j: (i, j, 0, 0))
    return pl.pallas_call(
        _cast_kernel,
        grid=(n_layers, n_exp),
        in_specs=[spec],
        out_specs=spec,
        out_shape=jax.ShapeDtypeStruct(w.shape, bf16),
        compiler_params=_cparams(("parallel", "parallel")),
    )(w)


def _prepare_weights(w_in, gla_w2_f, gla_b_f, gla_w2_b, gla_b_b, gla_norm_g, q_norm_g, k_norm_g,
                     w_up_a, w_up_b, w_up_c, w_out, ln1_g, ln1_b, w_router, router_bias,
                     moe_w_gate, moe_w_up, moe_w_down, ln2_g, ln2_b):
    w_main, w_b = _repack_w_in(w_in)
    row = lambda a: a.reshape(DEPTH, 1, a.shape[-1]).astype(f32)
    return {
        "w_main": w_main,
        "w_b": w_b,
        "gla_w2_f": gla_w2_f.astype(f32),
        "gla_b_f": row(gla_b_f),
        "gla_w2_b": gla_w2_b.astype(f32),
        "gla_b_b": row(gla_b_b),
        "gla_norm_g": row(gla_norm_g),
        "q_norm_g": row(q_norm_g),
        "k_norm_g": row(k_norm_g),
        "w_up_a": w_up_a.astype(bf16),
        "w_up_b": w_up_b.astype(bf16),
        "w_up_c": w_up_c.astype(bf16),
        "w_out": w_out.astype(bf16),
        "ln1_g": row(ln1_g),
        "ln1_b": row(ln1_b),
        "w_router_pad": jnp.pad(w_router.astype(f32), ((0, 0), (0, LANES - N_EXPERTS))),
        "router_bias_col": router_bias.astype(f32).reshape(N_EXPERTS, 1),
        "moe_w_gate": _expert_weights_bf16(moe_w_gate.astype(f32)),
        "moe_w_up": _expert_weights_bf16(moe_w_up.astype(f32)),
        "moe_w_down": _expert_weights_bf16(moe_w_down.astype(f32)),
        "ln2_g": row(ln2_g),
        "ln2_b": row(ln2_b),
    }


def kernel(x_prompt, x_sample, w_in, gla_w2_f, gla_b_f, gla_w2_b, gla_b_b, gla_norm_g, q_norm_g, k_norm_g, w_up_a, w_up_b, w_up_c, w_out, ln1_g, ln1_b, w_router, router_bias, moe_w_gate, moe_w_up, moe_w_down, ln2_g, ln2_b):
    wts = _prepare_weights(w_in, gla_w2_f, gla_b_f, gla_w2_b, gla_b_b, gla_norm_g, q_norm_g, k_norm_g,
                           w_up_a, w_up_b, w_up_c, w_out, ln1_g, ln1_b, w_router, router_bias,
                           moe_w_gate, moe_w_up, moe_w_down, ln2_g, ln2_b)
    return _run_trunks((x_prompt, x_sample), wts)
```

```python
import functools

import jax
import jax.numpy as jnp
from jax import lax
from jax.experimental import pallas as pl
from jax.experimental.pallas import tpu as pltpu

D_MODEL = 2048
DEPTH = 4
HEAD_DIM = 128
GRID_W = 64
ROPE_THETA = 10000.0
LN_EPS = 1e-5
RMS_EPS = 1e-6
GLA_HEADS = 4
GLA_DK = 128
GLA_DV = 256
GLA_RANK = 16
GLA_TAU = 16.0
GLA_CHUNK = 64
DIL_CONFIGS = ((128, 1), (512, 4), (2048, 16))
DIL_HEADS = 4
GQA_Q_HEADS = 8
GQA_KV_HEADS = 2
N_EXPERTS = 16
N_EXPERT_GROUPS = 4
EXPERTS_PER_GROUP = N_EXPERTS // N_EXPERT_GROUPS
D_FF_EXPERT = 1024
DEEPNORM_ALPHA = (2.0 * DEPTH) ** 0.25

A_QK = GLA_HEADS * GLA_DK
A_V = GLA_HEADS * GLA_DV
A_LR = 2 * GLA_RANK
B_HEADS = len(DIL_CONFIGS) * DIL_HEADS
B_QKV = 3 * B_HEADS * HEAD_DIM
B_OUT = DIL_HEADS * HEAD_DIM
C_Q = GQA_Q_HEADS * HEAD_DIM
C_KV = GQA_KV_HEADS * HEAD_DIM
GATE_COLS = 3 * D_MODEL
IN_SPLITS = (A_QK, A_QK, A_V, A_V, A_LR, B_QKV, C_Q, C_KV, C_KV, GATE_COLS)

LANES = 128
VMEM_LIMIT = 56 * 1024 * 1024
LR_PAD = LANES

OFF_GATE = 0
OFF_AQ = OFF_GATE + GATE_COLS
OFF_AK = OFF_AQ + A_QK
OFF_AV = OFF_AK + A_QK
OFF_AR = OFF_AV + A_V
OFF_CQ = OFF_AR + A_V
OFF_CK = OFF_CQ + C_Q
OFF_CV = OFF_CK + C_KV
OFF_LR = OFF_CV + C_KV
MAIN_USED = OFF_LR + LR_PAD

TM_PROJ = 1024
TN_MAIN = 1024
TM_ROUTE = 2048
MAIN_COLS = -(-MAIN_USED // TN_MAIN) * TN_MAIN
TN_B = B_HEADS * HEAD_DIM
TM_TOK = 256
TM_MOE = 512
MOE_GATHER_ROWS = 22528
COMBINE_GATHER_ROWS = 16384
TM_MAIN = 2048
GQA_TQ = 512
GQA_TK = 1024
GQA_GROUP = 2
GQA_STRIP = 32
LOG2_E = 1.4426950408889634
GLA_SUPER = 8
DIL_UNROLL = 8
DIL_HALF = 64
NEG_BIG = -1e30

f32 = jnp.float32
bf16 = jnp.bfloat16


def _cparams(sem):
    return pltpu.CompilerParams(dimension_semantics=sem, vmem_limit_bytes=VMEM_LIMIT)


def _nt_dot(a, b):
    return lax.dot_general(a, b, (((1,), (1,)), ((), ())), preferred_element_type=f32)


def _tn_dot(a, b):
    return lax.dot_general(a, b, (((0,), (0,)), ((), ())), preferred_element_type=f32)


def _axial_partner(x):
    lane = lax.broadcasted_iota(jnp.int32, x.shape, 1)
    quarter = HEAD_DIM // 4
    first = (lane % (2 * quarter)) < quarter
    return jnp.where(first, pltpu.roll(x, HEAD_DIM - quarter, 1), pltpu.roll(x, quarter, 1))


def _mm_kernel(x_ref, w_ref, o_ref):
    o_ref[...] = jnp.dot(x_ref[...], w_ref[...], preferred_element_type=f32).astype(o_ref.dtype)


def _proj_main(xb, w_main, layer):
    n, k = xb.shape
    grid = (MAIN_COLS // TN_MAIN, n // TM_MAIN)
    return pl.pallas_call(
        _mm_kernel,
        grid=grid,
        in_specs=[
            pl.BlockSpec((TM_MAIN, k), lambda j, i: (i, 0)),
            pl.BlockSpec((None, k, TN_MAIN), lambda j, i: (layer, 0, j)),
        ],
        out_specs=pl.BlockSpec((TM_MAIN, TN_MAIN), lambda j, i: (i, j)),
        out_shape=jax.ShapeDtypeStruct((n, MAIN_COLS), bf16),
        compiler_params=_cparams(("parallel", "parallel")),
    )(xb, w_main)


def _proj_b_kernel(x_ref, w_ref, cos_ref, sin_ref, o_ref):
    j = pl.program_id(0)
    pair = 2 * HEAD_DIM

    def emit(rotary, scale):
        x = x_ref[...]
        if rotary:
            cos = cos_ref[...] * scale
            sin = sin_ref[...] * scale
        for c in range(TN_B // pair):
            acc = jnp.dot(x, w_ref[:, c * pair:(c + 1) * pair], preferred_element_type=f32)
            for h in range(2):
                a = acc[:, h * HEAD_DIM:(h + 1) * HEAD_DIM]
                if rotary:
                    a = a * cos + pltpu.roll(a, HEAD_DIM // 2, 1) * sin
                o_ref[2 * c + h] = a.astype(o_ref.dtype)

    @pl.when(j == 0)
    def _():
        emit(True, HEAD_DIM ** -0.5)

    @pl.when(j == 1)
    def _():
        emit(True, 1.0)

    @pl.when(j == 2)
    def _():
        emit(False, 1.0)


def _proj_b(xb, w_b, cos_b, sin_b, layer, t):
    n, k = xb.shape
    tpb = t // TM_PROJ
    grid = (B_QKV // TN_B, n // TM_PROJ)
    hpt = TN_B // HEAD_DIM
    return pl.pallas_call(
        _proj_b_kernel,
        grid=grid,
        in_specs=[
            pl.BlockSpec((TM_PROJ, k), lambda j, i: (i, 0)),
            pl.BlockSpec((None, k, TN_B), lambda j, i: (layer, 0, j)),
            pl.BlockSpec((TM_PROJ, HEAD_DIM), lambda j, i: (i % tpb, 0)),
            pl.BlockSpec((TM_PROJ, HEAD_DIM), lambda j, i: (i % tpb, 0)),
        ],
        out_specs=pl.BlockSpec((hpt, TM_PROJ, HEAD_DIM), lambda j, i: (j, i, 0)),
        out_shape=jax.ShapeDtypeStruct((3 * B_HEADS, n, HEAD_DIM), bf16),
        compiler_params=_cparams(("parallel", "parallel")),
    )(xb, w_b, cos_b, sin_b)


def _gla_kernel(q_ref, k_ref, v_ref, r_ref, lr_ref, w2f_ref, bf_ref, w2b_ref, bb_ref, g_ref,
                o_ref, accf_ref, accb_ref, *, t):
    c_len = GLA_CHUNK
    sup = GLA_SUPER * c_len
    n_sup = t // sup
    row = lax.broadcasted_iota(jnp.int32, (sup, sup), 0)
    col = lax.broadcasted_iota(jnp.int32, (sup, sup), 1)
    same = (row // c_len) == (col // c_len)

    def stage_decay(base, fwd):
        w2 = (w2f_ref if fwd else w2b_ref)[...].astype(bf16)
        bias = (bf_ref if fwd else bb_ref)[...]
        lo = 0 if fwd else GLA_RANK
        lr = lr_ref[pl.ds(base, sup), :][:, lo:lo + GLA_RANK]
        z = jnp.dot(lr, w2, preferred_element_type=f32) + bias
        g = (jnp.minimum(z, 0.0) - jnp.log(1.0 + jnp.exp(-jnp.abs(z)))) * (1.0 / GLA_TAU)
        g_hi = g.astype(bf16)
        g_lo = (g - g_hi.astype(f32)).astype(bf16)
        return g_hi, g_lo

    def stage_cumsum(parts, fwd):
        g_hi, g_lo = parts
        tri = jnp.where(same & ((row >= col) if fwd else (row <= col)), 1.0, 0.0).astype(bf16)
        b = (jnp.dot(tri, g_hi, preferred_element_type=f32)
             + jnp.dot(tri, g_lo, preferred_element_type=f32))
        edge = c_len - 1 if fwd else 0
        tot = jnp.concatenate(
            [jnp.broadcast_to(b[c * c_len + edge:c * c_len + edge + 1, :], (c_len, GLA_DK))
             for c in range(GLA_SUPER)], axis=0)
        return b, tot

    def stage_scale(base, b, tot):
        sl = pl.ds(base, sup)
        q = q_ref[sl, :].astype(f32) * (GLA_DK ** -0.5)
        k = k_ref[sl, :].astype(f32)
        q_t = (q * jnp.exp(b)).astype(bf16)
        k_t = (k * jnp.exp(-b)).astype(bf16)
        k_s = (k * jnp.exp(tot - b)).astype(bf16)
        return q_t, k_t, k_s, jnp.exp(tot)

    def stage_intra(base, q_t, k_t, fwd):
        smask = same & ((col <= row) if fwd else (col > row))
        scores = jnp.where(smask, _nt_dot(q_t, k_t), 0.0)
        return jnp.dot(scores.astype(bf16), v_ref[pl.ds(base, sup), :], preferred_element_type=f32)

    def chunk_step(base, c, fwd, q_t, k_s, dec, o_intra, s_t):
        acc_ref = accf_ref if fwd else accb_ref
        rs = slice(c * c_len, (c + 1) * c_len)
        rows_c = pl.ds(base + c * c_len, c_len)
        acc_ref[rows_c, :] = o_intra[rs, :] + _nt_dot(q_t[rs, :], s_t.astype(bf16))
        return s_t * dec[c * c_len:c * c_len + 1, :] + _tn_dot(v_ref[rows_c, :], k_s[rs, :])

    def body(i, carry):
        s_f, s_b = carry
        base_f = pl.multiple_of(i * sup, sup)
        base_b = pl.multiple_of((n_sup - 1 - i) * sup, sup)
        parts_f = stage_decay(base_f, True)
        parts_b = stage_decay(base_b, False)
        b_f, tot_f = stage_cumsum(parts_f, True)
        b_b, tot_b = stage_cumsum(parts_b, False)
        qt_f, kt_f, ks_f, dec_f = stage_scale(base_f, b_f, tot_f)
        qt_b, kt_b, ks_b, dec_b = stage_scale(base_b, b_b, tot_b)
        oi_f = stage_intra(base_f, qt_f, kt_f, True)
        oi_b = stage_intra(base_b, qt_b, kt_b, False)
        for c in range(GLA_SUPER):
            s_f = chunk_step(base_f, c, True, qt_f, ks_f, dec_f, oi_f, s_f)
            s_b = chunk_step(base_b, GLA_SUPER - 1 - c, False, qt_b, ks_b, dec_b, oi_b, s_b)
        return s_f, s_b

    zero = jnp.zeros((GLA_DV, GLA_DK), f32)
    lax.fori_loop(0, n_sup, body, (zero, zero))

    rows = 256
    gain = g_ref[...]

    def finish(i, carry):
        sl = pl.ds(pl.multiple_of(i * rows, rows), rows)
        x = accf_ref[sl, :] + accb_ref[sl, :]
        ms = jnp.mean(x * x, axis=-1, keepdims=True)
        y = x * lax.rsqrt(ms + RMS_EPS) * gain
        r = r_ref[sl, :].astype(f32)
        o_ref[sl, :] = (y * (r * jax.nn.sigmoid(r))).astype(o_ref.dtype)
        return carry

    lax.fori_loop(0, t // rows, finish, 0)


def _gla(main, w2f, b_f, w2b, b_b, gain, layer, bsz, t):
    n = bsz * t
    qk_blk = lambda off: (lambda b, h: (b, off // GLA_DK + h))
    v_blk = lambda off: (lambda b, h: (b, off // GLA_DV + h))
    wspec = pl.BlockSpec((None, GLA_RANK, GLA_DK), lambda b, h: (layer, 0, h))
    bspec = pl.BlockSpec((None, 1, GLA_DK), lambda b, h: (layer, 0, h))
    return pl.pallas_call(
        functools.partial(_gla_kernel, t=t),
        grid=(bsz, GLA_HEADS),
        in_specs=[
            pl.BlockSpec((t, GLA_DK), qk_blk(OFF_AQ)),
            pl.BlockSpec((t, GLA_DK), qk_blk(OFF_AK)),
            pl.BlockSpec((t, GLA_DV), v_blk(OFF_AV)),
            pl.BlockSpec((t, GLA_DV), v_blk(OFF_AR)),
            pl.BlockSpec((t, LR_PAD), lambda b, h: (b, OFF_LR // LR_PAD)),
            wspec, bspec, wspec, bspec,
            pl.BlockSpec((None, 1, GLA_DV), lambda b, h: (layer, 0, h)),
        ],
        out_specs=pl.BlockSpec((t, GLA_DV), lambda b, h: (b, h)),
        out_shape=jax.ShapeDtypeStruct((n, A_V), bf16),
        scratch_shapes=[pltpu.VMEM((t, GLA_DV), f32), pltpu.VMEM((t, GLA_DV), f32)],
        compiler_params=_cparams(("parallel", "parallel")),
    )(main, main, main, main, main, w2f, b_f, w2b, b_b, gain)


def _dil_kernel(q_ref, k_ref, v_ref, o_ref, lse_ref, *scratch, t, dil):
    n_sub = t // dil
    bq = min(128, n_sub)
    win = min(bq + 2 * DIL_HALF, n_sub)
    n_blk = n_sub // bq
    qi = lax.broadcasted_iota(jnp.int32, (bq, win), 0)
    ki = lax.broadcasted_iota(jnp.int32, (bq, win), 1)
    cvt = 512

    if dil > 1:
        q32, k32, v32, o32 = scratch

        def widen(i, carry):
            sl = pl.ds(pl.multiple_of(i * cvt, cvt), cvt)
            q32[sl, :] = q_ref[0, sl, :].astype(f32)
            k32[sl, :] = k_ref[0, sl, :].astype(f32)
            v32[sl, :] = v_ref[0, sl, :].astype(f32)
            return carry

        lax.fori_loop(0, t // cvt, widen, 0)

    def body(idx, carry):
        r = idx // n_blk
        i = idx % n_blk
        q0 = i * bq
        k0 = jnp.clip(q0 - DIL_HALF, 0, n_sub - win)
        if dil == 1:
            qs = pl.ds(pl.multiple_of(q0, bq), bq)
            ks = pl.ds(pl.multiple_of(k0, DIL_HALF), win)
            q = q_ref[0, qs, :]
            k = k_ref[0, ks, :]
            v = v_ref[0, ks, :]
        else:
            qs = pl.ds(r + q0 * dil, bq, stride=dil)
            ks = pl.ds(r + k0 * dil, win, stride=dil)
            q = q32[qs, :].astype(bf16)
            k = k32[ks, :].astype(bf16)
            v = v32[ks, :].astype(bf16)
        s = _nt_dot(q, k)
        valid = jnp.abs((q0 + qi) - (k0 + ki)) <= DIL_HALF
        s = jnp.where(valid, s, NEG_BIG)
        m = jnp.max(s, axis=-1, keepdims=True)
        p = jnp.exp(s - m)
        l = jnp.sum(p, axis=-1, keepdims=True)
        o = jnp.dot(p.astype(bf16), v, preferred_element_type=f32) / l
        lse = jnp.broadcast_to(m + jnp.log(l), (bq, HEAD_DIM))
        if dil == 1:
            o_ref[0, qs, :] = o.astype(o_ref.dtype)
            lse_ref[0, qs, :] = lse
        else:
            o32[qs, :] = o
            lse_ref[0, qs, :] = lse
        return carry

    def body_group(j, carry):
        for u in range(DIL_UNROLL):
            body(j * DIL_UNROLL + u, carry)
        return carry

    lax.fori_loop(0, dil * n_blk // DIL_UNROLL, body_group, 0)

    if dil > 1:
        def narrow(i, carry):
            sl = pl.ds(pl.multiple_of(i * cvt, cvt), cvt)
            o_ref[0, sl, :] = o32[sl, :].astype(o_ref.dtype)
            return carry

        lax.fori_loop(0, t // cvt, narrow, 0)


def _dilated(qkv_b, group, bsz, t):
    _, dil = DIL_CONFIGS[group]
    n = bsz * t
    blk = (1, t, HEAD_DIM)
    head0 = group * DIL_HEADS
    scratch = [pltpu.VMEM((t, HEAD_DIM), f32)] * 4 if dil > 1 else []
    return pl.pallas_call(
        functools.partial(_dil_kernel, t=t, dil=dil),
        grid=(bsz, DIL_HEADS),
        in_specs=[
            pl.BlockSpec(blk, lambda b, h: (head0 + h, b, 0)),
            pl.BlockSpec(blk, lambda b, h: (B_HEADS + head0 + h, b, 0)),
            pl.BlockSpec(blk, lambda b, h: (2 * B_HEADS + head0 + h, b, 0)),
        ],
        out_specs=[
            pl.BlockSpec(blk, lambda b, h: (h, b, 0)),
            pl.BlockSpec(blk, lambda b, h: (h, b, 0)),
        ],
        out_shape=[
            jax.ShapeDtypeStruct((DIL_HEADS, n, HEAD_DIM), bf16),
            jax.ShapeDtypeStruct((DIL_HEADS, n, HEAD_DIM), f32),
        ],
        scratch_shapes=scratch,
        compiler_params=_cparams(("parallel", "parallel")),
    )(qkv_b, qkv_b, qkv_b)


def _c_prep_kernel(q_ref, k_ref, cos_ref, sin_ref, qg_ref, kg_ref, qo_ref, ko_ref):
    cos = cos_ref[...]
    sin = sin_ref[...]

    def prep(x, gain, scale):
        x = x.astype(f32)
        ms = jnp.mean(x * x, axis=-1, keepdims=True)
        y = x * lax.rsqrt(ms + RMS_EPS) * gain
        return (y * cos + _axial_partner(y) * sin) * scale

    qg = qg_ref[...]
    kg = kg_ref[...]
    for h in range(GQA_Q_HEADS):
        ls = slice(h * HEAD_DIM, (h + 1) * HEAD_DIM)
        qo_ref[:, ls] = prep(q_ref[:, ls], qg, HEAD_DIM ** -0.5 * LOG2_E).astype(qo_ref.dtype)
    for h in range(GQA_KV_HEADS):
        ls = slice(h * HEAD_DIM, (h + 1) * HEAD_DIM)
        ko_ref[:, ls] = prep(k_ref[:, ls], kg, 1.0).astype(ko_ref.dtype)


def _c_prep(main, cos_c, sin_c, q_gain, k_gain, layer, t):
    n = main.shape[0]
    tm = TM_PROJ
    tpb = t // tm
    gspec = pl.BlockSpec((None, 1, HEAD_DIM), lambda i: (layer, 0, 0))
    return pl.pallas_call(
        _c_prep_kernel,
        grid=(n // tm,),
        in_specs=[
            pl.BlockSpec((tm, C_Q), lambda i: (i, OFF_CQ // C_Q)),
            pl.BlockSpec((tm, C_KV), lambda i: (i, OFF_CK // C_KV)),
            pl.BlockSpec((tm, HEAD_DIM), lambda i: (i % tpb, 0)),
            pl.BlockSpec((tm, HEAD_DIM), lambda i: (i % tpb, 0)),
            gspec, gspec,
        ],
        out_specs=[
            pl.BlockSpec((tm, C_Q), lambda i: (i, 0)),
            pl.BlockSpec((tm, C_KV), lambda i: (i, 0)),
        ],
        out_shape=[
            jax.ShapeDtypeStruct((n, C_Q), bf16),
            jax.ShapeDtypeStruct((n, C_KV), bf16),
        ],
        compiler_params=_cparams(("parallel",)),
    )(main, main, cos_c, sin_c, q_gain, k_gain)


def _gqa_kernel(q_ref, k_ref, v_ref, o_ref, s_scr, p_scr, m_scr, l_scr, acc_scr, *, t, tq, tk):
    grp = GQA_Q_HEADS // GQA_KV_HEADS
    q4 = q_ref[...]
    q = jnp.concatenate([q4[:, h * HEAD_DIM:(h + 1) * HEAD_DIM] for h in range(grp)], axis=0)
    rows = grp * tq
    n_strips = rows // GQA_STRIP
    m_scr[...] = jnp.full((rows, LANES), NEG_BIG, f32)
    l_scr[...] = jnp.zeros((rows, LANES), f32)
    acc_scr[...] = jnp.zeros((rows, HEAD_DIM), f32)

    def widen(col):
        return jnp.broadcast_to(col, (col.shape[0], LANES))

    def softmax_chunk(buf):
        mx = []
        for i in range(n_strips):
            rs = slice(i * GQA_STRIP, (i + 1) * GQA_STRIP)
            mx.append(widen(jnp.max(s_scr[buf, rs, :], axis=-1, keepdims=True)))
        m_old = m_scr[...]
        m_new = jnp.maximum(m_old, jnp.concatenate(mx, axis=0))
        a = jnp.exp2(m_old - m_new)
        m_scr[...] = m_new
        sums = []
        for i in range(n_strips):
            rs = slice(i * GQA_STRIP, (i + 1) * GQA_STRIP)
            m_wide = jnp.concatenate([m_new[rs, :]] * (tk // LANES), axis=1)
            p = jnp.exp2(s_scr[buf, rs, :] - m_wide)
            sums.append(widen(jnp.sum(p, axis=-1, keepdims=True)))
            p_scr[buf, rs, :] = p.astype(bf16)
        l_scr[...] = a * l_scr[...] + jnp.concatenate(sums, axis=0)
        return a

    def chunk_group(cg, carry):
        sls = []
        for u in range(GQA_GROUP):
            sl = pl.ds(pl.multiple_of((cg * GQA_GROUP + u) * tk, tk), tk)
            sls.append(sl)
            s_scr[u] = _nt_dot(q, k_ref[sl, :])
        for u in range(GQA_GROUP):
            a = softmax_chunk(u)
            acc_scr[...] = acc_scr[...] * a + jnp.dot(p_scr[u], v_ref[sls[u], :],
                                                      preferred_element_type=f32)
        return carry

    lax.fori_loop(0, t // (tk * GQA_GROUP), chunk_group, 0)
    o = (acc_scr[...] / l_scr[...]).astype(o_ref.dtype)
    o_ref[...] = jnp.concatenate([o[h * tq:(h + 1) * tq, :] for h in range(grp)], axis=1)


def _gqa(qc, kc, main, bsz, t):
    n = bsz * t
    tq = GQA_TQ
    tk = min(t, GQA_TK)
    grp = GQA_Q_HEADS // GQA_KV_HEADS
    grp_cols = grp * HEAD_DIM
    rows = grp * tq
    nq = t // tq
    return pl.pallas_call(
        functools.partial(_gqa_kernel, t=t, tq=tq, tk=tk),
        grid=(bsz, GQA_KV_HEADS, nq),
        in_specs=[
            pl.BlockSpec((tq, grp_cols), lambda b, j, i: (b * nq + i, j)),
            pl.BlockSpec((t, HEAD_DIM), lambda b, j, i: (b, j)),
            pl.BlockSpec((t, HEAD_DIM), lambda b, j, i: (b, OFF_CV // HEAD_DIM + j)),
        ],
        out_specs=pl.BlockSpec((tq, grp_cols), lambda b, j, i: (b * nq + i, j)),
        out_shape=jax.ShapeDtypeStruct((n, C_Q), bf16),
        scratch_shapes=[
            pltpu.VMEM((GQA_GROUP, rows, tk), f32),
            pltpu.VMEM((GQA_GROUP, rows, tk), bf16),
            pltpu.VMEM((rows, LANES), f32),
            pltpu.VMEM((rows, LANES), f32),
            pltpu.VMEM((rows, HEAD_DIM), f32),
        ],
        compiler_params=_cparams(("parallel", "parallel", "parallel")),
    )(qc, kc, main)


def _merge_kernel(oa_ref, ob0_ref, ob1_ref, ob2_ref, l0_ref, l1_ref, l2_ref, oc_ref,
                  ga_ref, gb_ref, gc_ref, wa_ref, wb_ref, wc_ref, o_ref):
    heads = []
    for h in range(DIL_HEADS):
        l0, l1, l2 = l0_ref[h], l1_ref[h], l2_ref[h]
        m = jnp.maximum(jnp.maximum(l0, l1), l2)
        e0, e1, e2 = jnp.exp(l0 - m), jnp.exp(l1 - m), jnp.exp(l2 - m)
        mix = (e0 * ob0_ref[h].astype(f32) + e1 * ob1_ref[h].astype(f32)
               + e2 * ob2_ref[h].astype(f32)) / (e0 + e1 + e2)
        heads.append(mix.astype(bf16))
    o_b = jnp.concatenate(heads, axis=1)
    up_a = jnp.dot(oa_ref[...], wa_ref[...], preferred_element_type=f32)
    up_b = jnp.dot(o_b, wb_ref[...], preferred_element_type=f32)
    up_c = jnp.dot(oc_ref[...], wc_ref[...], preferred_element_type=f32)
    merged = (jax.nn.sigmoid(ga_ref[...].astype(f32)) * up_a
              + jax.nn.sigmoid(gb_ref[...].astype(f32)) * up_b
              + jax.nn.sigmoid(gc_ref[...].astype(f32)) * up_c)
    o_ref[...] = merged.astype(o_ref.dtype)


def _merge(o_a, o_bs, lses, o_c, main, w_up_a, w_up_b, w_up_c, layer):
    n = o_a.shape[0]
    tm = TM_TOK
    gate_blk = OFF_GATE // D_MODEL
    hspec = pl.BlockSpec((DIL_HEADS, tm, HEAD_DIM), lambda i: (0, i, 0))
    gspec = lambda which: pl.BlockSpec((tm, D_MODEL), lambda i: (i, gate_blk + which))
    wspec = lambda rows: pl.BlockSpec((None, rows, D_MODEL), lambda i: (layer, 0, 0))
    return pl.pallas_call(
        _merge_kernel,
        grid=(n // tm,),
        in_specs=[
            pl.BlockSpec((tm, A_V), lambda i: (i, 0)),
            hspec, hspec, hspec, hspec, hspec, hspec,
            pl.BlockSpec((tm, C_Q), lambda i: (i, 0)),
            gspec(0), gspec(1), gspec(2),
            wspec(A_V), wspec(B_OUT), wspec(C_Q),
        ],
        out_specs=pl.BlockSpec((tm, D_MODEL), lambda i: (i, 0)),
        out_shape=jax.ShapeDtypeStruct((n, D_MODEL), bf16),
        compiler_params=_cparams(("parallel",)),
    )(o_a, o_bs[0], o_bs[1], o_bs[2], lses[0], lses[1], lses[2], o_c,
      main, main, main, w_up_a, w_up_b, w_up_c)


def _layer_norm_rows(y, g, b):
    mu = jnp.mean(y, axis=-1, keepdims=True)
    d = y - mu
    var = jnp.mean(d * d, axis=-1, keepdims=True)
    return d * lax.rsqrt(var + LN_EPS) * g + b


def _route_rows(logits_t, bias_col):
    scores = jax.nn.sigmoid(logits_t)
    sel = scores + bias_col
    rows = [sel[e:e + 1, :] for e in range(N_EXPERTS)]
    srow = [scores[e:e + 1, :] for e in range(N_EXPERTS)]
    best_val = None
    best_grp = None
    for g in range(N_EXPERT_GROUPS):
        mem = rows[g * EXPERTS_PER_GROUP:(g + 1) * EXPERTS_PER_GROUP]
        top2 = None
        for a in range(EXPERTS_PER_GROUP):
            for b in range(a + 1, EXPERTS_PER_GROUP):
                pair = mem[a] + mem[b]
                top2 = pair if top2 is None else jnp.maximum(top2, pair)
        if g == 0:
            best_val, best_grp = top2, jnp.zeros_like(top2, dtype=jnp.int32)
        else:
            upd = top2 > best_val
            best_val = jnp.where(upd, top2, best_val)
            best_grp = jnp.where(upd, g, best_grp)
    neg = jnp.full_like(best_val, -jnp.inf)
    cand = [jnp.where(best_grp == (e // EXPERTS_PER_GROUP), rows[e], neg) for e in range(N_EXPERTS)]

    def arg_top(vals):
        bv, bi = vals[0], jnp.zeros_like(best_grp)
        for e in range(1, N_EXPERTS):
            upd = vals[e] > bv
            bv = jnp.where(upd, vals[e], bv)
            bi = jnp.where(upd, e, bi)
        return bi

    idx1 = arg_top(cand)
    idx2 = arg_top([jnp.where(idx1 == e, neg, cand[e]) for e in range(N_EXPERTS)])
    zero = jnp.zeros_like(best_val)
    s1 = zero
    s2 = zero
    for e in range(N_EXPERTS):
        s1 = s1 + jnp.where(idx1 == e, srow[e], zero)
        s2 = s2 + jnp.where(idx2 == e, srow[e], zero)
    tot = s1 + s2
    return idx1, idx2, s1 / tot, s2 / tot


def _outproj_kernel(m_ref, w_ref, x_ref, g_ref, b_ref, wr_ref, x1_ref, x1b_ref, lg_ref):
    h = jnp.dot(m_ref[...], w_ref[...], preferred_element_type=f32)
    x1 = _layer_norm_rows(DEEPNORM_ALPHA * x_ref[...] + h, g_ref[...], b_ref[...])
    x1_ref[...] = x1
    x_hi = x1.astype(bf16)
    x1b_ref[...] = x_hi
    x_lo = (x1 - x_hi.astype(f32)).astype(bf16)
    wr = wr_ref[...]
    w_hi = wr.astype(bf16)
    w_lo = (wr - w_hi.astype(f32)).astype(bf16)
    lg_ref[...] = (jnp.dot(x_hi, w_hi, preferred_element_type=f32)
                   + jnp.dot(x_hi, w_lo, preferred_element_type=f32)
                   + jnp.dot(x_lo, w_hi, preferred_element_type=f32))


def _outproj_ln(merged, w_out, x, ln_g, ln_b, w_router_pad, layer):
    n = x.shape[0]
    tm = TM_TOK
    vspec = pl.BlockSpec((None, 1, D_MODEL), lambda i: (layer, 0, 0))
    row = pl.BlockSpec((tm, D_MODEL), lambda i: (i, 0))
    return pl.pallas_call(
        _outproj_kernel,
        grid=(n // tm,),
        in_specs=[
            row,
            pl.BlockSpec((None, D_MODEL, D_MODEL), lambda i: (layer, 0, 0)),
            row,
            vspec, vspec,
            pl.BlockSpec((D_MODEL, LANES), lambda i: (0, 0)),
        ],
        out_specs=[row, row, pl.BlockSpec((tm, LANES), lambda i: (i, 0))],
        out_shape=[
            jax.ShapeDtypeStruct((n, D_MODEL), f32),
            jax.ShapeDtypeStruct((n, D_MODEL), bf16),
            jax.ShapeDtypeStruct((n, LANES), f32),
        ],
        compiler_params=_cparams(("parallel",)),
    )(merged, w_out, x, ln_g, ln_b, w_router_pad)


def _route_kernel(lg_ref, rb_ref, idx_ref, w1_ref, w2_ref):
    logits_t = lg_ref[...].T[0:N_EXPERTS, :]
    i1, i2, w1, w2 = _route_rows(logits_t, rb_ref[...])
    idx_ref[0:1, :] = i1
    idx_ref[1:2, :] = i2
    tr = logits_t.shape[1]
    w1_ref[...] = jnp.broadcast_to(w1, (LANES, tr)).T
    w2_ref[...] = jnp.broadcast_to(w2, (LANES, tr)).T


def _route(logits, router_bias_col):
    n = logits.shape[0]
    tr = TM_ROUTE
    return pl.pallas_call(
        _route_kernel,
        grid=(n // tr,),
        in_specs=[
            pl.BlockSpec((tr, LANES), lambda i: (i, 0)),
            pl.BlockSpec((N_EXPERTS, 1), lambda i: (0, 0)),
        ],
        out_specs=[
            pl.BlockSpec((2, tr), lambda i: (0, i)),
            pl.BlockSpec((tr, LANES), lambda i: (i, 0)),
            pl.BlockSpec((tr, LANES), lambda i: (i, 0)),
        ],
        out_shape=[
            jax.ShapeDtypeStruct((2, n), jnp.int32),
            jax.ShapeDtypeStruct((n, LANES), f32),
            jax.ShapeDtypeStruct((n, LANES), f32),
        ],
        compiler_params=_cparams(("parallel",)),
    )(logits, router_bias_col)


def _moe_kernel(te_ref, nt_ref, *refs, tiles_per_piece):
    del te_ref
    x_refs, (wg_ref, wu_ref, wd_ref, o_ref) = refs[:-4], refs[-4:]
    tile = pl.program_id(0)

    @pl.when(tile < nt_ref[0])
    def _():
        x = x_refs[0][...]
        for p in range(1, len(x_refs)):
            x = jnp.where(tile >= p * tiles_per_piece, x_refs[p][...], x)
        gate = jnp.dot(x, wg_ref[...], preferred_element_type=f32)
        up = jnp.dot(x, wu_ref[...], preferred_element_type=f32)
        h = (gate * jax.nn.sigmoid(gate) * up).astype(bf16)
        o_ref[...] = jnp.dot(h, wd_ref[...], preferred_element_type=f32).astype(o_ref.dtype)

    @pl.when(tile >= nt_ref[0])
    def _():
        o_ref[...] = jnp.zeros_like(o_ref)


def _moe_grouped(x_pieces, tile_expert, n_tiles_used, w_gate, w_up, w_down, layer):
    tpp = x_pieces[0].shape[0] // TM_MOE
    n_tiles = tpp * len(x_pieces)
    wspec_in = pl.BlockSpec((None, None, D_MODEL, D_FF_EXPERT), lambda t, te, nt: (layer, te[t], 0, 0))
    wspec_out = pl.BlockSpec((None, None, D_FF_EXPERT, D_MODEL), lambda t, te, nt: (layer, te[t], 0, 0))

    def piece_spec(p):
        return pl.BlockSpec((TM_MOE, D_MODEL),
                            lambda t, te, nt: (jnp.clip(t - p * tpp, 0, tpp - 1), 0))

    grid_spec = pltpu.PrefetchScalarGridSpec(
        num_scalar_prefetch=2,
        grid=(n_tiles,),
        in_specs=[piece_spec(p) for p in range(len(x_pieces))] + [wspec_in, wspec_in, wspec_out],
        out_specs=pl.BlockSpec((TM_MOE, D_MODEL), lambda t, te, nt: (t, 0)),
    )
    return pl.pallas_call(
        functools.partial(_moe_kernel, tiles_per_piece=tpp),
        grid_spec=grid_spec,
        out_shape=jax.ShapeDtypeStruct((n_tiles * TM_MOE, D_MODEL), bf16),
        compiler_params=_cparams(("arbitrary",)),
    )(tile_expert, n_tiles_used, *x_pieces, w_gate, w_up, w_down)


def _combine_kernel(x_ref, *refs, n_pieces, tiles_per_piece):
    y1_refs, y2_refs = refs[:n_pieces], refs[n_pieces:2 * n_pieces]
    w1_ref, w2_ref, g_ref, b_ref, o_ref, ob_ref = refs[2 * n_pieces:]
    tile = pl.program_id(0)

    def pick(piece_refs):
        y = piece_refs[0][...]
        for p in range(1, n_pieces):
            y = jnp.where(tile >= p * tiles_per_piece, piece_refs[p][...], y)
        return y

    y1 = pick(y1_refs)
    y2 = pick(y2_refs)
    w1 = w1_ref[...]
    w2 = w2_ref[...]
    for c in range(D_MODEL // LANES):
        ls = slice(c * LANES, (c + 1) * LANES)
        o_ref[:, ls] = (DEEPNORM_ALPHA * x_ref[:, ls] + w1 * y1[:, ls].astype(f32)
                        + w2 * y2[:, ls].astype(f32))
    x2 = _layer_norm_rows(o_ref[...], g_ref[...], b_ref[...])
    o_ref[...] = x2
    ob_ref[...] = x2.astype(bf16)


def _combine_ln(x1, y1_pieces, y2_pieces, w1b, w2b, ln_g, ln_b, layer):
    n = x1.shape[0]
    tm = TM_TOK
    n_pieces = len(y1_pieces)
    tpp = y1_pieces[0].shape[0] // tm
    row = pl.BlockSpec((tm, D_MODEL), lambda i: (i, 0))
    wsp = pl.BlockSpec((tm, LANES), lambda i: (i, 0))
    vspec = pl.BlockSpec((None, 1, D_MODEL), lambda i: (layer, 0, 0))

    def piece_spec(p):
        return pl.BlockSpec((tm, D_MODEL), lambda i: (jnp.clip(i - p * tpp, 0, tpp - 1), 0))

    pieces = [piece_spec(p) for p in range(n_pieces)]
    return pl.pallas_call(
        functools.partial(_combine_kernel, n_pieces=n_pieces, tiles_per_piece=tpp),
        grid=(n // tm,),
        in_specs=[row] + pieces + pieces + [wsp, wsp, vspec, vspec],
        out_specs=[row, row],
        out_shape=[jax.ShapeDtypeStruct((n, D_MODEL), f32), jax.ShapeDtypeStruct((n, D_MODEL), bf16)],
        compiler_params=_cparams(("parallel",)),
    )(x1, *y1_pieces, *y2_pieces, w1b, w2b, ln_g, ln_b)


def _dispatch_plan(idx):
    n = idx.shape[1]
    m_pad = 2 * n + N_EXPERTS * TM_MOE
    n_tiles = m_pad // TM_MOE
    e_flat = idx.reshape(-1)
    onehot = (e_flat[:, None] == jnp.arange(N_EXPERTS, dtype=jnp.int32)[None, :]).astype(jnp.int32)
    rank = jnp.sum((jnp.cumsum(onehot, axis=0) - onehot) * onehot, axis=1)
    counts = jnp.sum(onehot, axis=0)
    tiles_per = (counts + TM_MOE - 1) // TM_MOE
    tile_end = jnp.cumsum(tiles_per)
    start_row = (tile_end - tiles_per) * TM_MOE
    dest = jnp.sum(onehot * start_row[None, :], axis=1) + rank
    tok = jnp.tile(jnp.arange(n, dtype=jnp.int32), 2)
    src = jnp.zeros((m_pad,), jnp.int32).at[dest].set(tok, mode="promise_in_bounds", unique_indices=True)
    n_used = tile_end[-1]
    tile_ids = jnp.minimum(jnp.arange(n_tiles, dtype=jnp.int32), n_used - 1)
    tile_expert = jnp.sum((tile_ids[:, None] >= tile_end[None, :]).astype(jnp.int32), axis=1)
    tile_expert = jnp.minimum(tile_expert, N_EXPERTS - 1)
    return src, dest, tile_expert, n_used.reshape(1).astype(jnp.int32)


def _rope_tables(pos, dim):
    inv_freq = ROPE_THETA ** (-jnp.arange(0, dim, 2, dtype=f32) / dim)
    ang = pos.astype(f32)[:, None] * inv_freq[None, :]
    return jnp.cos(ang), jnp.sin(ang)


def _tables(t):
    cos, sin = _rope_tables(jnp.arange(t), HEAD_DIM)
    cos_b = jnp.concatenate([cos, cos], axis=-1)
    sin_b = jnp.concatenate([-sin, sin], axis=-1)
    tok = jnp.arange(t)
    cr, sr = _rope_tables(tok // GRID_W, HEAD_DIM // 2)
    cc, sc = _rope_tables(tok % GRID_W, HEAD_DIM // 2)
    cos_c = jnp.concatenate([cr, cr, cc, cc], axis=-1)
    sin_c = jnp.concatenate([-sr, sr, -sc, sc], axis=-1)
    return cos_b, sin_b, cos_c, sin_c


def _repack_w_in(w_in):
    offs = [0]
    for s in IN_SPLITS:
        offs.append(offs[-1] + s)
    seg = [w_in[:, :, offs[i]:offs[i + 1]] for i in range(len(IN_SPLITS))]
    a_q, a_k, a_v, a_r, a_lr, b_qkv, c_q, c_k, c_v, gate = seg
    lr_pad = jnp.pad(a_lr, ((0, 0), (0, 0), (0, LR_PAD - A_LR)))
    w_main = jnp.concatenate([gate, a_q, a_k, a_v, a_r, c_q, c_k, c_v, lr_pad], axis=-1)
    w_main = jnp.pad(w_main, ((0, 0), (0, 0), (0, MAIN_COLS - MAIN_USED))).astype(bf16)
    return w_main, b_qkv.astype(bf16)


def _mixer(xb, wts, layer, bsz, t, tables):
    cos_b, sin_b, cos_c, sin_c = tables
    main = _proj_main(xb, wts["w_main"], layer)
    qkv_b = _proj_b(xb, wts["w_b"], cos_b, sin_b, layer, t)
    o_a = _gla(main, wts["gla_w2_f"], wts["gla_b_f"], wts["gla_w2_b"], wts["gla_b_b"],
               wts["gla_norm_g"], layer, bsz, t)
    o_bs, lses = [], []
    for group in range(len(DIL_CONFIGS)):
        o_g, lse_g = _dilated(qkv_b, group, bsz, t)
        o_bs.append(o_g)
        lses.append(lse_g)
    qc, kc = _c_prep(main, cos_c, sin_c, wts["q_norm_g"], wts["k_norm_g"], layer, t)
    o_c = _gqa(qc, kc, main, bsz, t)
    return _merge(o_a, o_bs, lses, o_c, main, wts["w_up_a"], wts["w_up_b"], wts["w_up_c"], layer)


def _gather_rows(a, rows):
    return a.at[rows].get(mode="promise_in_bounds")


class _Trunk:
    def __init__(self, x3, wts):
        self.bsz, self.t, _ = x3.shape
        self.wts = wts
        self.x = x3.reshape(self.bsz * self.t, D_MODEL)
        self.xb = self.x.astype(bf16)
        self.tables = _tables(self.t)

    def mix_and_route(self, layer):
        wts = self.wts
        merged = _mixer(self.xb, wts, layer, self.bsz, self.t, self.tables)
        self.x1, x1b, logits = _outproj_ln(merged, wts["w_out"], self.x, wts["ln1_g"], wts["ln1_b"],
                                           wts["w_router_pad"], layer)
        idx, self.w1b, self.w2b = _route(logits, wts["router_bias_col"])
        src, self.dest, self.tile_expert, self.n_used = _dispatch_plan(idx)
        n_tiles = src.shape[0] // TM_MOE
        n_pieces = next(d for d in range(1, n_tiles + 1)
                        if n_tiles % d == 0 and src.shape[0] // d <= MOE_GATHER_ROWS)
        self.x_pieces = [_gather_rows(x1b, piece) for piece in jnp.split(src, n_pieces)]

    def experts(self, layer):
        wts = self.wts
        n = self.x1.shape[0]
        y_sorted = _moe_grouped(self.x_pieces, self.tile_expert, self.n_used, wts["moe_w_gate"],
                                wts["moe_w_up"], wts["moe_w_down"], layer)
        c_pieces = -(-n // COMBINE_GATHER_ROWS)
        self.y1 = [_gather_rows(y_sorted, piece) for piece in jnp.split(self.dest[:n], c_pieces)]
        self.y2 = [_gather_rows(y_sorted, piece) for piece in jnp.split(self.dest[n:], c_pieces)]

    def combine(self, layer):
        self.x, self.xb = _combine_ln(self.x1, self.y1, self.y2, self.w1b, self.w2b,
                                      self.wts["ln2_g"], self.wts["ln2_b"], layer)

    def result(self):
        return self.x.reshape(self.bsz, self.t, D_MODEL)


def _run_trunks(inputs, wts, depth=DEPTH):
    trunks = [_Trunk(x3, wts) for x3 in inputs]
    for layer in range(depth):
        for stage in (_Trunk.mix_and_route, _Trunk.experts, _Trunk.combine):
            for trunk in trunks:
                stage(trunk, layer)
    return tuple(trunk.result() for trunk in trunks)


def _cast_kernel(x_ref, o_ref):
    o_ref[...] = x_ref[...].astype(o_ref.dtype)


def _expert_weights_bf16(w):
    n_layers, n_exp, rows, cols = w.shape
    spec = pl.BlockSpec((None, None, rows, cols), lambda i, j: (i, j, 0, 0))
    return pl.pallas_call(
        _cast_kernel,
        grid=(n_layers, n_exp),
        in_specs=[spec],
        out_specs=spec,
        out_shape=jax.ShapeDtypeStruct(w.shape, bf16),
        compiler_params=_cparams(("parallel", "parallel")),
    )(w)


def _prepare_weights(w_in, gla_w2_f, gla_b_f, gla_w2_b, gla_b_b, gla_norm_g, q_norm_g, k_norm_g,
                     w_up_a, w_up_b, w_up_c, w_out, ln1_g, ln1_b, w_router, router_bias,
                     moe_w_gate, moe_w_up, moe_w_down, ln2_g, ln2_b):
    w_main, w_b = _repack_w_in(w_in)
    row = lambda a: a.reshape(DEPTH, 1, a.shape[-1]).astype(f32)
    return {
        "w_main": w_main,
        "w_b": w_b,
        "gla_w2_f": gla_w2_f.astype(f32),
        "gla_b_f": row(gla_b_f),
        "gla_w2_b": gla_w2_b.astype(f32),
        "gla_b_b": row(gla_b_b),
        "gla_norm_g": row(gla_norm_g),
        "q_norm_g": row(q_norm_g),
        "k_norm_g": row(k_norm_g),
        "w_up_a": w_up_a.astype(bf16),
        "w_up_b": w_up_b.astype(bf16),
        "w_up_c": w_up_c.astype(bf16),
        "w_out": w_out.astype(bf16),
        "ln1_g": row(ln1_g),
        "ln1_b": row(ln1_b),
        "w_router_pad": jnp.pad(w_router.astype(f32), ((0, 0), (0, LANES - N_EXPERTS))),
        "router_bias_col": router_bias.astype(f32).reshape(N_EXPERTS, 1),
        "moe_w_gate": _expert_weights_bf16(moe_w_gate.astype(f32)),
        "moe_w_up": _expert_weights_bf16(moe_w_up.astype(f32)),
        "moe_w_down": _expert_weights_bf16(moe_w_down.astype(f32)),
        "ln2_g": row(ln2_g),
        "ln2_b": row(ln2_b),
    }


def kernel(x_prompt, x_sample, w_in, gla_w2_f, gla_b_f, gla_w2_b, gla_b_b, gla_norm_g, q_norm_g, k_norm_g, w_up_a, w_up_b, w_up_c, w_out, ln1_g, ln1_b, w_router, router_bias, moe_w_gate, moe_w_up, moe_w_down, ln2_g, ln2_b):
    wts = _prepare_weights(w_in, gla_w2_f, gla_b_f, gla_w2_b, gla_b_b, gla_norm_g, q_norm_g, k_norm_g,
                           w_up_a, w_up_b, w_up_c, w_out, ln1_g, ln1_b, w_router, router_bias,
                           moe_w_gate, moe_w_up, moe_w_down, ln2_g, ln2_b)
    return _run_trunks((x_prompt, x_sample), wts)
```

```python
import functools

import jax
import jax.numpy as jnp
from jax import lax
from jax.experimental import pallas as pl
from jax.experimental.pallas import tpu as pltpu

D_MODEL = 2048
DEPTH = 4
HEAD_DIM = 128
GRID_W = 64
ROPE_THETA = 10000.0
LN_EPS = 1e-5
RMS_EPS = 1e-6
GLA_HEADS = 4
GLA_DK = 128
GLA_DV = 256
GLA_RANK = 16
GLA_TAU = 16.0
GLA_CHUNK = 64
DIL_CONFIGS = ((128, 1), (512, 4), (2048, 16))
DIL_HEADS = 4
GQA_Q_HEADS = 8
GQA_KV_HEADS = 2
N_EXPERTS = 16
N_EXPERT_GROUPS = 4
EXPERTS_PER_GROUP = N_EXPERTS // N_EXPERT_GROUPS
D_FF_EXPERT = 1024
DEEPNORM_ALPHA = (2.0 * DEPTH) ** 0.25

A_QK = GLA_HEADS * GLA_DK
A_V = GLA_HEADS * GLA_DV
A_LR = 2 * GLA_RANK
B_HEADS = len(DIL_CONFIGS) * DIL_HEADS
B_QKV = 3 * B_HEADS * HEAD_DIM
B_OUT = DIL_HEADS * HEAD_DIM
C_Q = GQA_Q_HEADS * HEAD_DIM
C_KV = GQA_KV_HEADS * HEAD_DIM
GATE_COLS = 3 * D_MODEL
IN_SPLITS = (A_QK, A_QK, A_V, A_V, A_LR, B_QKV, C_Q, C_KV, C_KV, GATE_COLS)

LANES = 128
VMEM_LIMIT = 56 * 1024 * 1024
LR_PAD = LANES

OFF_GATE = 0
OFF_AQ = OFF_GATE + GATE_COLS
OFF_AK = OFF_AQ + A_QK
OFF_AV = OFF_AK + A_QK
OFF_AR = OFF_AV + A_V
OFF_CQ = OFF_AR + A_V
OFF_CK = OFF_CQ + C_Q
OFF_CV = OFF_CK + C_KV
OFF_LR = OFF_CV + C_KV
MAIN_USED = OFF_LR + LR_PAD

TM_PROJ = 1024
TN_MAIN = 1024
TM_ROUTE = 2048
MAIN_COLS = -(-MAIN_USED // TN_MAIN) * TN_MAIN
TN_B = B_HEADS * HEAD_DIM
TM_TOK = 256
TM_MOE = 512
MOE_GATHER_ROWS = 22528
COMBINE_GATHER_ROWS = 16384
GATHER_TABLE_ROWS = 32768
TM_MAIN = 2048
GQA_TQ = 512
GQA_TK = 1024
GQA_GROUP = 2
GQA_STRIP = 32
LOG2_E = 1.4426950408889634
GLA_SUPER = 8
DIL_UNROLL = 8
DIL_HALF = 64
NEG_BIG = -1e30

f32 = jnp.float32
bf16 = jnp.bfloat16


def _cparams(sem):
    return pltpu.CompilerParams(dimension_semantics=sem, vmem_limit_bytes=VMEM_LIMIT)


def _nt_dot(a, b):
    return lax.dot_general(a, b, (((1,), (1,)), ((), ())), preferred_element_type=f32)


def _tn_dot(a, b):
    return lax.dot_general(a, b, (((0,), (0,)), ((), ())), preferred_element_type=f32)


def _axial_partner(x):
    lane = lax.broadcasted_iota(jnp.int32, x.shape, 1)
    quarter = HEAD_DIM // 4
    first = (lane % (2 * quarter)) < quarter
    return jnp.where(first, pltpu.roll(x, HEAD_DIM - quarter, 1), pltpu.roll(x, quarter, 1))


def _mm_kernel(x_ref, w_ref, o_ref):
    o_ref[...] = jnp.dot(x_ref[...], w_ref[...], preferred_element_type=f32).astype(o_ref.dtype)


def _proj_main(xb, w_main, layer):
    n, k = xb.shape
    grid = (n // TM_MAIN, MAIN_COLS // TN_MAIN)
    return pl.pallas_call(
        _mm_kernel,
        grid=grid,
        in_specs=[
            pl.BlockSpec((TM_MAIN, k), lambda i, j: (i, 0)),
            pl.BlockSpec((None, k, TN_MAIN), lambda i, j: (layer, 0, j)),
        ],
        out_specs=pl.BlockSpec((TM_MAIN, TN_MAIN), lambda i, j: (i, j)),
        out_shape=jax.ShapeDtypeStruct((n, MAIN_COLS), bf16),
        compiler_params=_cparams(("parallel", "parallel")),
    )(xb, w_main)


def _proj_b_kernel(x_ref, w_ref, cos_ref, sin_ref, o_ref):
    j = pl.program_id(0)
    pair = 2 * HEAD_DIM

    def emit(rotary, scale):
        x = x_ref[...]
        if rotary:
            cos = cos_ref[...] * scale
            sin = sin_ref[...] * scale
        for c in range(TN_B // pair):
            acc = jnp.dot(x, w_ref[:, c * pair:(c + 1) * pair], preferred_element_type=f32)
            for h in range(2):
                a = acc[:, h * HEAD_DIM:(h + 1) * HEAD_DIM]
                if rotary:
                    a = a * cos + pltpu.roll(a, HEAD_DIM // 2, 1) * sin
                o_ref[2 * c + h] = a.astype(o_ref.dtype)

    @pl.when(j == 0)
    def _():
        emit(True, HEAD_DIM ** -0.5)

    @pl.when(j == 1)
    def _():
        emit(True, 1.0)

    @pl.when(j == 2)
    def _():
        emit(False, 1.0)


def _proj_b(xb, w_b, cos_b, sin_b, layer, t):
    n, k = xb.shape
    tpb = t // TM_PROJ
    grid = (B_QKV // TN_B, n // TM_PROJ)
    hpt = TN_B // HEAD_DIM
    return pl.pallas_call(
        _proj_b_kernel,
        grid=grid,
        in_specs=[
            pl.BlockSpec((TM_PROJ, k), lambda j, i: (i, 0)),
            pl.BlockSpec((None, k, TN_B), lambda j, i: (layer, 0, j)),
            pl.BlockSpec((TM_PROJ, HEAD_DIM), lambda j, i: (i % tpb, 0)),
            pl.BlockSpec((TM_PROJ, HEAD_DIM), lambda j, i: (i % tpb, 0)),
        ],
        out_specs=pl.BlockSpec((hpt, TM_PROJ, HEAD_DIM), lambda j, i: (j, i, 0)),
        out_shape=jax.ShapeDtypeStruct((3 * B_HEADS, n, HEAD_DIM), bf16),
        compiler_params=_cparams(("parallel", "parallel")),
    )(xb, w_b, cos_b, sin_b)


def _gla_kernel(q_ref, k_ref, v_ref, r_ref, lr_ref, w2f_ref, bf_ref, w2b_ref, bb_ref, g_ref,
                o_ref, accf_ref, accb_ref, *, t):
    c_len = GLA_CHUNK
    sup = GLA_SUPER * c_len
    n_sup = t // sup
    row = lax.broadcasted_iota(jnp.int32, (sup, sup), 0)
    col = lax.broadcasted_iota(jnp.int32, (sup, sup), 1)
    same = (row // c_len) == (col // c_len)

    def stage_decay(base, fwd):
        w2 = (w2f_ref if fwd else w2b_ref)[...].astype(bf16)
        bias = (bf_ref if fwd else bb_ref)[...]
        lo = 0 if fwd else GLA_RANK
        lr = lr_ref[pl.ds(base, sup), :][:, lo:lo + GLA_RANK]
        z = jnp.dot(lr, w2, preferred_element_type=f32) + bias
        g = (jnp.minimum(z, 0.0) - jnp.log(1.0 + jnp.exp(-jnp.abs(z)))) * (1.0 / GLA_TAU)
        g_hi = g.astype(bf16)
        g_lo = (g - g_hi.astype(f32)).astype(bf16)
        return g_hi, g_lo

    def stage_cumsum(parts, fwd):
        g_hi, g_lo = parts
        tri = jnp.where(same & ((row >= col) if fwd else (row <= col)), 1.0, 0.0).astype(bf16)
        b = (jnp.dot(tri, g_hi, preferred_element_type=f32)
             + jnp.dot(tri, g_lo, preferred_element_type=f32))
        edge = c_len - 1 if fwd else 0
        tot = jnp.concatenate(
            [jnp.broadcast_to(b[c * c_len + edge:c * c_len + edge + 1, :], (c_len, GLA_DK))
             for c in range(GLA_SUPER)], axis=0)
        return b, tot

    def stage_scale(base, b, tot):
        sl = pl.ds(base, sup)
        q = q_ref[sl, :].astype(f32) * (GLA_DK ** -0.5)
        k = k_ref[sl, :].astype(f32)
        q_t = (q * jnp.exp(b)).astype(bf16)
        k_t = (k * jnp.exp(-b)).astype(bf16)
        k_s = (k * jnp.exp(tot - b)).astype(bf16)
        return q_t, k_t, k_s, jnp.exp(tot)

    def stage_intra(base, q_t, k_t, fwd):
        smask = same & ((col <= row) if fwd else (col > row))
        scores = jnp.where(smask, _nt_dot(q_t, k_t), 0.0)
        return jnp.dot(scores.astype(bf16), v_ref[pl.ds(base, sup), :], preferred_element_type=f32)

    def chunk_step(base, c, fwd, q_t, k_s, dec, o_intra, s_t):
        acc_ref = accf_ref if fwd else accb_ref
        rs = slice(c * c_len, (c + 1) * c_len)
        rows_c = pl.ds(base + c * c_len, c_len)
        acc_ref[rows_c, :] = o_intra[rs, :] + _nt_dot(q_t[rs, :], s_t.astype(bf16))
        return s_t * dec[c * c_len:c * c_len + 1, :] + _tn_dot(v_ref[rows_c, :], k_s[rs, :])

    def body(i, carry):
        s_f, s_b = carry
        base_f = pl.multiple_of(i * sup, sup)
        base_b = pl.multiple_of((n_sup - 1 - i) * sup, sup)
        parts_f = stage_decay(base_f, True)
        parts_b = stage_decay(base_b, False)
        b_f, tot_f = stage_cumsum(parts_f, True)
        b_b, tot_b = stage_cumsum(parts_b, False)
        qt_f, kt_f, ks_f, dec_f = stage_scale(base_f, b_f, tot_f)
        qt_b, kt_b, ks_b, dec_b = stage_scale(base_b, b_b, tot_b)
        oi_f = stage_intra(base_f, qt_f, kt_f, True)
        oi_b = stage_intra(base_b, qt_b, kt_b, False)
        for c in range(GLA_SUPER):
            s_f = chunk_step(base_f, c, True, qt_f, ks_f, dec_f, oi_f, s_f)
            s_b = chunk_step(base_b, GLA_SUPER - 1 - c, False, qt_b, ks_b, dec_b, oi_b, s_b)
        return s_f, s_b

    zero = jnp.zeros((GLA_DV, GLA_DK), f32)
    lax.fori_loop(0, n_sup, body, (zero, zero))

    rows = 256
    gain = g_ref[...]

    def finish(i, carry):
        sl = pl.ds(pl.multiple_of(i * rows, rows), rows)
        x = accf_ref[sl, :] + accb_ref[sl, :]
        ms = jnp.mean(x * x, axis=-1, keepdims=True)
        y = x * lax.rsqrt(ms + RMS_EPS) * gain
        r = r_ref[sl, :].astype(f32)
        o_ref[sl, :] = (y * (r * jax.nn.sigmoid(r))).astype(o_ref.dtype)
        return carry

    lax.fori_loop(0, t // rows, finish, 0)


def _gla(main, w2f, b_f, w2b, b_b, gain, layer, bsz, t):
    n = bsz * t
    qk_blk = lambda off: (lambda b, h: (b, off // GLA_DK + h))
    v_blk = lambda off: (lambda b, h: (b, off // GLA_DV + h))
    wspec = pl.BlockSpec((None, GLA_RANK, GLA_DK), lambda b, h: (layer, 0, h))
    bspec = pl.BlockSpec((None, 1, GLA_DK), lambda b, h: (layer, 0, h))
    return pl.pallas_call(
        functools.partial(_gla_kernel, t=t),
        grid=(bsz, GLA_HEADS),
        in_specs=[
            pl.BlockSpec((t, GLA_DK), qk_blk(OFF_AQ)),
            pl.BlockSpec((t, GLA_DK), qk_blk(OFF_AK)),
            pl.BlockSpec((t, GLA_DV), v_blk(OFF_AV)),
            pl.BlockSpec((t, GLA_DV), v_blk(OFF_AR)),
            pl.BlockSpec((t, LR_PAD), lambda b, h: (b, OFF_LR // LR_PAD)),
            wspec, bspec, wspec, bspec,
            pl.BlockSpec((None, 1, GLA_DV), lambda b, h: (layer, 0, h)),
        ],
        out_specs=pl.BlockSpec((t, GLA_DV), lambda b, h: (b, h)),
        out_shape=jax.ShapeDtypeStruct((n, A_V), bf16),
        scratch_shapes=[pltpu.VMEM((t, GLA_DV), f32), pltpu.VMEM((t, GLA_DV), f32)],
        compiler_params=_cparams(("parallel", "parallel")),
    )(main, main, main, main, main, w2f, b_f, w2b, b_b, gain)


def _dil_kernel(q_ref, k_ref, v_ref, o_ref, lse_ref, *scratch, t, dil):
    n_sub = t // dil
    bq = min(128, n_sub)
    win = min(bq + 2 * DIL_HALF, n_sub)
    n_blk = n_sub // bq
    qi = lax.broadcasted_iota(jnp.int32, (bq, win), 0)
    ki = lax.broadcasted_iota(jnp.int32, (bq, win), 1)
    cvt = 512

    if dil > 1:
        q32, k32, v32, o32 = scratch

        def widen(i, carry):
            sl = pl.ds(pl.multiple_of(i * cvt, cvt), cvt)
            q32[sl, :] = q_ref[0, sl, :].astype(f32)
            k32[sl, :] = k_ref[0, sl, :].astype(f32)
            v32[sl, :] = v_ref[0, sl, :].astype(f32)
            return carry

        lax.fori_loop(0, t // cvt, widen, 0)

    def body(idx, carry):
        r = idx // n_blk
        i = idx % n_blk
        q0 = i * bq
        k0 = jnp.clip(q0 - DIL_HALF, 0, n_sub - win)
        if dil == 1:
            qs = pl.ds(pl.multiple_of(q0, bq), bq)
            ks = pl.ds(pl.multiple_of(k0, DIL_HALF), win)
            q = q_ref[0, qs, :]
            k = k_ref[0, ks, :]
            v = v_ref[0, ks, :]
        else:
            qs = pl.ds(r + q0 * dil, bq, stride=dil)
            ks = pl.ds(r + k0 * dil, win, stride=dil)
            q = q32[qs, :].astype(bf16)
            k = k32[ks, :].astype(bf16)
            v = v32[ks, :].astype(bf16)
        s = _nt_dot(q, k)
        valid = jnp.abs((q0 + qi) - (k0 + ki)) <= DIL_HALF
        s = jnp.where(valid, s, NEG_BIG)
        m = jnp.max(s, axis=-1, keepdims=True)
        p = jnp.exp(s - m)
        l = jnp.sum(p, axis=-1, keepdims=True)
        o = jnp.dot(p.astype(bf16), v, preferred_element_type=f32) / l
        lse = jnp.broadcast_to(m + jnp.log(l), (bq, HEAD_DIM))
        if dil == 1:
            o_ref[0, qs, :] = o.astype(o_ref.dtype)
            lse_ref[0, qs, :] = lse
        else:
            o32[qs, :] = o
            lse_ref[0, qs, :] = lse
        return carry

    def body_group(j, carry):
        for u in range(DIL_UNROLL):
            body(j * DIL_UNROLL + u, carry)
        return carry

    lax.fori_loop(0, dil * n_blk // DIL_UNROLL, body_group, 0)

    if dil > 1:
        def narrow(i, carry):
            sl = pl.ds(pl.multiple_of(i * cvt, cvt), cvt)
            o_ref[0, sl, :] = o32[sl, :].astype(o_ref.dtype)
            return carry

        lax.fori_loop(0, t // cvt, narrow, 0)


def _dilated(qkv_b, group, bsz, t):
    _, dil = DIL_CONFIGS[group]
    n = bsz * t
    blk = (1, t, HEAD_DIM)
    head0 = group * DIL_HEADS
    scratch = [pltpu.VMEM((t, HEAD_DIM), f32)] * 4 if dil > 1 else []
    return pl.pallas_call(
        functools.partial(_dil_kernel, t=t, dil=dil),
        grid=(bsz, DIL_HEADS),
        in_specs=[
            pl.BlockSpec(blk, lambda b, h: (head0 + h, b, 0)),
            pl.BlockSpec(blk, lambda b, h: (B_HEADS + head0 + h, b, 0)),
            pl.BlockSpec(blk, lambda b, h: (2 * B_HEADS + head0 + h, b, 0)),
        ],
        out_specs=[
            pl.BlockSpec(blk, lambda b, h: (h, b, 0)),
            pl.BlockSpec(blk, lambda b, h: (h, b, 0)),
        ],
        out_shape=[
            jax.ShapeDtypeStruct((DIL_HEADS, n, HEAD_DIM), bf16),
            jax.ShapeDtypeStruct((DIL_HEADS, n, HEAD_DIM), f32),
        ],
        scratch_shapes=scratch,
        compiler_params=_cparams(("parallel", "parallel")),
    )(qkv_b, qkv_b, qkv_b)


def _c_prep_kernel(q_ref, k_ref, cos_ref, sin_ref, qg_ref, kg_ref, qo_ref, ko_ref):
    cos = cos_ref[...]
    sin = sin_ref[...]

    def prep(x, gain, scale):
        x = x.astype(f32)
        ms = jnp.mean(x * x, axis=-1, keepdims=True)
        y = x * lax.rsqrt(ms + RMS_EPS) * gain
        return (y * cos + _axial_partner(y) * sin) * scale

    qg = qg_ref[...]
    kg = kg_ref[...]
    for h in range(GQA_Q_HEADS):
        ls = slice(h * HEAD_DIM, (h + 1) * HEAD_DIM)
        qo_ref[:, ls] = prep(q_ref[:, ls], qg, HEAD_DIM ** -0.5 * LOG2_E).astype(qo_ref.dtype)
    for h in range(GQA_KV_HEADS):
        ls = slice(h * HEAD_DIM, (h + 1) * HEAD_DIM)
        ko_ref[:, ls] = prep(k_ref[:, ls], kg, 1.0).astype(ko_ref.dtype)


def _c_prep(main, cos_c, sin_c, q_gain, k_gain, layer, t):
    n = main.shape[0]
    tm = TM_PROJ
    tpb = t // tm
    gspec = pl.BlockSpec((None, 1, HEAD_DIM), lambda i: (layer, 0, 0))
    return pl.pallas_call(
        _c_prep_kernel,
        grid=(n // tm,),
        in_specs=[
            pl.BlockSpec((tm, C_Q), lambda i: (i, OFF_CQ // C_Q)),
            pl.BlockSpec((tm, C_KV), lambda i: (i, OFF_CK // C_KV)),
            pl.BlockSpec((tm, HEAD_DIM), lambda i: (i % tpb, 0)),
            pl.BlockSpec((tm, HEAD_DIM), lambda i: (i % tpb, 0)),
            gspec, gspec,
        ],
        out_specs=[
            pl.BlockSpec((tm, C_Q), lambda i: (i, 0)),
            pl.BlockSpec((tm, C_KV), lambda i: (i, 0)),
        ],
        out_shape=[
            jax.ShapeDtypeStruct((n, C_Q), bf16),
            jax.ShapeDtypeStruct((n, C_KV), bf16),
        ],
        compiler_params=_cparams(("parallel",)),
    )(main, main, cos_c, sin_c, q_gain, k_gain)


def _gqa_kernel(q_ref, k_ref, v_ref, o_ref, s_scr, p_scr, m_scr, l_scr, acc_scr, *, t, tq, tk):
    grp = GQA_Q_HEADS // GQA_KV_HEADS
    q4 = q_ref[...]
    q = jnp.concatenate([q4[:, h * HEAD_DIM:(h + 1) * HEAD_DIM] for h in range(grp)], axis=0)
    rows = grp * tq
    n_strips = rows // GQA_STRIP
    m_scr[...] = jnp.full((rows, LANES), NEG_BIG, f32)
    l_scr[...] = jnp.zeros((rows, LANES), f32)
    acc_scr[...] = jnp.zeros((rows, HEAD_DIM), f32)

    def widen(col):
        return jnp.broadcast_to(col, (col.shape[0], LANES))

    def softmax_chunk(buf):
        mx = []
        for i in range(n_strips):
            rs = slice(i * GQA_STRIP, (i + 1) * GQA_STRIP)
            mx.append(widen(jnp.max(s_scr[buf, rs, :], axis=-1, keepdims=True)))
        m_old = m_scr[...]
        m_new = jnp.maximum(m_old, jnp.concatenate(mx, axis=0))
        a = jnp.exp2(m_old - m_new)
        m_scr[...] = m_new
        sums = []
        for i in range(n_strips):
            rs = slice(i * GQA_STRIP, (i + 1) * GQA_STRIP)
            m_wide = jnp.concatenate([m_new[rs, :]] * (tk // LANES), axis=1)
            p = jnp.exp2(s_scr[buf, rs, :] - m_wide)
            sums.append(widen(jnp.sum(p, axis=-1, keepdims=True)))
            p_scr[buf, rs, :] = p.astype(bf16)
        l_scr[...] = a * l_scr[...] + jnp.concatenate(sums, axis=0)
        return a

    def chunk_group(cg, carry):
        sls = []
        for u in range(GQA_GROUP):
            sl = pl.ds(pl.multiple_of((cg * GQA_GROUP + u) * tk, tk), tk)
            sls.append(sl)
            s_scr[u] = _nt_dot(q, k_ref[sl, :])
        for u in range(GQA_GROUP):
            a = softmax_chunk(u)
            acc_scr[...] = acc_scr[...] * a + jnp.dot(p_scr[u], v_ref[sls[u], :],
                                                      preferred_element_type=f32)
        return carry

    lax.fori_loop(0, t // (tk * GQA_GROUP), chunk_group, 0)
    o = (acc_scr[...] / l_scr[...]).astype(o_ref.dtype)
    o_ref[...] = jnp.concatenate([o[h * tq:(h + 1) * tq, :] for h in range(grp)], axis=1)


def _gqa(qc, kc, main, bsz, t):
    n = bsz * t
    tq = GQA_TQ
    tk = min(t, GQA_TK)
    grp = GQA_Q_HEADS // GQA_KV_HEADS
    grp_cols = grp * HEAD_DIM
    rows = grp * tq
    nq = t // tq
    return pl.pallas_call(
        functools.partial(_gqa_kernel, t=t, tq=tq, tk=tk),
        grid=(bsz, GQA_KV_HEADS, nq),
        in_specs=[
            pl.BlockSpec((tq, grp_cols), lambda b, j, i: (b * nq + i, j)),
            pl.BlockSpec((t, HEAD_DIM), lambda b, j, i: (b, j)),
            pl.BlockSpec((t, HEAD_DIM), lambda b, j, i: (b, OFF_CV // HEAD_DIM + j)),
        ],
        out_specs=pl.BlockSpec((tq, grp_cols), lambda b, j, i: (b * nq + i, j)),
        out_shape=jax.ShapeDtypeStruct((n, C_Q), bf16),
        scratch_shapes=[
            pltpu.VMEM((GQA_GROUP, rows, tk), f32),
            pltpu.VMEM((GQA_GROUP, rows, tk), bf16),
            pltpu.VMEM((rows, LANES), f32),
            pltpu.VMEM((rows, LANES), f32),
            pltpu.VMEM((rows, HEAD_DIM), f32),
        ],
        compiler_params=_cparams(("parallel", "parallel", "parallel")),
    )(qc, kc, main)


def _merge_kernel(oa_ref, ob0_ref, ob1_ref, ob2_ref, l0_ref, l1_ref, l2_ref, oc_ref,
                  ga_ref, gb_ref, gc_ref, wa_ref, wb_ref, wc_ref, o_ref):
    heads = []
    for h in range(DIL_HEADS):
        l0, l1, l2 = l0_ref[h], l1_ref[h], l2_ref[h]
        m = jnp.maximum(jnp.maximum(l0, l1), l2)
        e0, e1, e2 = jnp.exp(l0 - m), jnp.exp(l1 - m), jnp.exp(l2 - m)
        mix = (e0 * ob0_ref[h].astype(f32) + e1 * ob1_ref[h].astype(f32)
               + e2 * ob2_ref[h].astype(f32)) / (e0 + e1 + e2)
        heads.append(mix.astype(bf16))
    o_b = jnp.concatenate(heads, axis=1)
    up_a = jnp.dot(oa_ref[...], wa_ref[...], preferred_element_type=f32)
    up_b = jnp.dot(o_b, wb_ref[...], preferred_element_type=f32)
    up_c = jnp.dot(oc_ref[...], wc_ref[...], preferred_element_type=f32)
    merged = (jax.nn.sigmoid(ga_ref[...].astype(f32)) * up_a
              + jax.nn.sigmoid(gb_ref[...].astype(f32)) * up_b
              + jax.nn.sigmoid(gc_ref[...].astype(f32)) * up_c)
    o_ref[...] = merged.astype(o_ref.dtype)


def _merge(o_a, o_bs, lses, o_c, main, w_up_a, w_up_b, w_up_c, layer):
    n = o_a.shape[0]
    tm = TM_TOK
    gate_blk = OFF_GATE // D_MODEL
    hspec = pl.BlockSpec((DIL_HEADS, tm, HEAD_DIM), lambda i: (0, i, 0))
    gspec = lambda which: pl.BlockSpec((tm, D_MODEL), lambda i: (i, gate_blk + which))
    wspec = lambda rows: pl.BlockSpec((None, rows, D_MODEL), lambda i: (layer, 0, 0))
    return pl.pallas_call(
        _merge_kernel,
        grid=(n // tm,),
        in_specs=[
            pl.BlockSpec((tm, A_V), lambda i: (i, 0)),
            hspec, hspec, hspec, hspec, hspec, hspec,
            pl.BlockSpec((tm, C_Q), lambda i: (i, 0)),
            gspec(0), gspec(1), gspec(2),
            wspec(A_V), wspec(B_OUT), wspec(C_Q),
        ],
        out_specs=pl.BlockSpec((tm, D_MODEL), lambda i: (i, 0)),
        out_shape=jax.ShapeDtypeStruct((n, D_MODEL), bf16),
        compiler_params=_cparams(("parallel",)),
    )(o_a, o_bs[0], o_bs[1], o_bs[2], lses[0], lses[1], lses[2], o_c,
      main, main, main, w_up_a, w_up_b, w_up_c)


def _layer_norm_rows(y, g, b):
    mu = jnp.mean(y, axis=-1, keepdims=True)
    d = y - mu
    var = jnp.mean(d * d, axis=-1, keepdims=True)
    return d * lax.rsqrt(var + LN_EPS) * g + b


def _route_rows(logits_t, bias_col):
    scores = jax.nn.sigmoid(logits_t)
    sel = scores + bias_col
    rows = [sel[e:e + 1, :] for e in range(N_EXPERTS)]
    srow = [scores[e:e + 1, :] for e in range(N_EXPERTS)]
    best_val = None
    best_grp = None
    for g in range(N_EXPERT_GROUPS):
        mem = rows[g * EXPERTS_PER_GROUP:(g + 1) * EXPERTS_PER_GROUP]
        top2 = None
        for a in range(EXPERTS_PER_GROUP):
            for b in range(a + 1, EXPERTS_PER_GROUP):
                pair = mem[a] + mem[b]
                top2 = pair if top2 is None else jnp.maximum(top2, pair)
        if g == 0:
            best_val, best_grp = top2, jnp.zeros_like(top2, dtype=jnp.int32)
        else:
            upd = top2 > best_val
            best_val = jnp.where(upd, top2, best_val)
            best_grp = jnp.where(upd, g, best_grp)
    neg = jnp.full_like(best_val, -jnp.inf)
    cand = [jnp.where(best_grp == (e // EXPERTS_PER_GROUP), rows[e], neg) for e in range(N_EXPERTS)]

    def arg_top(vals):
        bv, bi = vals[0], jnp.zeros_like(best_grp)
        for e in range(1, N_EXPERTS):
            upd = vals[e] > bv
            bv = jnp.where(upd, vals[e], bv)
            bi = jnp.where(upd, e, bi)
        return bi

    idx1 = arg_top(cand)
    idx2 = arg_top([jnp.where(idx1 == e, neg, cand[e]) for e in range(N_EXPERTS)])
    zero = jnp.zeros_like(best_val)
    s1 = zero
    s2 = zero
    for e in range(N_EXPERTS):
        s1 = s1 + jnp.where(idx1 == e, srow[e], zero)
        s2 = s2 + jnp.where(idx2 == e, srow[e], zero)
    tot = s1 + s2
    return idx1, idx2, s1 / tot, s2 / tot


def _outproj_kernel(m_ref, w_ref, x_ref, g_ref, b_ref, wr_ref, x1_ref, x1b_ref, lg_ref):
    h = jnp.dot(m_ref[...], w_ref[...], preferred_element_type=f32)
    x1 = _layer_norm_rows(DEEPNORM_ALPHA * x_ref[...] + h, g_ref[...], b_ref[...])
    x1_ref[...] = x1
    x_hi = x1.astype(bf16)
    x1b_ref[...] = x_hi
    x_lo = (x1 - x_hi.astype(f32)).astype(bf16)
    wr = wr_ref[...]
    w_hi = wr.astype(bf16)
    w_lo = (wr - w_hi.astype(f32)).astype(bf16)
    lg_ref[...] = (jnp.dot(x_hi, w_hi, preferred_element_type=f32)
                   + jnp.dot(x_hi, w_lo, preferred_element_type=f32)
                   + jnp.dot(x_lo, w_hi, preferred_element_type=f32))


def _outproj_ln(merged, w_out, x, ln_g, ln_b, w_router_pad, layer):
    n = x.shape[0]
    tm = TM_TOK
    vspec = pl.BlockSpec((None, 1, D_MODEL), lambda i: (layer, 0, 0))
    row = pl.BlockSpec((tm, D_MODEL), lambda i: (i, 0))
    return pl.pallas_call(
        _outproj_kernel,
        grid=(n // tm,),
        in_specs=[
            row,
            pl.BlockSpec((None, D_MODEL, D_MODEL), lambda i: (layer, 0, 0)),
            row,
            vspec, vspec,
            pl.BlockSpec((D_MODEL, LANES), lambda i: (0, 0)),
        ],
        out_specs=[row, row, pl.BlockSpec((tm, LANES), lambda i: (i, 0))],
        out_shape=[
            jax.ShapeDtypeStruct((n, D_MODEL), f32),
            jax.ShapeDtypeStruct((n, D_MODEL), bf16),
            jax.ShapeDtypeStruct((n, LANES), f32),
        ],
        compiler_params=_cparams(("parallel",)),
    )(merged, w_out, x, ln_g, ln_b, w_router_pad)


def _route_kernel(lg_ref, rb_ref, idx_ref, w1_ref, w2_ref):
    logits_t = lg_ref[...].T[0:N_EXPERTS, :]
    i1, i2, w1, w2 = _route_rows(logits_t, rb_ref[...])
    idx_ref[0:1, :] = i1
    idx_ref[1:2, :] = i2
    tr = logits_t.shape[1]
    w1_ref[...] = jnp.broadcast_to(w1, (LANES, tr)).T
    w2_ref[...] = jnp.broadcast_to(w2, (LANES, tr)).T


def _route(logits, router_bias_col):
    n = logits.shape[0]
    tr = TM_ROUTE
    return pl.pallas_call(
        _route_kernel,
        grid=(n // tr,),
        in_specs=[
            pl.BlockSpec((tr, LANES), lambda i: (i, 0)),
            pl.BlockSpec((N_EXPERTS, 1), lambda i: (0, 0)),
        ],
        out_specs=[
            pl.BlockSpec((2, tr), lambda i: (0, i)),
            pl.BlockSpec((tr, LANES), lambda i: (i, 0)),
            pl.BlockSpec((tr, LANES), lambda i: (i, 0)),
        ],
        out_shape=[
            jax.ShapeDtypeStruct((2, n), jnp.int32),
            jax.ShapeDtypeStruct((n, LANES), f32),
            jax.ShapeDtypeStruct((n, LANES), f32),
        ],
        compiler_params=_cparams(("parallel",)),
    )(logits, router_bias_col)


def _moe_kernel(te_ref, nt_ref, *refs, tiles_per_piece):
    del te_ref
    x_refs, (wg_ref, wu_ref, wd_ref, o_ref) = refs[:-4], refs[-4:]
    tile = pl.program_id(0)

    @pl.when(tile < nt_ref[0])
    def _():
        x = x_refs[0][...]
        for p in range(1, len(x_refs)):
            x = jnp.where(tile >= p * tiles_per_piece, x_refs[p][...], x)
        gate = jnp.dot(x, wg_ref[...], preferred_element_type=f32)
        up = jnp.dot(x, wu_ref[...], preferred_element_type=f32)
        h = (gate * jax.nn.sigmoid(gate) * up).astype(bf16)
        o_ref[...] = jnp.dot(h, wd_ref[...], preferred_element_type=f32).astype(o_ref.dtype)

    @pl.when(tile >= nt_ref[0])
    def _():
        o_ref[...] = jnp.zeros_like(o_ref)


def _moe_grouped(x_pieces, tile_expert, n_tiles_used, w_gate, w_up, w_down, layer):
    tpp = x_pieces[0].shape[0] // TM_MOE
    n_tiles = tpp * len(x_pieces)
    wspec_in = pl.BlockSpec((None, None, D_MODEL, D_FF_EXPERT), lambda t, te, nt: (layer, te[t], 0, 0))
    wspec_out = pl.BlockSpec((None, None, D_FF_EXPERT, D_MODEL), lambda t, te, nt: (layer, te[t], 0, 0))

    def piece_spec(p):
        return pl.BlockSpec((TM_MOE, D_MODEL),
                            lambda t, te, nt: (jnp.clip(t - p * tpp, 0, tpp - 1), 0))

    grid_spec = pltpu.PrefetchScalarGridSpec(
        num_scalar_prefetch=2,
        grid=(n_tiles,),
        in_specs=[piece_spec(p) for p in range(len(x_pieces))] + [wspec_in, wspec_in, wspec_out],
        out_specs=pl.BlockSpec((TM_MOE, D_MODEL), lambda t, te, nt: (t, 0)),
    )
    return pl.pallas_call(
        functools.partial(_moe_kernel, tiles_per_piece=tpp),
        grid_spec=grid_spec,
        out_shape=jax.ShapeDtypeStruct((n_tiles * TM_MOE, D_MODEL), bf16),
        compiler_params=_cparams(("arbitrary",)),
    )(tile_expert, n_tiles_used, *x_pieces, w_gate, w_up, w_down)


def _combine_kernel(x_ref, *refs, n_pieces, tiles_per_piece):
    y1_refs, y2_refs = refs[:n_pieces], refs[n_pieces:2 * n_pieces]
    w1_ref, w2_ref, g_ref, b_ref, o_ref, ob_ref = refs[2 * n_pieces:]
    tile = pl.program_id(0)

    def pick(piece_refs):
        y = piece_refs[0][...]
        for p in range(1, n_pieces):
            y = jnp.where(tile >= p * tiles_per_piece, piece_refs[p][...], y)
        return y

    y1 = pick(y1_refs)
    y2 = pick(y2_refs)
    w1 = w1_ref[...]
    w2 = w2_ref[...]
    for c in range(D_MODEL // LANES):
        ls = slice(c * LANES, (c + 1) * LANES)
        o_ref[:, ls] = (DEEPNORM_ALPHA * x_ref[:, ls] + w1 * y1[:, ls].astype(f32)
                        + w2 * y2[:, ls].astype(f32))
    x2 = _layer_norm_rows(o_ref[...], g_ref[...], b_ref[...])
    o_ref[...] = x2
    ob_ref[...] = x2.astype(bf16)


def _combine_ln(x1, y1_pieces, y2_pieces, w1b, w2b, ln_g, ln_b, layer):
    n = x1.shape[0]
    tm = TM_TOK
    n_pieces = len(y1_pieces)
    tpp = y1_pieces[0].shape[0] // tm
    row = pl.BlockSpec((tm, D_MODEL), lambda i: (i, 0))
    wsp = pl.BlockSpec((tm, LANES), lambda i: (i, 0))
    vspec = pl.BlockSpec((None, 1, D_MODEL), lambda i: (layer, 0, 0))

    def piece_spec(p):
        return pl.BlockSpec((tm, D_MODEL), lambda i: (jnp.clip(i - p * tpp, 0, tpp - 1), 0))

    pieces = [piece_spec(p) for p in range(n_pieces)]
    return pl.pallas_call(
        functools.partial(_combine_kernel, n_pieces=n_pieces, tiles_per_piece=tpp),
        grid=(n // tm,),
        in_specs=[row] + pieces + pieces + [wsp, wsp, vspec, vspec],
        out_specs=[row, row],
        out_shape=[jax.ShapeDtypeStruct((n, D_MODEL), f32), jax.ShapeDtypeStruct((n, D_MODEL), bf16)],
        compiler_params=_cparams(("parallel",)),
    )(x1, *y1_pieces, *y2_pieces, w1b, w2b, ln_g, ln_b)


def _dispatch_plan(idx):
    n = idx.shape[1]
    m_pad = 2 * n + N_EXPERTS * TM_MOE
    n_tiles = m_pad // TM_MOE
    e_flat = idx.reshape(-1)
    onehot = (e_flat[:, None] == jnp.arange(N_EXPERTS, dtype=jnp.int32)[None, :]).astype(jnp.int32)
    rank = jnp.sum((jnp.cumsum(onehot, axis=0) - onehot) * onehot, axis=1)
    counts = jnp.sum(onehot, axis=0)
    tiles_per = (counts + TM_MOE - 1) // TM_MOE
    tile_end = jnp.cumsum(tiles_per)
    start_row = (tile_end - tiles_per) * TM_MOE
    dest = jnp.sum(onehot * start_row[None, :], axis=1) + rank
    tok = jnp.tile(jnp.arange(n, dtype=jnp.int32), 2)
    src = jnp.zeros((m_pad,), jnp.int32).at[dest].set(tok, mode="promise_in_bounds", unique_indices=True)
    n_used = tile_end[-1]
    tile_ids = jnp.minimum(jnp.arange(n_tiles, dtype=jnp.int32), n_used - 1)
    tile_expert = jnp.sum((tile_ids[:, None] >= tile_end[None, :]).astype(jnp.int32), axis=1)
    tile_expert = jnp.minimum(tile_expert, N_EXPERTS - 1)
    return src, dest, tile_expert, n_used.reshape(1).astype(jnp.int32)


def _rope_tables(pos, dim):
    inv_freq = ROPE_THETA ** (-jnp.arange(0, dim, 2, dtype=f32) / dim)
    ang = pos.astype(f32)[:, None] * inv_freq[None, :]
    return jnp.cos(ang), jnp.sin(ang)


def _tables(t):
    cos, sin = _rope_tables(jnp.arange(t), HEAD_DIM)
    cos_b = jnp.concatenate([cos, cos], axis=-1)
    sin_b = jnp.concatenate([-sin, sin], axis=-1)
    tok = jnp.arange(t)
    cr, sr = _rope_tables(tok // GRID_W, HEAD_DIM // 2)
    cc, sc = _rope_tables(tok % GRID_W, HEAD_DIM // 2)
    cos_c = jnp.concatenate([cr, cr, cc, cc], axis=-1)
    sin_c = jnp.concatenate([-sr, sr, -sc, sc], axis=-1)
    return cos_b, sin_b, cos_c, sin_c


def _repack_w_in(w_in):
    offs = [0]
    for s in IN_SPLITS:
        offs.append(offs[-1] + s)
    seg = [w_in[:, :, offs[i]:offs[i + 1]] for i in range(len(IN_SPLITS))]
    a_q, a_k, a_v, a_r, a_lr, b_qkv, c_q, c_k, c_v, gate = seg
    lr_pad = jnp.pad(a_lr, ((0, 0), (0, 0), (0, LR_PAD - A_LR)))
    w_main = jnp.concatenate([gate, a_q, a_k, a_v, a_r, c_q, c_k, c_v, lr_pad], axis=-1)
    w_main = jnp.pad(w_main, ((0, 0), (0, 0), (0, MAIN_COLS - MAIN_USED))).astype(bf16)
    return w_main, b_qkv.astype(bf16)


def _mixer(xb, wts, layer, bsz, t, tables):
    cos_b, sin_b, cos_c, sin_c = tables
    main = _proj_main(xb, wts["w_main"], layer)
    qkv_b = _proj_b(xb, wts["w_b"], cos_b, sin_b, layer, t)
    o_a = _gla(main, wts["gla_w2_f"], wts["gla_b_f"], wts["gla_w2_b"], wts["gla_b_b"],
               wts["gla_norm_g"], layer, bsz, t)
    o_bs, lses = [], []
    for group in range(len(DIL_CONFIGS)):
        o_g, lse_g = _dilated(qkv_b, group, bsz, t)
        o_bs.append(o_g)
        lses.append(lse_g)
    qc, kc = _c_prep(main, cos_c, sin_c, wts["q_norm_g"], wts["k_norm_g"], layer, t)
    o_c = _gqa(qc, kc, main, bsz, t)
    return _merge(o_a, o_bs, lses, o_c, main, wts["w_up_a"], wts["w_up_b"], wts["w_up_c"], layer)


def _gather_rows(a, rows):
    return a.at[rows].get(mode="promise_in_bounds")


class _Trunk:
    def __init__(self, x3, wts):
        self.bsz, self.t, _ = x3.shape
        self.wts = wts
        self.x = x3.reshape(self.bsz * self.t, D_MODEL)
        self.xb = self.x.astype(bf16)
        self.tables = _tables(self.t)

    def mix_and_route(self, layer):
        wts = self.wts
        merged = _mixer(self.xb, wts, layer, self.bsz, self.t, self.tables)
        self.x1, x1b, logits = _outproj_ln(merged, wts["w_out"], self.x, wts["ln1_g"], wts["ln1_b"],
                                           wts["w_router_pad"], layer)
        idx, self.w1b, self.w2b = _route(logits, wts["router_bias_col"])
        src, self.dest, self.tile_expert, self.n_used = _dispatch_plan(idx)
        n_tiles = src.shape[0] // TM_MOE
        max_rows = min(MOE_GATHER_ROWS, x1b.shape[0])
        n_pieces = next(d for d in range(1, n_tiles + 1)
                        if n_tiles % d == 0 and src.shape[0] // d <= max_rows)
        copies = -(-GATHER_TABLE_ROWS // x1b.shape[0])
        table = x1b if copies == 1 else jnp.concatenate([x1b] * copies, axis=0)
        self.x_pieces = [_gather_rows(table, piece) for piece in jnp.split(src, n_pieces)]

    def experts(self, layer):
        wts = self.wts
        n = self.x1.shape[0]
        y_sorted = _moe_grouped(self.x_pieces, self.tile_expert, self.n_used, wts["moe_w_gate"],
                                wts["moe_w_up"], wts["moe_w_down"], layer)
        c_pieces = -(-n // COMBINE_GATHER_ROWS)
        self.y1 = [_gather_rows(y_sorted, piece) for piece in jnp.split(self.dest[:n], c_pieces)]
        self.y2 = [_gather_rows(y_sorted, piece) for piece in jnp.split(self.dest[n:], c_pieces)]

    def combine(self, layer):
        self.x, self.xb = _combine_ln(self.x1, self.y1, self.y2, self.w1b, self.w2b,
                                      self.wts["ln2_g"], self.wts["ln2_b"], layer)

    def result(self):
        return self.x.reshape(self.bsz, self.t, D_MODEL)


def _run_trunks(inputs, wts, depth=DEPTH):
    trunks = [_Trunk(x3, wts) for x3 in inputs]
    for layer in range(depth):
        for stage in (_Trunk.mix_and_route, _Trunk.experts, _Trunk.combine):
            for trunk in trunks:
                stage(trunk, layer)
    return tuple(trunk.result() for trunk in trunks)


def _cast_kernel(x_ref, o_ref):
    o_ref[...] = x_ref[...].astype(o_ref.dtype)


def _expert_weights_bf16(w):
    n_layers, n_exp, rows, cols = w.shape
    spec = pl.BlockSpec((None, None, rows, cols), lambda i, j: (i, j, 0, 0))
    return pl.pallas_call(
        _cast_kernel,
        grid=(n_layers, n_exp),
        in_specs=[spec],
        out_specs=spec,
        out_shape=jax.ShapeDtypeStruct(w.shape, bf16),
        compiler_params=_cparams(("parallel", "parallel")),
    )(w)


def _prepare_weights(w_in, gla_w2_f, gla_b_f, gla_w2_b, gla_b_b, gla_norm_g, q_norm_g, k_norm_g,
                     w_up_a, w_up_b, w_up_c, w_out, ln1_g, ln1_b, w_router, router_bias,
                     moe_w_gate, moe_w_up, moe_w_down, ln2_g, ln2_b):
    w_main, w_b = _repack_w_in(w_in)
    row = lambda a: a.reshape(DEPTH, 1, a.shape[-1]).astype(f32)
    return {
        "w_main": w_main,
        "w_b": w_b,
        "gla_w2_f": gla_w2_f.astype(f32),
        "gla_b_f": row(gla_b_f),
        "gla_w2_b": gla_w2_b.astype(f32),
        "gla_b_b": row(gla_b_b),
        "gla_norm_g": row(gla_norm_g),
        "q_norm_g": row(q_norm_g),
        "k_norm_g": row(k_norm_g),
        "w_up_a": w_up_a.astype(bf16),
        "w_up_b": w_up_b.astype(bf16),
        "w_up_c": w_up_c.astype(bf16),
        "w_out": w_out.astype(bf16),
        "ln1_g": row(ln1_g),
        "ln1_b": row(ln1_b),
        "w_router_pad": jnp.pad(w_router.astype(f32), ((0, 0), (0, LANES - N_EXPERTS))),
        "router_bias_col": router_bias.astype(f32).reshape(N_EXPERTS, 1),
        "moe_w_gate": _expert_weights_bf16(moe_w_gate.astype(f32)),
        "moe_w_up": _expert_weights_bf16(moe_w_up.astype(f32)),
        "moe_w_down": _expert_weights_bf16(moe_w_down.astype(f32)),
        "ln2_g": row(ln2_g),
        "ln2_b": row(ln2_b),
    }


def kernel(x_prompt, x_sample, w_in, gla_w2_f, gla_b_f, gla_w2_b, gla_b_b, gla_norm_g, q_norm_g, k_norm_g, w_up_a, w_up_b, w_up_c, w_out, ln1_g, ln1_b, w_router, router_bias, moe_w_gate, moe_w_up, moe_w_down, ln2_g, ln2_b):
    wts = _prepare_weights(w_in, gla_w2_f, gla_b_f, gla_w2_b, gla_b_b, gla_norm_g, q_norm_g, k_norm_g,
                           w_up_a, w_up_b, w_up_c, w_out, ln1_g, ln1_b, w_router, router_bias,
                           moe_w_gate, moe_w_up, moe_w_down, ln2_g, ln2_b)
    return _run_trunks((x_prompt, x_sample), wts)
```

```python
import functools

import jax
import jax.numpy as jnp
from jax import lax
from jax.experimental import pallas as pl
from jax.experimental.pallas import tpu as pltpu

D_MODEL = 2048
DEPTH = 4
HEAD_DIM = 128
GRID_W = 64
ROPE_THETA = 10000.0
LN_EPS = 1e-5
RMS_EPS = 1e-6
GLA_HEADS = 4
GLA_DK = 128
GLA_DV = 256
GLA_RANK = 16
GLA_TAU = 16.0
GLA_CHUNK = 64
DIL_CONFIGS = ((128, 1), (512, 4), (2048, 16))
DIL_HEADS = 4
GQA_Q_HEADS = 8
GQA_KV_HEADS = 2
N_EXPERTS = 16
N_EXPERT_GROUPS = 4
EXPERTS_PER_GROUP = N_EXPERTS // N_EXPERT_GROUPS
D_FF_EXPERT = 1024
DEEPNORM_ALPHA = (2.0 * DEPTH) ** 0.25

A_QK = GLA_HEADS * GLA_DK
A_V = GLA_HEADS * GLA_DV
A_LR = 2 * GLA_RANK
B_HEADS = len(DIL_CONFIGS) * DIL_HEADS
B_QKV = 3 * B_HEADS * HEAD_DIM
B_OUT = DIL_HEADS * HEAD_DIM
C_Q = GQA_Q_HEADS * HEAD_DIM
C_KV = GQA_KV_HEADS * HEAD_DIM
GATE_COLS = 3 * D_MODEL
IN_SPLITS = (A_QK, A_QK, A_V, A_V, A_LR, B_QKV, C_Q, C_KV, C_KV, GATE_COLS)

LANES = 128
VMEM_LIMIT = 56 * 1024 * 1024
LR_PAD = LANES

OFF_GATE = 0
OFF_AQ = OFF_GATE + GATE_COLS
OFF_AK = OFF_AQ + A_QK
OFF_AV = OFF_AK + A_QK
OFF_AR = OFF_AV + A_V
OFF_CQ = OFF_AR + A_V
OFF_CK = OFF_CQ + C_Q
OFF_CV = OFF_CK + C_KV
OFF_LR = OFF_CV + C_KV
MAIN_USED = OFF_LR + LR_PAD

TM_PROJ = 1024
TN_MAIN = 1024
TM_ROUTE = 2048
MAIN_COLS = -(-MAIN_USED // TN_MAIN) * TN_MAIN
TN_B = B_HEADS * HEAD_DIM
TM_TOK = 256
TM_MOE = 512
MOE_GATHER_ROWS = 22528
COMBINE_GATHER_ROWS = 16384
GATHER_TABLE_ROWS = 32768
TM_MAIN = 2048
GQA_TQ = 512
GQA_TK = 1024
GQA_GROUP = 2
GQA_STRIP = 32
LOG2_E = 1.4426950408889634
GLA_SUPER = 8
DIL_UNROLL = 8
DIL_HALF = 64
NEG_BIG = -1e30

f32 = jnp.float32
bf16 = jnp.bfloat16


def _cparams(sem):
    return pltpu.CompilerParams(dimension_semantics=sem, vmem_limit_bytes=VMEM_LIMIT)


def _nt_dot(a, b):
    return lax.dot_general(a, b, (((1,), (1,)), ((), ())), preferred_element_type=f32)


def _tn_dot(a, b):
    return lax.dot_general(a, b, (((0,), (0,)), ((), ())), preferred_element_type=f32)


def _axial_partner(x):
    lane = lax.broadcasted_iota(jnp.int32, x.shape, 1)
    quarter = HEAD_DIM // 4
    first = (lane % (2 * quarter)) < quarter
    return jnp.where(first, pltpu.roll(x, HEAD_DIM - quarter, 1), pltpu.roll(x, quarter, 1))


def _mm_kernel(x_ref, w_ref, o_ref):
    o_ref[...] = jnp.dot(x_ref[...], w_ref[...], preferred_element_type=f32).astype(o_ref.dtype)


def _proj_main(xb, w_main, layer):
    n, k = xb.shape
    grid = (n // TM_MAIN, MAIN_COLS // TN_MAIN)
    return pl.pallas_call(
        _mm_kernel,
        grid=grid,
        in_specs=[
            pl.BlockSpec((TM_MAIN, k), lambda i, j: (i, 0)),
            pl.BlockSpec((None, None, k, TN_MAIN), lambda i, j: (layer, j, 0, 0)),
        ],
        out_specs=pl.BlockSpec((None, TM_MAIN, TN_MAIN), lambda i, j: (j, i, 0)),
        out_shape=jax.ShapeDtypeStruct((MAIN_COLS // TN_MAIN, n, TN_MAIN), bf16),
        compiler_params=_cparams(("parallel", "parallel")),
    )(xb, w_main)


def _proj_b_kernel(x_ref, w_ref, cos_ref, sin_ref, o_ref):
    j = pl.program_id(0)
    pair = 2 * HEAD_DIM

    def emit(rotary, scale):
        x = x_ref[...]
        if rotary:
            cos = cos_ref[...] * scale
            sin = sin_ref[...] * scale
        for c in range(TN_B // pair):
            acc = jnp.dot(x, w_ref[:, c * pair:(c + 1) * pair], preferred_element_type=f32)
            for h in range(2):
                a = acc[:, h * HEAD_DIM:(h + 1) * HEAD_DIM]
                if rotary:
                    a = a * cos + pltpu.roll(a, HEAD_DIM // 2, 1) * sin
                o_ref[2 * c + h] = a.astype(o_ref.dtype)

    @pl.when(j == 0)
    def _():
        emit(True, HEAD_DIM ** -0.5)

    @pl.when(j == 1)
    def _():
        emit(True, 1.0)

    @pl.when(j == 2)
    def _():
        emit(False, 1.0)


def _proj_b(xb, w_b, cos_b, sin_b, layer, t):
    n, k = xb.shape
    tpb = t // TM_PROJ
    grid = (B_QKV // TN_B, n // TM_PROJ)
    hpt = TN_B // HEAD_DIM
    return pl.pallas_call(
        _proj_b_kernel,
        grid=grid,
        in_specs=[
            pl.BlockSpec((TM_PROJ, k), lambda j, i: (i, 0)),
            pl.BlockSpec((None, None, k, TN_B), lambda j, i: (layer, j, 0, 0)),
            pl.BlockSpec((TM_PROJ, HEAD_DIM), lambda j, i: (i % tpb, 0)),
            pl.BlockSpec((TM_PROJ, HEAD_DIM), lambda j, i: (i % tpb, 0)),
        ],
        out_specs=pl.BlockSpec((hpt, TM_PROJ, HEAD_DIM), lambda j, i: (j, i, 0)),
        out_shape=jax.ShapeDtypeStruct((3 * B_HEADS, n, HEAD_DIM), bf16),
        compiler_params=_cparams(("parallel", "parallel")),
    )(xb, w_b, cos_b, sin_b)


def _gla_kernel(q_ref, k_ref, v_ref, r_ref, lr_ref, w2f_ref, bf_ref, w2b_ref, bb_ref, g_ref,
                o_ref, accf_ref, accb_ref, *, t):
    c_len = GLA_CHUNK
    sup = GLA_SUPER * c_len
    n_sup = t // sup
    row = lax.broadcasted_iota(jnp.int32, (sup, sup), 0)
    col = lax.broadcasted_iota(jnp.int32, (sup, sup), 1)
    same = (row // c_len) == (col // c_len)

    def stage_decay(base, fwd):
        w2 = (w2f_ref if fwd else w2b_ref)[...].astype(bf16)
        bias = (bf_ref if fwd else bb_ref)[...]
        lo = 0 if fwd else GLA_RANK
        lr = lr_ref[pl.ds(base, sup), :][:, lo:lo + GLA_RANK]
        z = jnp.dot(lr, w2, preferred_element_type=f32) + bias
        g = (jnp.minimum(z, 0.0) - jnp.log(1.0 + jnp.exp(-jnp.abs(z)))) * (1.0 / GLA_TAU)
        g_hi = g.astype(bf16)
        g_lo = (g - g_hi.astype(f32)).astype(bf16)
        return g_hi, g_lo

    def stage_cumsum(parts, fwd):
        g_hi, g_lo = parts
        tri = jnp.where(same & ((row >= col) if fwd else (row <= col)), 1.0, 0.0).astype(bf16)
        b = (jnp.dot(tri, g_hi, preferred_element_type=f32)
             + jnp.dot(tri, g_lo, preferred_element_type=f32))
        edge = c_len - 1 if fwd else 0
        tot = jnp.concatenate(
            [jnp.broadcast_to(b[c * c_len + edge:c * c_len + edge + 1, :], (c_len, GLA_DK))
             for c in range(GLA_SUPER)], axis=0)
        return b, tot

    def stage_scale(base, b, tot):
        sl = pl.ds(base, sup)
        q = q_ref[sl, :].astype(f32) * (GLA_DK ** -0.5)
        k = k_ref[sl, :].astype(f32)
        q_t = (q * jnp.exp(b)).astype(bf16)
        k_t = (k * jnp.exp(-b)).astype(bf16)
        k_s = (k * jnp.exp(tot - b)).astype(bf16)
        return q_t, k_t, k_s, jnp.exp(tot)

    def stage_intra(base, q_t, k_t, fwd):
        smask = same & ((col <= row) if fwd else (col > row))
        scores = jnp.where(smask, _nt_dot(q_t, k_t), 0.0)
        return jnp.dot(scores.astype(bf16), v_ref[pl.ds(base, sup), :], preferred_element_type=f32)

    def chunk_step(base, c, fwd, q_t, k_s, dec, o_intra, s_t):
        acc_ref = accf_ref if fwd else accb_ref
        rs = slice(c * c_len, (c + 1) * c_len)
        rows_c = pl.ds(base + c * c_len, c_len)
        acc_ref[rows_c, :] = o_intra[rs, :] + _nt_dot(q_t[rs, :], s_t.astype(bf16))
        return s_t * dec[c * c_len:c * c_len + 1, :] + _tn_dot(v_ref[rows_c, :], k_s[rs, :])

    def body(i, carry):
        s_f, s_b = carry
        base_f = pl.multiple_of(i * sup, sup)
        base_b = pl.multiple_of((n_sup - 1 - i) * sup, sup)
        parts_f = stage_decay(base_f, True)
        parts_b = stage_decay(base_b, False)
        b_f, tot_f = stage_cumsum(parts_f, True)
        b_b, tot_b = stage_cumsum(parts_b, False)
        qt_f, kt_f, ks_f, dec_f = stage_scale(base_f, b_f, tot_f)
        qt_b, kt_b, ks_b, dec_b = stage_scale(base_b, b_b, tot_b)
        oi_f = stage_intra(base_f, qt_f, kt_f, True)
        oi_b = stage_intra(base_b, qt_b, kt_b, False)
        for c in range(GLA_SUPER):
            s_f = chunk_step(base_f, c, True, qt_f, ks_f, dec_f, oi_f, s_f)
            s_b = chunk_step(base_b, GLA_SUPER - 1 - c, False, qt_b, ks_b, dec_b, oi_b, s_b)
        return s_f, s_b

    zero = jnp.zeros((GLA_DV, GLA_DK), f32)
    lax.fori_loop(0, n_sup, body, (zero, zero))

    rows = 256
    gain = g_ref[...]

    def finish(i, carry):
        sl = pl.ds(pl.multiple_of(i * rows, rows), rows)
        x = accf_ref[sl, :] + accb_ref[sl, :]
        ms = jnp.mean(x * x, axis=-1, keepdims=True)
        y = x * lax.rsqrt(ms + RMS_EPS) * gain
        r = r_ref[sl, :].astype(f32)
        o_ref[sl, :] = (y * (r * jax.nn.sigmoid(r))).astype(o_ref.dtype)
        return carry

    lax.fori_loop(0, t // rows, finish, 0)


def _gla(main, w2f, b_f, w2b, b_b, gain, layer, bsz, t):
    n = bsz * t
    blk = lambda off, width: (lambda b, h: (off // TN_MAIN, b, (off % TN_MAIN) // width + h))
    wspec = pl.BlockSpec((None, GLA_RANK, GLA_DK), lambda b, h: (layer, 0, h))
    bspec = pl.BlockSpec((None, 1, GLA_DK), lambda b, h: (layer, 0, h))
    return pl.pallas_call(
        functools.partial(_gla_kernel, t=t),
        grid=(bsz, GLA_HEADS),
        in_specs=[
            pl.BlockSpec((None, t, GLA_DK), blk(OFF_AQ, GLA_DK)),
            pl.BlockSpec((None, t, GLA_DK), blk(OFF_AK, GLA_DK)),
            pl.BlockSpec((None, t, GLA_DV), blk(OFF_AV, GLA_DV)),
            pl.BlockSpec((None, t, GLA_DV), blk(OFF_AR, GLA_DV)),
            pl.BlockSpec((None, t, LR_PAD),
                         lambda b, h: (OFF_LR // TN_MAIN, b, (OFF_LR % TN_MAIN) // LR_PAD)),
            wspec, bspec, wspec, bspec,
            pl.BlockSpec((None, 1, GLA_DV), lambda b, h: (layer, 0, h)),
        ],
        out_specs=pl.BlockSpec((t, GLA_DV), lambda b, h: (b, h)),
        out_shape=jax.ShapeDtypeStruct((n, A_V), bf16),
        scratch_shapes=[pltpu.VMEM((t, GLA_DV), f32), pltpu.VMEM((t, GLA_DV), f32)],
        compiler_params=_cparams(("parallel", "parallel")),
    )(main, main, main, main, main, w2f, b_f, w2b, b_b, gain)


def _dil_kernel(q_ref, k_ref, v_ref, o_ref, lse_ref, *scratch, t, dil):
    n_sub = t // dil
    bq = min(128, n_sub)
    win = min(bq + 2 * DIL_HALF, n_sub)
    n_blk = n_sub // bq
    qi = lax.broadcasted_iota(jnp.int32, (bq, win), 0)
    ki = lax.broadcasted_iota(jnp.int32, (bq, win), 1)
    cvt = 512

    if dil > 1:
        q32, k32, v32, o32 = scratch

        def widen(i, carry):
            sl = pl.ds(pl.multiple_of(i * cvt, cvt), cvt)
            q32[sl, :] = q_ref[0, sl, :].astype(f32)
            k32[sl, :] = k_ref[0, sl, :].astype(f32)
            v32[sl, :] = v_ref[0, sl, :].astype(f32)
            return carry

        lax.fori_loop(0, t // cvt, widen, 0)

    def body(idx, carry):
        r = idx // n_blk
        i = idx % n_blk
        q0 = i * bq
        k0 = jnp.clip(q0 - DIL_HALF, 0, n_sub - win)
        if dil == 1:
            qs = pl.ds(pl.multiple_of(q0, bq), bq)
            ks = pl.ds(pl.multiple_of(k0, DIL_HALF), win)
            q = q_ref[0, qs, :]
            k = k_ref[0, ks, :]
            v = v_ref[0, ks, :]
        else:
            qs = pl.ds(r + q0 * dil, bq, stride=dil)
            ks = pl.ds(r + k0 * dil, win, stride=dil)
            q = q32[qs, :].astype(bf16)
            k = k32[ks, :].astype(bf16)
            v = v32[ks, :].astype(bf16)
        s = _nt_dot(q, k)
        valid = jnp.abs((q0 + qi) - (k0 + ki)) <= DIL_HALF
        s = jnp.where(valid, s, NEG_BIG)
        m = jnp.max(s, axis=-1, keepdims=True)
        p = jnp.exp(s - m)
        l = jnp.sum(p, axis=-1, keepdims=True)
        o = jnp.dot(p.astype(bf16), v, preferred_element_type=f32) / l
        lse = jnp.broadcast_to(m + jnp.log(l), (bq, HEAD_DIM))
        if dil == 1:
            o_ref[0, qs, :] = o.astype(o_ref.dtype)
            lse_ref[0, qs, :] = lse
        else:
            o32[qs, :] = o
            lse_ref[0, qs, :] = lse
        return carry

    def body_group(j, carry):
        for u in range(DIL_UNROLL):
            body(j * DIL_UNROLL + u, carry)
        return carry

    lax.fori_loop(0, dil * n_blk // DIL_UNROLL, body_group, 0)

    if dil > 1:
        def narrow(i, carry):
            sl = pl.ds(pl.multiple_of(i * cvt, cvt), cvt)
            o_ref[0, sl, :] = o32[sl, :].astype(o_ref.dtype)
            return carry

        lax.fori_loop(0, t // cvt, narrow, 0)


def _dilated(qkv_b, group, bsz, t):
    _, dil = DIL_CONFIGS[group]
    n = bsz * t
    blk = (1, t, HEAD_DIM)
    head0 = group * DIL_HEADS
    scratch = [pltpu.VMEM((t, HEAD_DIM), f32)] * 4 if dil > 1 else []
    return pl.pallas_call(
        functools.partial(_dil_kernel, t=t, dil=dil),
        grid=(bsz, DIL_HEADS),
        in_specs=[
            pl.BlockSpec(blk, lambda b, h: (head0 + h, b, 0)),
            pl.BlockSpec(blk, lambda b, h: (B_HEADS + head0 + h, b, 0)),
            pl.BlockSpec(blk, lambda b, h: (2 * B_HEADS + head0 + h, b, 0)),
        ],
        out_specs=[
            pl.BlockSpec(blk, lambda b, h: (h, b, 0)),
            pl.BlockSpec(blk, lambda b, h: (h, b, 0)),
        ],
        out_shape=[
            jax.ShapeDtypeStruct((DIL_HEADS, n, HEAD_DIM), bf16),
            jax.ShapeDtypeStruct((DIL_HEADS, n, HEAD_DIM), f32),
        ],
        scratch_shapes=scratch,
        compiler_params=_cparams(("parallel", "parallel")),
    )(qkv_b, qkv_b, qkv_b)


def _c_prep_kernel(q_ref, k_ref, cos_ref, sin_ref, qg_ref, kg_ref, qo_ref, ko_ref):
    cos = cos_ref[...]
    sin = sin_ref[...]

    def prep(x, gain, scale):
        x = x.astype(f32)
        ms = jnp.mean(x * x, axis=-1, keepdims=True)
        y = x * lax.rsqrt(ms + RMS_EPS) * gain
        return (y * cos + _axial_partner(y) * sin) * scale

    qg = qg_ref[...]
    kg = kg_ref[...]
    for h in range(GQA_Q_HEADS):
        ls = slice(h * HEAD_DIM, (h + 1) * HEAD_DIM)
        qo_ref[:, ls] = prep(q_ref[:, ls], qg, HEAD_DIM ** -0.5 * LOG2_E).astype(qo_ref.dtype)
    for h in range(GQA_KV_HEADS):
        ls = slice(h * HEAD_DIM, (h + 1) * HEAD_DIM)
        ko_ref[:, ls] = prep(k_ref[:, ls], kg, 1.0).astype(ko_ref.dtype)


def _c_prep(main, cos_c, sin_c, q_gain, k_gain, layer, t):
    n = main.shape[1]
    tm = TM_PROJ
    tpb = t // tm
    gspec = pl.BlockSpec((None, 1, HEAD_DIM), lambda i: (layer, 0, 0))
    return pl.pallas_call(
        _c_prep_kernel,
        grid=(n // tm,),
        in_specs=[
            pl.BlockSpec((None, tm, C_Q), lambda i: (OFF_CQ // TN_MAIN, i, (OFF_CQ % TN_MAIN) // C_Q)),
            pl.BlockSpec((None, tm, C_KV), lambda i: (OFF_CK // TN_MAIN, i, (OFF_CK % TN_MAIN) // C_KV)),
            pl.BlockSpec((tm, HEAD_DIM), lambda i: (i % tpb, 0)),
            pl.BlockSpec((tm, HEAD_DIM), lambda i: (i % tpb, 0)),
            gspec, gspec,
        ],
        out_specs=[
            pl.BlockSpec((tm, C_Q), lambda i: (i, 0)),
            pl.BlockSpec((tm, C_KV), lambda i: (i, 0)),
        ],
        out_shape=[
            jax.ShapeDtypeStruct((n, C_Q), bf16),
            jax.ShapeDtypeStruct((n, C_KV), bf16),
        ],
        compiler_params=_cparams(("parallel",)),
    )(main, main, cos_c, sin_c, q_gain, k_gain)


def _gqa_kernel(q_ref, k_ref, v_ref, o_ref, s_scr, p_scr, m_scr, l_scr, acc_scr, *, t, tq, tk):
    grp = GQA_Q_HEADS // GQA_KV_HEADS
    q4 = q_ref[...]
    q = jnp.concatenate([q4[:, h * HEAD_DIM:(h + 1) * HEAD_DIM] for h in range(grp)], axis=0)
    rows = grp * tq
    n_strips = rows // GQA_STRIP
    m_scr[...] = jnp.full((rows, LANES), NEG_BIG, f32)
    l_scr[...] = jnp.zeros((rows, LANES), f32)
    acc_scr[...] = jnp.zeros((rows, HEAD_DIM), f32)

    def widen(col):
        return jnp.broadcast_to(col, (col.shape[0], LANES))

    def softmax_chunk(buf):
        mx = []
        for i in range(n_strips):
            rs = slice(i * GQA_STRIP, (i + 1) * GQA_STRIP)
            mx.append(widen(jnp.max(s_scr[buf, rs, :], axis=-1, keepdims=True)))
        m_old = m_scr[...]
        m_new = jnp.maximum(m_old, jnp.concatenate(mx, axis=0))
        a = jnp.exp2(m_old - m_new)
        m_scr[...] = m_new
        sums = []
        for i in range(n_strips):
            rs = slice(i * GQA_STRIP, (i + 1) * GQA_STRIP)
            m_wide = jnp.concatenate([m_new[rs, :]] * (tk // LANES), axis=1)
            p = jnp.exp2(s_scr[buf, rs, :] - m_wide)
            sums.append(widen(jnp.sum(p, axis=-1, keepdims=True)))
            p_scr[buf, rs, :] = p.astype(bf16)
        l_scr[...] = a * l_scr[...] + jnp.concatenate(sums, axis=0)
        return a

    def chunk_group(cg, carry):
        sls = []
        for u in range(GQA_GROUP):
            sl = pl.ds(pl.multiple_of((cg * GQA_GROUP + u) * tk, tk), tk)
            sls.append(sl)
            s_scr[u] = _nt_dot(q, k_ref[sl, :])
        for u in range(GQA_GROUP):
            a = softmax_chunk(u)
            acc_scr[...] = acc_scr[...] * a + jnp.dot(p_scr[u], v_ref[sls[u], :],
                                                      preferred_element_type=f32)
        return carry

    lax.fori_loop(0, t // (tk * GQA_GROUP), chunk_group, 0)
    o = (acc_scr[...] / l_scr[...]).astype(o_ref.dtype)
    o_ref[...] = jnp.concatenate([o[h * tq:(h + 1) * tq, :] for h in range(grp)], axis=1)


def _gqa(qc, kc, main, bsz, t):
    n = bsz * t
    tq = GQA_TQ
    tk = min(t, GQA_TK)
    grp = GQA_Q_HEADS // GQA_KV_HEADS
    grp_cols = grp * HEAD_DIM
    rows = grp * tq
    nq = t // tq
    return pl.pallas_call(
        functools.partial(_gqa_kernel, t=t, tq=tq, tk=tk),
        grid=(bsz, GQA_KV_HEADS, nq),
        in_specs=[
            pl.BlockSpec((tq, grp_cols), lambda b, j, i: (b * nq + i, j)),
            pl.BlockSpec((t, HEAD_DIM), lambda b, j, i: (b, j)),
            pl.BlockSpec((None, t, HEAD_DIM),
                         lambda b, j, i: (OFF_CV // TN_MAIN, b, (OFF_CV % TN_MAIN) // HEAD_DIM + j)),
        ],
        out_specs=pl.BlockSpec((tq, grp_cols), lambda b, j, i: (b * nq + i, j)),
        out_shape=jax.ShapeDtypeStruct((n, C_Q), bf16),
        scratch_shapes=[
            pltpu.VMEM((GQA_GROUP, rows, tk), f32),
            pltpu.VMEM((GQA_GROUP, rows, tk), bf16),
            pltpu.VMEM((rows, LANES), f32),
            pltpu.VMEM((rows, LANES), f32),
            pltpu.VMEM((rows, HEAD_DIM), f32),
        ],
        compiler_params=_cparams(("parallel", "parallel", "parallel")),
    )(qc, kc, main)


def _merge_kernel(oa_ref, ob0_ref, ob1_ref, ob2_ref, l0_ref, l1_ref, l2_ref, oc_ref,
                  ga_ref, gb_ref, gc_ref, wa_ref, wb_ref, wc_ref, o_ref):
    heads = []
    for h in range(DIL_HEADS):
        l0, l1, l2 = l0_ref[h], l1_ref[h], l2_ref[h]
        m = jnp.maximum(jnp.maximum(l0, l1), l2)
        e0, e1, e2 = jnp.exp(l0 - m), jnp.exp(l1 - m), jnp.exp(l2 - m)
        mix = (e0 * ob0_ref[h].astype(f32) + e1 * ob1_ref[h].astype(f32)
               + e2 * ob2_ref[h].astype(f32)) / (e0 + e1 + e2)
        heads.append(mix.astype(bf16))
    o_b = jnp.concatenate(heads, axis=1)
    up_a = jnp.dot(oa_ref[...], wa_ref[...], preferred_element_type=f32)
    up_b = jnp.dot(o_b, wb_ref[...], preferred_element_type=f32)
    up_c = jnp.dot(oc_ref[...], wc_ref[...], preferred_element_type=f32)
    def gate(ref):
        logits = jnp.concatenate([ref[c] for c in range(ref.shape[0])], axis=1)
        return jax.nn.sigmoid(logits.astype(f32))

    merged = gate(ga_ref) * up_a + gate(gb_ref) * up_b + gate(gc_ref) * up_c
    o_ref[...] = merged.astype(o_ref.dtype)


def _merge(o_a, o_bs, lses, o_c, main, w_up_a, w_up_b, w_up_c, layer):
    n = o_a.shape[0]
    tm = TM_TOK
    tiles_per_gate = D_MODEL // TN_MAIN
    hspec = pl.BlockSpec((DIL_HEADS, tm, HEAD_DIM), lambda i: (0, i, 0))
    gspec = lambda which: pl.BlockSpec((tiles_per_gate, tm, TN_MAIN),
                                       lambda i: (OFF_GATE // D_MODEL + which, i, 0))
    wspec = lambda rows: pl.BlockSpec((None, rows, D_MODEL), lambda i: (layer, 0, 0))
    return pl.pallas_call(
        _merge_kernel,
        grid=(n // tm,),
        in_specs=[
            pl.BlockSpec((tm, A_V), lambda i: (i, 0)),
            hspec, hspec, hspec, hspec, hspec, hspec,
            pl.BlockSpec((tm, C_Q), lambda i: (i, 0)),
            gspec(0), gspec(1), gspec(2),
            wspec(A_V), wspec(B_OUT), wspec(C_Q),
        ],
        out_specs=pl.BlockSpec((tm, D_MODEL), lambda i: (i, 0)),
        out_shape=jax.ShapeDtypeStruct((n, D_MODEL), bf16),
        compiler_params=_cparams(("parallel",)),
    )(o_a, o_bs[0], o_bs[1], o_bs[2], lses[0], lses[1], lses[2], o_c,
      main, main, main, w_up_a, w_up_b, w_up_c)


def _layer_norm_rows(y, g, b):
    mu = jnp.mean(y, axis=-1, keepdims=True)
    d = y - mu
    var = jnp.mean(d * d, axis=-1, keepdims=True)
    return d * lax.rsqrt(var + LN_EPS) * g + b


def _route_rows(logits_t, bias_col):
    scores = jax.nn.sigmoid(logits_t)
    sel = scores + bias_col
    rows = [sel[e:e + 1, :] for e in range(N_EXPERTS)]
    srow = [scores[e:e + 1, :] for e in range(N_EXPERTS)]
    best_val = None
    best_grp = None
    for g in range(N_EXPERT_GROUPS):
        mem = rows[g * EXPERTS_PER_GROUP:(g + 1) * EXPERTS_PER_GROUP]
        top2 = None
        for a in range(EXPERTS_PER_GROUP):
            for b in range(a + 1, EXPERTS_PER_GROUP):
                pair = mem[a] + mem[b]
                top2 = pair if top2 is None else jnp.maximum(top2, pair)
        if g == 0:
            best_val, best_grp = top2, jnp.zeros_like(top2, dtype=jnp.int32)
        else:
            upd = top2 > best_val
            best_val = jnp.where(upd, top2, best_val)
            best_grp = jnp.where(upd, g, best_grp)
    neg = jnp.full_like(best_val, -jnp.inf)
    cand = [jnp.where(best_grp == (e // EXPERTS_PER_GROUP), rows[e], neg) for e in range(N_EXPERTS)]

    def arg_top(vals):
        bv, bi = vals[0], jnp.zeros_like(best_grp)
        for e in range(1, N_EXPERTS):
            upd = vals[e] > bv
            bv = jnp.where(upd, vals[e], bv)
            bi = jnp.where(upd, e, bi)
        return bi

    idx1 = arg_top(cand)
    idx2 = arg_top([jnp.where(idx1 == e, neg, cand[e]) for e in range(N_EXPERTS)])
    zero = jnp.zeros_like(best_val)
    s1 = zero
    s2 = zero
    for e in range(N_EXPERTS):
        s1 = s1 + jnp.where(idx1 == e, srow[e], zero)
        s2 = s2 + jnp.where(idx2 == e, srow[e], zero)
    tot = s1 + s2
    return idx1, idx2, s1 / tot, s2 / tot


def _outproj_kernel(m_ref, w_ref, x_ref, g_ref, b_ref, wr_ref, x1_ref, x1b_ref, lg_ref):
    h = jnp.dot(m_ref[...], w_ref[...], preferred_element_type=f32)
    x1 = _layer_norm_rows(DEEPNORM_ALPHA * x_ref[...] + h, g_ref[...], b_ref[...])
    x1_ref[...] = x1
    x_hi = x1.astype(bf16)
    x1b_ref[...] = x_hi
    x_lo = (x1 - x_hi.astype(f32)).astype(bf16)
    wr = wr_ref[...]
    w_hi = wr.astype(bf16)
    w_lo = (wr - w_hi.astype(f32)).astype(bf16)
    lg_ref[...] = (jnp.dot(x_hi, w_hi, preferred_element_type=f32)
                   + jnp.dot(x_hi, w_lo, preferred_element_type=f32)
                   + jnp.dot(x_lo, w_hi, preferred_element_type=f32))


def _outproj_ln(merged, w_out, x, ln_g, ln_b, w_router_pad, layer):
    n = x.shape[0]
    tm = TM_TOK
    vspec = pl.BlockSpec((None, 1, D_MODEL), lambda i: (layer, 0, 0))
    row = pl.BlockSpec((tm, D_MODEL), lambda i: (i, 0))
    return pl.pallas_call(
        _outproj_kernel,
        grid=(n // tm,),
        in_specs=[
            row,
            pl.BlockSpec((None, D_MODEL, D_MODEL), lambda i: (layer, 0, 0)),
            row,
            vspec, vspec,
            pl.BlockSpec((D_MODEL, LANES), lambda i: (0, 0)),
        ],
        out_specs=[row, row, pl.BlockSpec((tm, LANES), lambda i: (i, 0))],
        out_shape=[
            jax.ShapeDtypeStruct((n, D_MODEL), f32),
            jax.ShapeDtypeStruct((n, D_MODEL), bf16),
            jax.ShapeDtypeStruct((n, LANES), f32),
        ],
        compiler_params=_cparams(("parallel",)),
    )(merged, w_out, x, ln_g, ln_b, w_router_pad)


def _route_kernel(lg_ref, rb_ref, idx_ref, w1_ref, w2_ref):
    logits_t = lg_ref[...].T[0:N_EXPERTS, :]
    i1, i2, w1, w2 = _route_rows(logits_t, rb_ref[...])
    idx_ref[0:1, :] = i1
    idx_ref[1:2, :] = i2
    tr = logits_t.shape[1]
    w1_ref[...] = jnp.broadcast_to(w1, (LANES, tr)).T
    w2_ref[...] = jnp.broadcast_to(w2, (LANES, tr)).T


def _route(logits, router_bias_col):
    n = logits.shape[0]
    tr = TM_ROUTE
    return pl.pallas_call(
        _route_kernel,
        grid=(n // tr,),
        in_specs=[
            pl.BlockSpec((tr, LANES), lambda i: (i, 0)),
            pl.BlockSpec((N_EXPERTS, 1), lambda i: (0, 0)),
        ],
        out_specs=[
            pl.BlockSpec((2, tr), lambda i: (0, i)),
            pl.BlockSpec((tr, LANES), lambda i: (i, 0)),
            pl.BlockSpec((tr, LANES), lambda i: (i, 0)),
        ],
        out_shape=[
            jax.ShapeDtypeStruct((2, n), jnp.int32),
            jax.ShapeDtypeStruct((n, LANES), f32),
            jax.ShapeDtypeStruct((n, LANES), f32),
        ],
        compiler_params=_cparams(("parallel",)),
    )(logits, router_bias_col)


def _moe_kernel(te_ref, nt_ref, *refs, tiles_per_piece):
    del te_ref
    x_refs, (wg_ref, wu_ref, wd_ref, o_ref) = refs[:-4], refs[-4:]
    tile = pl.program_id(0)

    @pl.when(tile < nt_ref[0])
    def _():
        x = x_refs[0][...]
        for p in range(1, len(x_refs)):
            x = jnp.where(tile >= p * tiles_per_piece, x_refs[p][...], x)
        gate = jnp.dot(x, wg_ref[...], preferred_element_type=f32)
        up = jnp.dot(x, wu_ref[...], preferred_element_type=f32)
        h = (gate * jax.nn.sigmoid(gate) * up).astype(bf16)
        o_ref[...] = jnp.dot(h, wd_ref[...], preferred_element_type=f32).astype(o_ref.dtype)

    @pl.when(tile >= nt_ref[0])
    def _():
        o_ref[...] = jnp.zeros_like(o_ref)


def _moe_grouped(x_pieces, tile_expert, n_tiles_used, w_gate, w_up, w_down, layer):
    tpp = x_pieces[0].shape[0] // TM_MOE
    n_tiles = tpp * len(x_pieces)
    wspec_in = pl.BlockSpec((None, None, D_MODEL, D_FF_EXPERT), lambda t, te, nt: (layer, te[t], 0, 0))
    wspec_out = pl.BlockSpec((None, None, D_FF_EXPERT, D_MODEL), lambda t, te, nt: (layer, te[t], 0, 0))

    def piece_spec(p):
        return pl.BlockSpec((TM_MOE, D_MODEL),
                            lambda t, te, nt: (jnp.clip(t - p * tpp, 0, tpp - 1), 0))

    grid_spec = pltpu.PrefetchScalarGridSpec(
        num_scalar_prefetch=2,
        grid=(n_tiles,),
        in_specs=[piece_spec(p) for p in range(len(x_pieces))] + [wspec_in, wspec_in, wspec_out],
        out_specs=pl.BlockSpec((TM_MOE, D_MODEL), lambda t, te, nt: (t, 0)),
    )
    return pl.pallas_call(
        functools.partial(_moe_kernel, tiles_per_piece=tpp),
        grid_spec=grid_spec,
        out_shape=jax.ShapeDtypeStruct((n_tiles * TM_MOE, D_MODEL), bf16),
        compiler_params=_cparams(("arbitrary",)),
    )(tile_expert, n_tiles_used, *x_pieces, w_gate, w_up, w_down)


def _combine_kernel(x_ref, *refs, n_pieces, tiles_per_piece):
    y1_refs, y2_refs = refs[:n_pieces], refs[n_pieces:2 * n_pieces]
    w1_ref, w2_ref, g_ref, b_ref, o_ref, ob_ref = refs[2 * n_pieces:]
    tile = pl.program_id(0)

    def pick(piece_refs):
        y = piece_refs[0][...]
        for p in range(1, n_pieces):
            y = jnp.where(tile >= p * tiles_per_piece, piece_refs[p][...], y)
        return y

    y1 = pick(y1_refs)
    y2 = pick(y2_refs)
    w1 = w1_ref[...]
    w2 = w2_ref[...]
    for c in range(D_MODEL // LANES):
        ls = slice(c * LANES, (c + 1) * LANES)
        o_ref[:, ls] = (DEEPNORM_ALPHA * x_ref[:, ls] + w1 * y1[:, ls].astype(f32)
                        + w2 * y2[:, ls].astype(f32))
    x2 = _layer_norm_rows(o_ref[...], g_ref[...], b_ref[...])
    o_ref[...] = x2
    ob_ref[...] = x2.astype(bf16)


def _combine_ln(x1, y1_pieces, y2_pieces, w1b, w2b, ln_g, ln_b, layer):
    n = x1.shape[0]
    tm = TM_TOK
    n_pieces = len(y1_pieces)
    tpp = y1_pieces[0].shape[0] // tm
    row = pl.BlockSpec((tm, D_MODEL), lambda i: (i, 0))
    wsp = pl.BlockSpec((tm, LANES), lambda i: (i, 0))
    vspec = pl.BlockSpec((None, 1, D_MODEL), lambda i: (layer, 0, 0))

    def piece_spec(p):
        return pl.BlockSpec((tm, D_MODEL), lambda i: (jnp.clip(i - p * tpp, 0, tpp - 1), 0))

    pieces = [piece_spec(p) for p in range(n_pieces)]
    return pl.pallas_call(
        functools.partial(_combine_kernel, n_pieces=n_pieces, tiles_per_piece=tpp),
        grid=(n // tm,),
        in_specs=[row] + pieces + pieces + [wsp, wsp, vspec, vspec],
        out_specs=[row, row],
        out_shape=[jax.ShapeDtypeStruct((n, D_MODEL), f32), jax.ShapeDtypeStruct((n, D_MODEL), bf16)],
        compiler_params=_cparams(("parallel",)),
    )(x1, *y1_pieces, *y2_pieces, w1b, w2b, ln_g, ln_b)


def _dispatch_plan(idx):
    n = idx.shape[1]
    m_pad = 2 * n + N_EXPERTS * TM_MOE
    n_tiles = m_pad // TM_MOE
    e_flat = idx.reshape(-1)
    onehot = (e_flat[:, None] == jnp.arange(N_EXPERTS, dtype=jnp.int32)[None, :]).astype(jnp.int32)
    rank = jnp.sum((jnp.cumsum(onehot, axis=0) - onehot) * onehot, axis=1)
    counts = jnp.sum(onehot, axis=0)
    tiles_per = (counts + TM_MOE - 1) // TM_MOE
    tile_end = jnp.cumsum(tiles_per)
    start_row = (tile_end - tiles_per) * TM_MOE
    dest = jnp.sum(onehot * start_row[None, :], axis=1) + rank
    tok = jnp.tile(jnp.arange(n, dtype=jnp.int32), 2)
    src = jnp.zeros((m_pad,), jnp.int32).at[dest].set(tok, mode="promise_in_bounds", unique_indices=True)
    n_used = tile_end[-1]
    tile_ids = jnp.minimum(jnp.arange(n_tiles, dtype=jnp.int32), n_used - 1)
    tile_expert = jnp.sum((tile_ids[:, None] >= tile_end[None, :]).astype(jnp.int32), axis=1)
    tile_expert = jnp.minimum(tile_expert, N_EXPERTS - 1)
    return src, dest, tile_expert, n_used.reshape(1).astype(jnp.int32)


def _rope_tables(pos, dim):
    inv_freq = ROPE_THETA ** (-jnp.arange(0, dim, 2, dtype=f32) / dim)
    ang = pos.astype(f32)[:, None] * inv_freq[None, :]
    return jnp.cos(ang), jnp.sin(ang)


def _tables(t):
    cos, sin = _rope_tables(jnp.arange(t), HEAD_DIM)
    cos_b = jnp.concatenate([cos, cos], axis=-1)
    sin_b = jnp.concatenate([-sin, sin], axis=-1)
    tok = jnp.arange(t)
    cr, sr = _rope_tables(tok // GRID_W, HEAD_DIM // 2)
    cc, sc = _rope_tables(tok % GRID_W, HEAD_DIM // 2)
    cos_c = jnp.concatenate([cr, cr, cc, cc], axis=-1)
    sin_c = jnp.concatenate([-sr, sr, -sc, sc], axis=-1)
    return cos_b, sin_b, cos_c, sin_c


def _repack_w_in(w_in):
    offs = [0]
    for s in IN_SPLITS:
        offs.append(offs[-1] + s)
    seg = [w_in[:, :, offs[i]:offs[i + 1]] for i in range(len(IN_SPLITS))]
    a_q, a_k, a_v, a_r, a_lr, b_qkv, c_q, c_k, c_v, gate = seg
    lr_pad = jnp.pad(a_lr, ((0, 0), (0, 0), (0, LR_PAD - A_LR)))
    w_main = jnp.concatenate([gate, a_q, a_k, a_v, a_r, c_q, c_k, c_v, lr_pad], axis=-1)
    w_main = jnp.pad(w_main, ((0, 0), (0, 0), (0, MAIN_COLS - MAIN_USED))).astype(bf16)
    return _tile_major(w_main, TN_MAIN), _tile_major(b_qkv.astype(bf16), TN_B)


def _tile_major(w, tn):
    n_layers, k, cols = w.shape
    return w.reshape(n_layers, k, cols // tn, tn).transpose(0, 2, 1, 3)


def _mixer(xb, wts, layer, bsz, t, tables):
    cos_b, sin_b, cos_c, sin_c = tables
    main = _proj_main(xb, wts["w_main"], layer)
    qkv_b = _proj_b(xb, wts["w_b"], cos_b, sin_b, layer, t)
    o_a = _gla(main, wts["gla_w2_f"], wts["gla_b_f"], wts["gla_w2_b"], wts["gla_b_b"],
               wts["gla_norm_g"], layer, bsz, t)
    o_bs, lses = [], []
    for group in range(len(DIL_CONFIGS)):
        o_g, lse_g = _dilated(qkv_b, group, bsz, t)
        o_bs.append(o_g)
        lses.append(lse_g)
    qc, kc = _c_prep(main, cos_c, sin_c, wts["q_norm_g"], wts["k_norm_g"], layer, t)
    o_c = _gqa(qc, kc, main, bsz, t)
    return _merge(o_a, o_bs, lses, o_c, main, wts["w_up_a"], wts["w_up_b"], wts["w_up_c"], layer)


def _gather_rows(a, rows):
    return a.at[rows].get(mode="promise_in_bounds")


class _Trunk:
    def __init__(self, x3, wts):
        self.bsz, self.t, _ = x3.shape
        self.wts = wts
        self.x = x3.reshape(self.bsz * self.t, D_MODEL)
        self.xb = self.x.astype(bf16)
        self.tables = _tables(self.t)

    def mix_and_route(self, layer):
        wts = self.wts
        merged = _mixer(self.xb, wts, layer, self.bsz, self.t, self.tables)
        self.x1, x1b, logits = _outproj_ln(merged, wts["w_out"], self.x, wts["ln1_g"], wts["ln1_b"],
                                           wts["w_router_pad"], layer)
        idx, self.w1b, self.w2b = _route(logits, wts["router_bias_col"])
        src, self.dest, self.tile_expert, self.n_used = _dispatch_plan(idx)
        n_tiles = src.shape[0] // TM_MOE
        max_rows = min(MOE_GATHER_ROWS, x1b.shape[0])
        n_pieces = next(d for d in range(1, n_tiles + 1)
                        if n_tiles % d == 0 and src.shape[0] // d <= max_rows)
        copies = -(-GATHER_TABLE_ROWS // x1b.shape[0])
        table = x1b if copies == 1 else jnp.concatenate([x1b] * copies, axis=0)
        self.x_pieces = [_gather_rows(table, piece) for piece in jnp.split(src, n_pieces)]

    def experts(self, layer):
        wts = self.wts
        n = self.x1.shape[0]
        y_sorted = _moe_grouped(self.x_pieces, self.tile_expert, self.n_used, wts["moe_w_gate"],
                                wts["moe_w_up"], wts["moe_w_down"], layer)
        c_pieces = -(-n // COMBINE_GATHER_ROWS)
        self.y1 = [_gather_rows(y_sorted, piece) for piece in jnp.split(self.dest[:n], c_pieces)]
        self.y2 = [_gather_rows(y_sorted, piece) for piece in jnp.split(self.dest[n:], c_pieces)]

    def combine(self, layer):
        self.x, self.xb = _combine_ln(self.x1, self.y1, self.y2, self.w1b, self.w2b,
                                      self.wts["ln2_g"], self.wts["ln2_b"], layer)

    def result(self):
        return self.x.reshape(self.bsz, self.t, D_MODEL)


def _run_trunks(inputs, wts, depth=DEPTH):
    trunks = [_Trunk(x3, wts) for x3 in inputs]
    for layer in range(depth):
        for stage in (_Trunk.mix_and_route, _Trunk.experts, _Trunk.combine):
            for trunk in trunks:
                stage(trunk, layer)
    return tuple(trunk.result() for trunk in trunks)


def _cast_kernel(x_ref, o_ref):
    o_ref[...] = x_ref[...].astype(o_ref.dtype)


def _expert_weights_bf16(w):
    n_layers, n_exp, rows, cols = w.shape
    spec = pl.BlockSpec((None, None, rows, cols), lambda i, j: (i, j, 0, 0))
    return pl.pallas_call(
        _cast_kernel,
        grid=(n_layers, n_exp),
        in_specs=[spec],
        out_specs=spec,
        out_shape=jax.ShapeDtypeStruct(w.shape, bf16),
        compiler_params=_cparams(("parallel", "parallel")),
    )(w)


def _prepare_weights(w_in, gla_w2_f, gla_b_f, gla_w2_b, gla_b_b, gla_norm_g, q_norm_g, k_norm_g,
                     w_up_a, w_up_b, w_up_c, w_out, ln1_g, ln1_b, w_router, router_bias,
                     moe_w_gate, moe_w_up, moe_w_down, ln2_g, ln2_b):
    w_main, w_b = _repack_w_in(w_in)
    row = lambda a: a.reshape(DEPTH, 1, a.shape[-1]).astype(f32)
    return {
        "w_main": w_main,
        "w_b": w_b,
        "gla_w2_f": gla_w2_f.astype(f32),
        "gla_b_f": row(gla_b_f),
        "gla_w2_b": gla_w2_b.astype(f32),
        "gla_b_b": row(gla_b_b),
        "gla_norm_g": row(gla_norm_g),
        "q_norm_g": row(q_norm_g),
        "k_norm_g": row(k_norm_g),
        "w_up_a": w_up_a.astype(bf16),
        "w_up_b": w_up_b.astype(bf16),
        "w_up_c": w_up_c.astype(bf16),
        "w_out": w_out.astype(bf16),
        "ln1_g": row(ln1_g),
        "ln1_b": row(ln1_b),
        "w_router_pad": jnp.pad(w_router.astype(f32), ((0, 0), (0, LANES - N_EXPERTS))),
        "router_bias_col": router_bias.astype(f32).reshape(N_EXPERTS, 1),
        "moe_w_gate": _expert_weights_bf16(moe_w_gate.astype(f32)),
        "moe_w_up": _expert_weights_bf16(moe_w_up.astype(f32)),
        "moe_w_down": _expert_weights_bf16(moe_w_down.astype(f32)),
        "ln2_g": row(ln2_g),
        "ln2_b": row(ln2_b),
    }


def kernel(x_prompt, x_sample, w_in, gla_w2_f, gla_b_f, gla_w2_b, gla_b_b, gla_norm_g, q_norm_g, k_norm_g, w_up_a, w_up_b, w_up_c, w_out, ln1_g, ln1_b, w_router, router_bias, moe_w_gate, moe_w_up, moe_w_down, ln2_g, ln2_b):
    wts = _prepare_weights(w_in, gla_w2_f, gla_b_f, gla_w2_b, gla_b_b, gla_norm_g, q_norm_g, k_norm_g,
                           w_up_a, w_up_b, w_up_c, w_out, ln1_g, ln1_b, w_router, router_bias,
                           moe_w_gate, moe_w_up, moe_w_down, ln2_g, ln2_b)
    return _run_trunks((x_prompt, x_sample), wts)
```

```python
import functools

import jax
import jax.numpy as jnp
from jax import lax
from jax.experimental import pallas as pl
from jax.experimental.pallas import tpu as pltpu

D_MODEL = 2048
DEPTH = 4
HEAD_DIM = 128
GRID_W = 64
ROPE_THETA = 10000.0
LN_EPS = 1e-5
RMS_EPS = 1e-6
GLA_HEADS = 4
GLA_DK = 128
GLA_DV = 256
GLA_RANK = 16
GLA_TAU = 16.0
GLA_CHUNK = 64
DIL_CONFIGS = ((128, 1), (512, 4), (2048, 16))
DIL_HEADS = 4
GQA_Q_HEADS = 8
GQA_KV_HEADS = 2
N_EXPERTS = 16
N_EXPERT_GROUPS = 4
EXPERTS_PER_GROUP = N_EXPERTS // N_EXPERT_GROUPS
D_FF_EXPERT = 1024
DEEPNORM_ALPHA = (2.0 * DEPTH) ** 0.25

A_QK = GLA_HEADS * GLA_DK
A_V = GLA_HEADS * GLA_DV
A_LR = 2 * GLA_RANK
B_HEADS = len(DIL_CONFIGS) * DIL_HEADS
B_QKV = 3 * B_HEADS * HEAD_DIM
B_OUT = DIL_HEADS * HEAD_DIM
C_Q = GQA_Q_HEADS * HEAD_DIM
C_KV = GQA_KV_HEADS * HEAD_DIM
GATE_COLS = 3 * D_MODEL
IN_SPLITS = (A_QK, A_QK, A_V, A_V, A_LR, B_QKV, C_Q, C_KV, C_KV, GATE_COLS)

LANES = 128
VMEM_LIMIT = 56 * 1024 * 1024
LR_PAD = LANES

OFF_GATE = 0
OFF_AQ = OFF_GATE + GATE_COLS
OFF_AK = OFF_AQ + A_QK
OFF_AV = OFF_AK + A_QK
OFF_AR = OFF_AV + A_V
OFF_CQ = OFF_AR + A_V
OFF_CK = OFF_CQ + C_Q
OFF_CV = OFF_CK + C_KV
OFF_LR = OFF_CV + C_KV
MAIN_USED = OFF_LR + LR_PAD

TM_PROJ = 1024
TN_MAIN = 1024
TM_ROUTE = 2048
MAIN_COLS = -(-MAIN_USED // TN_MAIN) * TN_MAIN
TN_B = B_HEADS * HEAD_DIM
TM_TOK = 256
TM_MOE = 512
MOE_GATHER_ROWS = 22528
COMBINE_GATHER_ROWS = 16384
GATHER_TABLE_ROWS = 32768
TM_MAIN = 2048
GQA_TQ = 512
GQA_TK = 1024
GQA_GROUP = 2
GQA_STRIP = 32
LOG2_E = 1.4426950408889634
GLA_SUPER = 8
DIL_UNROLL = 8
DIL_HALF = 64
NEG_BIG = -1e30

f32 = jnp.float32
bf16 = jnp.bfloat16


def _cparams(sem):
    return pltpu.CompilerParams(dimension_semantics=("arbitrary",) * len(sem),
                                vmem_limit_bytes=VMEM_LIMIT)


def _nt_dot(a, b):
    return lax.dot_general(a, b, (((1,), (1,)), ((), ())), preferred_element_type=f32)


def _tn_dot(a, b):
    return lax.dot_general(a, b, (((0,), (0,)), ((), ())), preferred_element_type=f32)


def _axial_partner(x):
    lane = lax.broadcasted_iota(jnp.int32, x.shape, 1)
    quarter = HEAD_DIM // 4
    first = (lane % (2 * quarter)) < quarter
    return jnp.where(first, pltpu.roll(x, HEAD_DIM - quarter, 1), pltpu.roll(x, quarter, 1))


def _mm_kernel(x_ref, w_ref, o_ref):
    o_ref[...] = jnp.dot(x_ref[...], w_ref[...], preferred_element_type=f32).astype(o_ref.dtype)


def _proj_main(xb, w_main, layer):
    n, k = xb.shape
    grid = (n // TM_MAIN, MAIN_COLS // TN_MAIN)
    return pl.pallas_call(
        _mm_kernel,
        grid=grid,
        in_specs=[
            pl.BlockSpec((TM_MAIN, k), lambda i, j: (i, 0)),
            pl.BlockSpec((None, None, k, TN_MAIN), lambda i, j: (layer, j, 0, 0)),
        ],
        out_specs=pl.BlockSpec((None, TM_MAIN, TN_MAIN), lambda i, j: (j, i, 0)),
        out_shape=jax.ShapeDtypeStruct((MAIN_COLS // TN_MAIN, n, TN_MAIN), bf16),
        compiler_params=_cparams(("parallel", "parallel")),
    )(xb, w_main)


def _proj_b_kernel(x_ref, w_ref, cos_ref, sin_ref, o_ref):
    j = pl.program_id(0)
    pair = 2 * HEAD_DIM

    def emit(rotary, scale):
        x = x_ref[...]
        if rotary:
            cos = cos_ref[...] * scale
            sin = sin_ref[...] * scale
        for c in range(TN_B // pair):
            acc = jnp.dot(x, w_ref[:, c * pair:(c + 1) * pair], preferred_element_type=f32)
            for h in range(2):
                a = acc[:, h * HEAD_DIM:(h + 1) * HEAD_DIM]
                if rotary:
                    a = a * cos + pltpu.roll(a, HEAD_DIM // 2, 1) * sin
                o_ref[2 * c + h] = a.astype(o_ref.dtype)

    @pl.when(j == 0)
    def _():
        emit(True, HEAD_DIM ** -0.5)

    @pl.when(j == 1)
    def _():
        emit(True, 1.0)

    @pl.when(j == 2)
    def _():
        emit(False, 1.0)


def _proj_b(xb, w_b, cos_b, sin_b, layer, t):
    n, k = xb.shape
    tpb = t // TM_PROJ
    grid = (B_QKV // TN_B, n // TM_PROJ)
    hpt = TN_B // HEAD_DIM
    return pl.pallas_call(
        _proj_b_kernel,
        grid=grid,
        in_specs=[
            pl.BlockSpec((TM_PROJ, k), lambda j, i: (i, 0)),
            pl.BlockSpec((None, None, k, TN_B), lambda j, i: (layer, j, 0, 0)),
            pl.BlockSpec((TM_PROJ, HEAD_DIM), lambda j, i: (i % tpb, 0)),
            pl.BlockSpec((TM_PROJ, HEAD_DIM), lambda j, i: (i % tpb, 0)),
        ],
        out_specs=pl.BlockSpec((hpt, TM_PROJ, HEAD_DIM), lambda j, i: (j, i, 0)),
        out_shape=jax.ShapeDtypeStruct((3 * B_HEADS, n, HEAD_DIM), bf16),
        compiler_params=_cparams(("parallel", "parallel")),
    )(xb, w_b, cos_b, sin_b)


def _gla_kernel(q_ref, k_ref, v_ref, r_ref, lr_ref, w2f_ref, bf_ref, w2b_ref, bb_ref, g_ref,
                o_ref, accf_ref, accb_ref, *, t):
    c_len = GLA_CHUNK
    sup = GLA_SUPER * c_len
    n_sup = t // sup
    row = lax.broadcasted_iota(jnp.int32, (sup, sup), 0)
    col = lax.broadcasted_iota(jnp.int32, (sup, sup), 1)
    same = (row // c_len) == (col // c_len)

    def stage_decay(base, fwd):
        w2 = (w2f_ref if fwd else w2b_ref)[...].astype(bf16)
        bias = (bf_ref if fwd else bb_ref)[...]
        lo = 0 if fwd else GLA_RANK
        lr = lr_ref[pl.ds(base, sup), :][:, lo:lo + GLA_RANK]
        z = jnp.dot(lr, w2, preferred_element_type=f32) + bias
        g = (jnp.minimum(z, 0.0) - jnp.log(1.0 + jnp.exp(-jnp.abs(z)))) * (1.0 / GLA_TAU)
        g_hi = g.astype(bf16)
        g_lo = (g - g_hi.astype(f32)).astype(bf16)
        return g_hi, g_lo

    def stage_cumsum(parts, fwd):
        g_hi, g_lo = parts
        tri = jnp.where(same & ((row >= col) if fwd else (row <= col)), 1.0, 0.0).astype(bf16)
        b = (jnp.dot(tri, g_hi, preferred_element_type=f32)
             + jnp.dot(tri, g_lo, preferred_element_type=f32))
        edge = c_len - 1 if fwd else 0
        tot = jnp.concatenate(
            [jnp.broadcast_to(b[c * c_len + edge:c * c_len + edge + 1, :], (c_len, GLA_DK))
             for c in range(GLA_SUPER)], axis=0)
        return b, tot

    def stage_scale(base, b, tot):
        sl = pl.ds(base, sup)
        q = q_ref[sl, :].astype(f32) * (GLA_DK ** -0.5)
        k = k_ref[sl, :].astype(f32)
        q_t = (q * jnp.exp(b)).astype(bf16)
        k_t = (k * jnp.exp(-b)).astype(bf16)
        k_s = (k * jnp.exp(tot - b)).astype(bf16)
        return q_t, k_t, k_s, jnp.exp(tot)

    def stage_intra(base, q_t, k_t, fwd):
        smask = same & ((col <= row) if fwd else (col > row))
        scores = jnp.where(smask, _nt_dot(q_t, k_t), 0.0)
        return jnp.dot(scores.astype(bf16), v_ref[pl.ds(base, sup), :], preferred_element_type=f32)

    def chunk_step(base, c, fwd, q_t, k_s, dec, o_intra, s_t):
        acc_ref = accf_ref if fwd else accb_ref
        rs = slice(c * c_len, (c + 1) * c_len)
        rows_c = pl.ds(base + c * c_len, c_len)
        acc_ref[rows_c, :] = o_intra[rs, :] + _nt_dot(q_t[rs, :], s_t.astype(bf16))
        return s_t * dec[c * c_len:c * c_len + 1, :] + _tn_dot(v_ref[rows_c, :], k_s[rs, :])

    def body(i, carry):
        s_f, s_b = carry
        base_f = pl.multiple_of(i * sup, sup)
        base_b = pl.multiple_of((n_sup - 1 - i) * sup, sup)
        parts_f = stage_decay(base_f, True)
        parts_b = stage_decay(base_b, False)
        b_f, tot_f = stage_cumsum(parts_f, True)
        b_b, tot_b = stage_cumsum(parts_b, False)
        qt_f, kt_f, ks_f, dec_f = stage_scale(base_f, b_f, tot_f)
        qt_b, kt_b, ks_b, dec_b = stage_scale(base_b, b_b, tot_b)
        oi_f = stage_intra(base_f, qt_f, kt_f, True)
        oi_b = stage_intra(base_b, qt_b, kt_b, False)
        for c in range(GLA_SUPER):
            s_f = chunk_step(base_f, c, True, qt_f, ks_f, dec_f, oi_f, s_f)
            s_b = chunk_step(base_b, GLA_SUPER - 1 - c, False, qt_b, ks_b, dec_b, oi_b, s_b)
        return s_f, s_b

    zero = jnp.zeros((GLA_DV, GLA_DK), f32)
    lax.fori_loop(0, n_sup, body, (zero, zero))

    rows = 256
    gain = g_ref[...]

    def finish(i, carry):
        sl = pl.ds(pl.multiple_of(i * rows, rows), rows)
        x = accf_ref[sl, :] + accb_ref[sl, :]
        ms = jnp.mean(x * x, axis=-1, keepdims=True)
        y = x * lax.rsqrt(ms + RMS_EPS) * gain
        r = r_ref[sl, :].astype(f32)
        o_ref[sl, :] = (y * (r * jax.nn.sigmoid(r))).astype(o_ref.dtype)
        return carry

    lax.fori_loop(0, t // rows, finish, 0)


def _gla(main, w2f, b_f, w2b, b_b, gain, layer, bsz, t):
    n = bsz * t
    blk = lambda off, width: (lambda b, h: (off // TN_MAIN, b, (off % TN_MAIN) // width + h))
    wspec = pl.BlockSpec((None, GLA_RANK, GLA_DK), lambda b, h: (layer, 0, h))
    bspec = pl.BlockSpec((None, 1, GLA_DK), lambda b, h: (layer, 0, h))
    return pl.pallas_call(
        functools.partial(_gla_kernel, t=t),
        grid=(bsz, GLA_HEADS),
        in_specs=[
            pl.BlockSpec((None, t, GLA_DK), blk(OFF_AQ, GLA_DK)),
            pl.BlockSpec((None, t, GLA_DK), blk(OFF_AK, GLA_DK)),
            pl.BlockSpec((None, t, GLA_DV), blk(OFF_AV, GLA_DV)),
            pl.BlockSpec((None, t, GLA_DV), blk(OFF_AR, GLA_DV)),
            pl.BlockSpec((None, t, LR_PAD),
                         lambda b, h: (OFF_LR // TN_MAIN, b, (OFF_LR % TN_MAIN) // LR_PAD)),
            wspec, bspec, wspec, bspec,
            pl.BlockSpec((None, 1, GLA_DV), lambda b, h: (layer, 0, h)),
        ],
        out_specs=pl.BlockSpec((t, GLA_DV), lambda b, h: (b, h)),
        out_shape=jax.ShapeDtypeStruct((n, A_V), bf16),
        scratch_shapes=[pltpu.VMEM((t, GLA_DV), f32), pltpu.VMEM((t, GLA_DV), f32)],
        compiler_params=_cparams(("parallel", "parallel")),
    )(main, main, main, main, main, w2f, b_f, w2b, b_b, gain)


def _dil_kernel(q_ref, k_ref, v_ref, o_ref, lse_ref, *scratch, t, dil):
    n_sub = t // dil
    bq = min(128, n_sub)
    win = min(bq + 2 * DIL_HALF, n_sub)
    n_blk = n_sub // bq
    qi = lax.broadcasted_iota(jnp.int32, (bq, win), 0)
    ki = lax.broadcasted_iota(jnp.int32, (bq, win), 1)
    cvt = 512

    if dil > 1:
        q32, k32, v32, o32 = scratch

        def widen(i, carry):
            sl = pl.ds(pl.multiple_of(i * cvt, cvt), cvt)
            q32[sl, :] = q_ref[0, sl, :].astype(f32)
            k32[sl, :] = k_ref[0, sl, :].astype(f32)
            v32[sl, :] = v_ref[0, sl, :].astype(f32)
            return carry

        lax.fori_loop(0, t // cvt, widen, 0)

    def body(idx, carry):
        r = idx // n_blk
        i = idx % n_blk
        q0 = i * bq
        k0 = jnp.clip(q0 - DIL_HALF, 0, n_sub - win)
        if dil == 1:
            qs = pl.ds(pl.multiple_of(q0, bq), bq)
            ks = pl.ds(pl.multiple_of(k0, DIL_HALF), win)
            q = q_ref[0, qs, :]
            k = k_ref[0, ks, :]
            v = v_ref[0, ks, :]
        else:
            qs = pl.ds(r + q0 * dil, bq, stride=dil)
            ks = pl.ds(r + k0 * dil, win, stride=dil)
            q = q32[qs, :].astype(bf16)
            k = k32[ks, :].astype(bf16)
            v = v32[ks, :].astype(bf16)
        s = _nt_dot(q, k)
        valid = jnp.abs((q0 + qi) - (k0 + ki)) <= DIL_HALF
        s = jnp.where(valid, s, NEG_BIG)
        m = jnp.max(s, axis=-1, keepdims=True)
        p = jnp.exp(s - m)
        l = jnp.sum(p, axis=-1, keepdims=True)
        o = jnp.dot(p.astype(bf16), v, preferred_element_type=f32) / l
        lse = jnp.broadcast_to(m + jnp.log(l), (bq, HEAD_DIM))
        if dil == 1:
            o_ref[0, qs, :] = o.astype(o_ref.dtype)
            lse_ref[0, qs, :] = lse
        else:
            o32[qs, :] = o
            lse_ref[0, qs, :] = lse
        return carry

    def body_group(j, carry):
        for u in range(DIL_UNROLL):
            body(j * DIL_UNROLL + u, carry)
        return carry

    lax.fori_loop(0, dil * n_blk // DIL_UNROLL, body_group, 0)

    if dil > 1:
        def narrow(i, carry):
            sl = pl.ds(pl.multiple_of(i * cvt, cvt), cvt)
            o_ref[0, sl, :] = o32[sl, :].astype(o_ref.dtype)
            return carry

        lax.fori_loop(0, t // cvt, narrow, 0)


def _dilated(qkv_b, group, bsz, t):
    _, dil = DIL_CONFIGS[group]
    n = bsz * t
    blk = (1, t, HEAD_DIM)
    head0 = group * DIL_HEADS
    scratch = [pltpu.VMEM((t, HEAD_DIM), f32)] * 4 if dil > 1 else []
    return pl.pallas_call(
        functools.partial(_dil_kernel, t=t, dil=dil),
        grid=(bsz, DIL_HEADS),
        in_specs=[
            pl.BlockSpec(blk, lambda b, h: (head0 + h, b, 0)),
            pl.BlockSpec(blk, lambda b, h: (B_HEADS + head0 + h, b, 0)),
            pl.BlockSpec(blk, lambda b, h: (2 * B_HEADS + head0 + h, b, 0)),
        ],
        out_specs=[
            pl.BlockSpec(blk, lambda b, h: (h, b, 0)),
            pl.BlockSpec(blk, lambda b, h: (h, b, 0)),
        ],
        out_shape=[
            jax.ShapeDtypeStruct((DIL_HEADS, n, HEAD_DIM), bf16),
            jax.ShapeDtypeStruct((DIL_HEADS, n, HEAD_DIM), f32),
        ],
        scratch_shapes=scratch,
        compiler_params=_cparams(("parallel", "parallel")),
    )(qkv_b, qkv_b, qkv_b)


def _c_prep_kernel(q_ref, k_ref, cos_ref, sin_ref, qg_ref, kg_ref, qo_ref, ko_ref):
    cos = cos_ref[...]
    sin = sin_ref[...]

    def prep(x, gain, scale):
        x = x.astype(f32)
        ms = jnp.mean(x * x, axis=-1, keepdims=True)
        y = x * lax.rsqrt(ms + RMS_EPS) * gain
        return (y * cos + _axial_partner(y) * sin) * scale

    qg = qg_ref[...]
    kg = kg_ref[...]
    for h in range(GQA_Q_HEADS):
        ls = slice(h * HEAD_DIM, (h + 1) * HEAD_DIM)
        qo_ref[:, ls] = prep(q_ref[:, ls], qg, HEAD_DIM ** -0.5 * LOG2_E).astype(qo_ref.dtype)
    for h in range(GQA_KV_HEADS):
        ls = slice(h * HEAD_DIM, (h + 1) * HEAD_DIM)
        ko_ref[:, ls] = prep(k_ref[:, ls], kg, 1.0).astype(ko_ref.dtype)


def _c_prep(main, cos_c, sin_c, q_gain, k_gain, layer, t):
    n = main.shape[1]
    tm = TM_PROJ
    tpb = t // tm
    gspec = pl.BlockSpec((None, 1, HEAD_DIM), lambda i: (layer, 0, 0))
    return pl.pallas_call(
        _c_prep_kernel,
        grid=(n // tm,),
        in_specs=[
            pl.BlockSpec((None, tm, C_Q), lambda i: (OFF_CQ // TN_MAIN, i, (OFF_CQ % TN_MAIN) // C_Q)),
            pl.BlockSpec((None, tm, C_KV), lambda i: (OFF_CK // TN_MAIN, i, (OFF_CK % TN_MAIN) // C_KV)),
            pl.BlockSpec((tm, HEAD_DIM), lambda i: (i % tpb, 0)),
            pl.BlockSpec((tm, HEAD_DIM), lambda i: (i % tpb, 0)),
            gspec, gspec,
        ],
        out_specs=[
            pl.BlockSpec((tm, C_Q), lambda i: (i, 0)),
            pl.BlockSpec((tm, C_KV), lambda i: (i, 0)),
        ],
        out_shape=[
            jax.ShapeDtypeStruct((n, C_Q), bf16),
            jax.ShapeDtypeStruct((n, C_KV), bf16),
        ],
        compiler_params=_cparams(("parallel",)),
    )(main, main, cos_c, sin_c, q_gain, k_gain)


def _gqa_kernel(q_ref, k_ref, v_ref, o_ref, s_scr, p_scr, m_scr, l_scr, acc_scr, *, t, tq, tk):
    grp = GQA_Q_HEADS // GQA_KV_HEADS
    q4 = q_ref[...]
    q = jnp.concatenate([q4[:, h * HEAD_DIM:(h + 1) * HEAD_DIM] for h in range(grp)], axis=0)
    rows = grp * tq
    n_strips = rows // GQA_STRIP
    m_scr[...] = jnp.full((rows, LANES), NEG_BIG, f32)
    l_scr[...] = jnp.zeros((rows, LANES), f32)
    acc_scr[...] = jnp.zeros((rows, HEAD_DIM), f32)

    def widen(col):
        return jnp.broadcast_to(col, (col.shape[0], LANES))

    def softmax_chunk(buf):
        mx = []
        for i in range(n_strips):
            rs = slice(i * GQA_STRIP, (i + 1) * GQA_STRIP)
            mx.append(widen(jnp.max(s_scr[buf, rs, :], axis=-1, keepdims=True)))
        m_old = m_scr[...]
        m_new = jnp.maximum(m_old, jnp.concatenate(mx, axis=0))
        a = jnp.exp2(m_old - m_new)
        m_scr[...] = m_new
        sums = []
        for i in range(n_strips):
            rs = slice(i * GQA_STRIP, (i + 1) * GQA_STRIP)
            m_wide = jnp.concatenate([m_new[rs, :]] * (tk // LANES), axis=1)
            p = jnp.exp2(s_scr[buf, rs, :] - m_wide)
            sums.append(widen(jnp.sum(p, axis=-1, keepdims=True)))
            p_scr[buf, rs, :] = p.astype(bf16)
        l_scr[...] = a * l_scr[...] + jnp.concatenate(sums, axis=0)
        return a

    def chunk_group(cg, carry):
        sls = []
        for u in range(GQA_GROUP):
            sl = pl.ds(pl.multiple_of((cg * GQA_GROUP + u) * tk, tk), tk)
            sls.append(sl)
            s_scr[u] = _nt_dot(q, k_ref[sl, :])
        for u in range(GQA_GROUP):
            a = softmax_chunk(u)
            acc_scr[...] = acc_scr[...] * a + jnp.dot(p_scr[u], v_ref[sls[u], :],
                                                      preferred_element_type=f32)
        return carry

    lax.fori_loop(0, t // (tk * GQA_GROUP), chunk_group, 0)
    o = (acc_scr[...] / l_scr[...]).astype(o_ref.dtype)
    o_ref[...] = jnp.concatenate([o[h * tq:(h + 1) * tq, :] for h in range(grp)], axis=1)


def _gqa(qc, kc, main, bsz, t):
    n = bsz * t
    tq = GQA_TQ
    tk = min(t, GQA_TK)
    grp = GQA_Q_HEADS // GQA_KV_HEADS
    grp_cols = grp * HEAD_DIM
    rows = grp * tq
    nq = t // tq
    return pl.pallas_call(
        functools.partial(_gqa_kernel, t=t, tq=tq, tk=tk),
        grid=(bsz, GQA_KV_HEADS, nq),
        in_specs=[
            pl.BlockSpec((tq, grp_cols), lambda b, j, i: (b * nq + i, j)),
            pl.BlockSpec((t, HEAD_DIM), lambda b, j, i: (b, j)),
            pl.BlockSpec((None, t, HEAD_DIM),
                         lambda b, j, i: (OFF_CV // TN_MAIN, b, (OFF_CV % TN_MAIN) // HEAD_DIM + j)),
        ],
        out_specs=pl.BlockSpec((tq, grp_cols), lambda b, j, i: (b * nq + i, j)),
        out_shape=jax.ShapeDtypeStruct((n, C_Q), bf16),
        scratch_shapes=[
            pltpu.VMEM((GQA_GROUP, rows, tk), f32),
            pltpu.VMEM((GQA_GROUP, rows, tk), bf16),
            pltpu.VMEM((rows, LANES), f32),
            pltpu.VMEM((rows, LANES), f32),
            pltpu.VMEM((rows, HEAD_DIM), f32),
        ],
        compiler_params=_cparams(("parallel", "parallel", "parallel")),
    )(qc, kc, main)


def _merge_kernel(oa_ref, ob0_ref, ob1_ref, ob2_ref, l0_ref, l1_ref, l2_ref, oc_ref,
                  ga_ref, gb_ref, gc_ref, wa_ref, wb_ref, wc_ref, o_ref):
    heads = []
    for h in range(DIL_HEADS):
        l0, l1, l2 = l0_ref[h], l1_ref[h], l2_ref[h]
        m = jnp.maximum(jnp.maximum(l0, l1), l2)
        e0, e1, e2 = jnp.exp(l0 - m), jnp.exp(l1 - m), jnp.exp(l2 - m)
        mix = (e0 * ob0_ref[h].astype(f32) + e1 * ob1_ref[h].astype(f32)
               + e2 * ob2_ref[h].astype(f32)) / (e0 + e1 + e2)
        heads.append(mix.astype(bf16))
    o_b = jnp.concatenate(heads, axis=1)
    up_a = jnp.dot(oa_ref[...], wa_ref[...], preferred_element_type=f32)
    up_b = jnp.dot(o_b, wb_ref[...], preferred_element_type=f32)
    up_c = jnp.dot(oc_ref[...], wc_ref[...], preferred_element_type=f32)
    def gate(ref):
        logits = jnp.concatenate([ref[c] for c in range(ref.shape[0])], axis=1)
        return jax.nn.sigmoid(logits.astype(f32))

    merged = gate(ga_ref) * up_a + gate(gb_ref) * up_b + gate(gc_ref) * up_c
    o_ref[...] = merged.astype(o_ref.dtype)


def _merge(o_a, o_bs, lses, o_c, main, w_up_a, w_up_b, w_up_c, layer):
    n = o_a.shape[0]
    tm = TM_TOK
    tiles_per_gate = D_MODEL // TN_MAIN
    hspec = pl.BlockSpec((DIL_HEADS, tm, HEAD_DIM), lambda i: (0, i, 0))
    gspec = lambda which: pl.BlockSpec((tiles_per_gate, tm, TN_MAIN),
                                       lambda i: (OFF_GATE // D_MODEL + which, i, 0))
    wspec = lambda rows: pl.BlockSpec((None, rows, D_MODEL), lambda i: (layer, 0, 0))
    return pl.pallas_call(
        _merge_kernel,
        grid=(n // tm,),
        in_specs=[
            pl.BlockSpec((tm, A_V), lambda i: (i, 0)),
            hspec, hspec, hspec, hspec, hspec, hspec,
            pl.BlockSpec((tm, C_Q), lambda i: (i, 0)),
            gspec(0), gspec(1), gspec(2),
            wspec(A_V), wspec(B_OUT), wspec(C_Q),
        ],
        out_specs=pl.BlockSpec((tm, D_MODEL), lambda i: (i, 0)),
        out_shape=jax.ShapeDtypeStruct((n, D_MODEL), bf16),
        compiler_params=_cparams(("parallel",)),
    )(o_a, o_bs[0], o_bs[1], o_bs[2], lses[0], lses[1], lses[2], o_c,
      main, main, main, w_up_a, w_up_b, w_up_c)


def _layer_norm_rows(y, g, b):
    mu = jnp.mean(y, axis=-1, keepdims=True)
    d = y - mu
    var = jnp.mean(d * d, axis=-1, keepdims=True)
    return d * lax.rsqrt(var + LN_EPS) * g + b


def _route_rows(logits_t, bias_col):
    scores = jax.nn.sigmoid(logits_t)
    sel = scores + bias_col
    rows = [sel[e:e + 1, :] for e in range(N_EXPERTS)]
    srow = [scores[e:e + 1, :] for e in range(N_EXPERTS)]
    best_val = None
    best_grp = None
    for g in range(N_EXPERT_GROUPS):
        mem = rows[g * EXPERTS_PER_GROUP:(g + 1) * EXPERTS_PER_GROUP]
        top2 = None
        for a in range(EXPERTS_PER_GROUP):
            for b in range(a + 1, EXPERTS_PER_GROUP):
                pair = mem[a] + mem[b]
                top2 = pair if top2 is None else jnp.maximum(top2, pair)
        if g == 0:
            best_val, best_grp = top2, jnp.zeros_like(top2, dtype=jnp.int32)
        else:
            upd = top2 > best_val
            best_val = jnp.where(upd, top2, best_val)
            best_grp = jnp.where(upd, g, best_grp)
    neg = jnp.full_like(best_val, -jnp.inf)
    cand = [jnp.where(best_grp == (e // EXPERTS_PER_GROUP), rows[e], neg) for e in range(N_EXPERTS)]

    def arg_top(vals):
        bv, bi = vals[0], jnp.zeros_like(best_grp)
        for e in range(1, N_EXPERTS):
            upd = vals[e] > bv
            bv = jnp.where(upd, vals[e], bv)
            bi = jnp.where(upd, e, bi)
        return bi

    idx1 = arg_top(cand)
    idx2 = arg_top([jnp.where(idx1 == e, neg, cand[e]) for e in range(N_EXPERTS)])
    zero = jnp.zeros_like(best_val)
    s1 = zero
    s2 = zero
    for e in range(N_EXPERTS):
        s1 = s1 + jnp.where(idx1 == e, srow[e], zero)
        s2 = s2 + jnp.where(idx2 == e, srow[e], zero)
    tot = s1 + s2
    return idx1, idx2, s1 / tot, s2 / tot


def _outproj_kernel(m_ref, w_ref, x_ref, g_ref, b_ref, wr_ref, x1_ref, x1b_ref, lg_ref):
    h = jnp.dot(m_ref[...], w_ref[...], preferred_element_type=f32)
    x1 = _layer_norm_rows(DEEPNORM_ALPHA * x_ref[...] + h, g_ref[...], b_ref[...])
    x1_ref[...] = x1
    x_hi = x1.astype(bf16)
    x1b_ref[...] = x_hi
    x_lo = (x1 - x_hi.astype(f32)).astype(bf16)
    wr = wr_ref[...]
    w_hi = wr.astype(bf16)
    w_lo = (wr - w_hi.astype(f32)).astype(bf16)
    lg_ref[...] = (jnp.dot(x_hi, w_hi, preferred_element_type=f32)
                   + jnp.dot(x_hi, w_lo, preferred_element_type=f32)
                   + jnp.dot(x_lo, w_hi, preferred_element_type=f32))


def _outproj_ln(merged, w_out, x, ln_g, ln_b, w_router_pad, layer):
    n = x.shape[0]
    tm = TM_TOK
    vspec = pl.BlockSpec((None, 1, D_MODEL), lambda i: (layer, 0, 0))
    row = pl.BlockSpec((tm, D_MODEL), lambda i: (i, 0))
    return pl.pallas_call(
        _outproj_kernel,
        grid=(n // tm,),
        in_specs=[
            row,
            pl.BlockSpec((None, D_MODEL, D_MODEL), lambda i: (layer, 0, 0)),
            row,
            vspec, vspec,
            pl.BlockSpec((D_MODEL, LANES), lambda i: (0, 0)),
        ],
        out_specs=[row, row, pl.BlockSpec((tm, LANES), lambda i: (i, 0))],
        out_shape=[
            jax.ShapeDtypeStruct((n, D_MODEL), f32),
            jax.ShapeDtypeStruct((n, D_MODEL), bf16),
            jax.ShapeDtypeStruct((n, LANES), f32),
        ],
        compiler_params=_cparams(("parallel",)),
    )(merged, w_out, x, ln_g, ln_b, w_router_pad)


def _route_kernel(lg_ref, rb_ref, idx_ref, w1_ref, w2_ref):
    logits_t = lg_ref[...].T[0:N_EXPERTS, :]
    i1, i2, w1, w2 = _route_rows(logits_t, rb_ref[...])
    idx_ref[0:1, :] = i1
    idx_ref[1:2, :] = i2
    tr = logits_t.shape[1]
    w1_ref[...] = jnp.broadcast_to(w1, (LANES, tr)).T
    w2_ref[...] = jnp.broadcast_to(w2, (LANES, tr)).T


def _route(logits, router_bias_col):
    n = logits.shape[0]
    tr = TM_ROUTE
    return pl.pallas_call(
        _route_kernel,
        grid=(n // tr,),
        in_specs=[
            pl.BlockSpec((tr, LANES), lambda i: (i, 0)),
            pl.BlockSpec((N_EXPERTS, 1), lambda i: (0, 0)),
        ],
        out_specs=[
            pl.BlockSpec((2, tr), lambda i: (0, i)),
            pl.BlockSpec((tr, LANES), lambda i: (i, 0)),
            pl.BlockSpec((tr, LANES), lambda i: (i, 0)),
        ],
        out_shape=[
            jax.ShapeDtypeStruct((2, n), jnp.int32),
            jax.ShapeDtypeStruct((n, LANES), f32),
            jax.ShapeDtypeStruct((n, LANES), f32),
        ],
        compiler_params=_cparams(("parallel",)),
    )(logits, router_bias_col)


def _moe_kernel(te_ref, nt_ref, *refs, tiles_per_piece):
    del te_ref
    x_refs, (wg_ref, wu_ref, wd_ref, o_ref) = refs[:-4], refs[-4:]
    tile = pl.program_id(0)

    @pl.when(tile < nt_ref[0])
    def _():
        x = x_refs[0][...]
        for p in range(1, len(x_refs)):
            x = jnp.where(tile >= p * tiles_per_piece, x_refs[p][...], x)
        gate = jnp.dot(x, wg_ref[...], preferred_element_type=f32)
        up = jnp.dot(x, wu_ref[...], preferred_element_type=f32)
        h = (gate * jax.nn.sigmoid(gate) * up).astype(bf16)
        o_ref[...] = jnp.dot(h, wd_ref[...], preferred_element_type=f32).astype(o_ref.dtype)

    @pl.when(tile >= nt_ref[0])
    def _():
        o_ref[...] = jnp.zeros_like(o_ref)


def _moe_grouped(x_pieces, tile_expert, n_tiles_used, w_gate, w_up, w_down, layer):
    tpp = x_pieces[0].shape[0] // TM_MOE
    n_tiles = tpp * len(x_pieces)
    wspec_in = pl.BlockSpec((None, None, D_MODEL, D_FF_EXPERT), lambda t, te, nt: (layer, te[t], 0, 0))
    wspec_out = pl.BlockSpec((None, None, D_FF_EXPERT, D_MODEL), lambda t, te, nt: (layer, te[t], 0, 0))

    def piece_spec(p):
        return pl.BlockSpec((TM_MOE, D_MODEL),
                            lambda t, te, nt: (jnp.clip(t - p * tpp, 0, tpp - 1), 0))

    grid_spec = pltpu.PrefetchScalarGridSpec(
        num_scalar_prefetch=2,
        grid=(n_tiles,),
        in_specs=[piece_spec(p) for p in range(len(x_pieces))] + [wspec_in, wspec_in, wspec_out],
        out_specs=pl.BlockSpec((TM_MOE, D_MODEL), lambda t, te, nt: (t, 0)),
    )
    return pl.pallas_call(
        functools.partial(_moe_kernel, tiles_per_piece=tpp),
        grid_spec=grid_spec,
        out_shape=jax.ShapeDtypeStruct((n_tiles * TM_MOE, D_MODEL), bf16),
        compiler_params=_cparams(("arbitrary",)),
    )(tile_expert, n_tiles_used, *x_pieces, w_gate, w_up, w_down)


def _combine_kernel(x_ref, *refs, n_pieces, tiles_per_piece):
    y1_refs, y2_refs = refs[:n_pieces], refs[n_pieces:2 * n_pieces]
    w1_ref, w2_ref, g_ref, b_ref, o_ref, ob_ref = refs[2 * n_pieces:]
    tile = pl.program_id(0)

    def pick(piece_refs):
        y = piece_refs[0][...]
        for p in range(1, n_pieces):
            y = jnp.where(tile >= p * tiles_per_piece, piece_refs[p][...], y)
        return y

    y1 = pick(y1_refs)
    y2 = pick(y2_refs)
    w1 = w1_ref[...]
    w2 = w2_ref[...]
    for c in range(D_MODEL // LANES):
        ls = slice(c * LANES, (c + 1) * LANES)
        o_ref[:, ls] = (DEEPNORM_ALPHA * x_ref[:, ls] + w1 * y1[:, ls].astype(f32)
                        + w2 * y2[:, ls].astype(f32))
    x2 = _layer_norm_rows(o_ref[...], g_ref[...], b_ref[...])
    o_ref[...] = x2
    ob_ref[...] = x2.astype(bf16)


def _combine_ln(x1, y1_pieces, y2_pieces, w1b, w2b, ln_g, ln_b, layer):
    n = x1.shape[0]
    tm = TM_TOK
    n_pieces = len(y1_pieces)
    tpp = y1_pieces[0].shape[0] // tm
    row = pl.BlockSpec((tm, D_MODEL), lambda i: (i, 0))
    wsp = pl.BlockSpec((tm, LANES), lambda i: (i, 0))
    vspec = pl.BlockSpec((None, 1, D_MODEL), lambda i: (layer, 0, 0))

    def piece_spec(p):
        return pl.BlockSpec((tm, D_MODEL), lambda i: (jnp.clip(i - p * tpp, 0, tpp - 1), 0))

    pieces = [piece_spec(p) for p in range(n_pieces)]
    return pl.pallas_call(
        functools.partial(_combine_kernel, n_pieces=n_pieces, tiles_per_piece=tpp),
        grid=(n // tm,),
        in_specs=[row] + pieces + pieces + [wsp, wsp, vspec, vspec],
        out_specs=[row, row],
        out_shape=[jax.ShapeDtypeStruct((n, D_MODEL), f32), jax.ShapeDtypeStruct((n, D_MODEL), bf16)],
        compiler_params=_cparams(("parallel",)),
    )(x1, *y1_pieces, *y2_pieces, w1b, w2b, ln_g, ln_b)


def _dispatch_plan(idx):
    n = idx.shape[1]
    m_pad = 2 * n + N_EXPERTS * TM_MOE
    n_tiles = m_pad // TM_MOE
    e_flat = idx.reshape(-1)
    onehot = (e_flat[:, None] == jnp.arange(N_EXPERTS, dtype=jnp.int32)[None, :]).astype(jnp.int32)
    rank = jnp.sum((jnp.cumsum(onehot, axis=0) - onehot) * onehot, axis=1)
    counts = jnp.sum(onehot, axis=0)
    tiles_per = (counts + TM_MOE - 1) // TM_MOE
    tile_end = jnp.cumsum(tiles_per)
    start_row = (tile_end - tiles_per) * TM_MOE
    dest = jnp.sum(onehot * start_row[None, :], axis=1) + rank
    tok = jnp.tile(jnp.arange(n, dtype=jnp.int32), 2)
    src = jnp.zeros((m_pad,), jnp.int32).at[dest].set(tok, mode="promise_in_bounds", unique_indices=True)
    n_used = tile_end[-1]
    tile_ids = jnp.minimum(jnp.arange(n_tiles, dtype=jnp.int32), n_used - 1)
    tile_expert = jnp.sum((tile_ids[:, None] >= tile_end[None, :]).astype(jnp.int32), axis=1)
    tile_expert = jnp.minimum(tile_expert, N_EXPERTS - 1)
    return src, dest, tile_expert, n_used.reshape(1).astype(jnp.int32)


def _rope_tables(pos, dim):
    inv_freq = ROPE_THETA ** (-jnp.arange(0, dim, 2, dtype=f32) / dim)
    ang = pos.astype(f32)[:, None] * inv_freq[None, :]
    return jnp.cos(ang), jnp.sin(ang)


def _tables(t):
    cos, sin = _rope_tables(jnp.arange(t), HEAD_DIM)
    cos_b = jnp.concatenate([cos, cos], axis=-1)
    sin_b = jnp.concatenate([-sin, sin], axis=-1)
    tok = jnp.arange(t)
    cr, sr = _rope_tables(tok // GRID_W, HEAD_DIM // 2)
    cc, sc = _rope_tables(tok % GRID_W, HEAD_DIM // 2)
    cos_c = jnp.concatenate([cr, cr, cc, cc], axis=-1)
    sin_c = jnp.concatenate([-sr, sr, -sc, sc], axis=-1)
    return cos_b, sin_b, cos_c, sin_c


def _repack_w_in(w_in):
    offs = [0]
    for s in IN_SPLITS:
        offs.append(offs[-1] + s)
    seg = [w_in[:, :, offs[i]:offs[i + 1]] for i in range(len(IN_SPLITS))]
    a_q, a_k, a_v, a_r, a_lr, b_qkv, c_q, c_k, c_v, gate = seg
    lr_pad = jnp.pad(a_lr, ((0, 0), (0, 0), (0, LR_PAD - A_LR)))
    w_main = jnp.concatenate([gate, a_q, a_k, a_v, a_r, c_q, c_k, c_v, lr_pad], axis=-1)
    w_main = jnp.pad(w_main, ((0, 0), (0, 0), (0, MAIN_COLS - MAIN_USED))).astype(bf16)
    return _tile_major(w_main, TN_MAIN), _tile_major(b_qkv.astype(bf16), TN_B)


def _tile_major(w, tn):
    n_layers, k, cols = w.shape
    return w.reshape(n_layers, k, cols // tn, tn).transpose(0, 2, 1, 3)


def _mixer(xb, wts, layer, bsz, t, tables):
    cos_b, sin_b, cos_c, sin_c = tables
    main = _proj_main(xb, wts["w_main"], layer)
    qkv_b = _proj_b(xb, wts["w_b"], cos_b, sin_b, layer, t)
    o_a = _gla(main, wts["gla_w2_f"], wts["gla_b_f"], wts["gla_w2_b"], wts["gla_b_b"],
               wts["gla_norm_g"], layer, bsz, t)
    o_bs, lses = [], []
    for group in range(len(DIL_CONFIGS)):
        o_g, lse_g = _dilated(qkv_b, group, bsz, t)
        o_bs.append(o_g)
        lses.append(lse_g)
    qc, kc = _c_prep(main, cos_c, sin_c, wts["q_norm_g"], wts["k_norm_g"], layer, t)
    o_c = _gqa(qc, kc, main, bsz, t)
    return _merge(o_a, o_bs, lses, o_c, main, wts["w_up_a"], wts["w_up_b"], wts["w_up_c"], layer)


def _gather_rows(a, rows):
    return a.at[rows].get(mode="promise_in_bounds")


class _Trunk:
    def __init__(self, x3, wts):
        self.bsz, self.t, _ = x3.shape
        self.wts = wts
        self.x = x3.reshape(self.bsz * self.t, D_MODEL)
        self.xb = self.x.astype(bf16)
        self.tables = _tables(self.t)

    def mix_and_route(self, layer):
        wts = self.wts
        merged = _mixer(self.xb, wts, layer, self.bsz, self.t, self.tables)
        self.x1, x1b, logits = _outproj_ln(merged, wts["w_out"], self.x, wts["ln1_g"], wts["ln1_b"],
                                           wts["w_router_pad"], layer)
        idx, self.w1b, self.w2b = _route(logits, wts["router_bias_col"])
        src, self.dest, self.tile_expert, self.n_used = _dispatch_plan(idx)
        n_tiles = src.shape[0] // TM_MOE
        max_rows = min(MOE_GATHER_ROWS, x1b.shape[0])
        n_pieces = next(d for d in range(1, n_tiles + 1)
                        if n_tiles % d == 0 and src.shape[0] // d <= max_rows)
        copies = -(-GATHER_TABLE_ROWS // x1b.shape[0])
        table = x1b if copies == 1 else jnp.concatenate([x1b] * copies, axis=0)
        self.x_pieces = [_gather_rows(table, piece) for piece in jnp.split(src, n_pieces)]

    def experts(self, layer):
        wts = self.wts
        n = self.x1.shape[0]
        y_sorted = _moe_grouped(self.x_pieces, self.tile_expert, self.n_used, wts["moe_w_gate"],
                                wts["moe_w_up"], wts["moe_w_down"], layer)
        c_pieces = -(-n // COMBINE_GATHER_ROWS)
        self.y1 = [_gather_rows(y_sorted, piece) for piece in jnp.split(self.dest[:n], c_pieces)]
        self.y2 = [_gather_rows(y_sorted, piece) for piece in jnp.split(self.dest[n:], c_pieces)]

    def combine(self, layer):
        self.x, self.xb = _combine_ln(self.x1, self.y1, self.y2, self.w1b, self.w2b,
                                      self.wts["ln2_g"], self.wts["ln2_b"], layer)

    def result(self):
        return self.x.reshape(self.bsz, self.t, D_MODEL)


def _run_trunks(inputs, wts, depth=DEPTH):
    trunks = [_Trunk(x3, wts) for x3 in inputs]
    for layer in range(depth):
        for stage in (_Trunk.mix_and_route, _Trunk.experts, _Trunk.combine):
            for trunk in trunks:
                stage(trunk, layer)
    return tuple(trunk.result() for trunk in trunks)


def _cast_kernel(x_ref, o_ref):
    o_ref[...] = x_ref[...].astype(o_ref.dtype)


def _expert_weights_bf16(w):
    n_layers, n_exp, rows, cols = w.shape
    spec = pl.BlockSpec((None, None, rows, cols), lambda i, j: (i, j, 0, 0))
    return pl.pallas_call(
        _cast_kernel,
        grid=(n_layers, n_exp),
        in_specs=[spec],
        out_specs=spec,
        out_shape=jax.ShapeDtypeStruct(w.shape, bf16),
        compiler_params=_cparams(("parallel", "parallel")),
    )(w)


def _prepare_weights(w_in, gla_w2_f, gla_b_f, gla_w2_b, gla_b_b, gla_norm_g, q_norm_g, k_norm_g,
                     w_up_a, w_up_b, w_up_c, w_out, ln1_g, ln1_b, w_router, router_bias,
                     moe_w_gate, moe_w_up, moe_w_down, ln2_g, ln2_b):
    w_main, w_b = _repack_w_in(w_in)
    row = lambda a: a.reshape(DEPTH, 1, a.shape[-1]).astype(f32)
    return {
        "w_main": w_main,
        "w_b": w_b,
        "gla_w2_f": gla_w2_f.astype(f32),
        "gla_b_f": row(gla_b_f),
        "gla_w2_b": gla_w2_b.astype(f32),
        "gla_b_b": row(gla_b_b),
        "gla_norm_g": row(gla_norm_g),
        "q_norm_g": row(q_norm_g),
        "k_norm_g": row(k_norm_g),
        "w_up_a": w_up_a.astype(bf16),
        "w_up_b": w_up_b.astype(bf16),
        "w_up_c": w_up_c.astype(bf16),
        "w_out": w_out.astype(bf16),
        "ln1_g": row(ln1_g),
        "ln1_b": row(ln1_b),
        "w_router_pad": jnp.pad(w_router.astype(f32), ((0, 0), (0, LANES - N_EXPERTS))),
        "router_bias_col": router_bias.astype(f32).reshape(N_EXPERTS, 1),
        "moe_w_gate": _expert_weights_bf16(moe_w_gate.astype(f32)),
        "moe_w_up": _expert_weights_bf16(moe_w_up.astype(f32)),
        "moe_w_down": _expert_weights_bf16(moe_w_down.astype(f32)),
        "ln2_g": row(ln2_g),
        "ln2_b": row(ln2_b),
    }


def kernel(x_prompt, x_sample, w_in, gla_w2_f, gla_b_f, gla_w2_b, gla_b_b, gla_norm_g, q_norm_g, k_norm_g, w_up_a, w_up_b, w_up_c, w_out, ln1_g, ln1_b, w_router, router_bias, moe_w_gate, moe_w_up, moe_w_down, ln2_g, ln2_b):
    wts = _prepare_weights(w_in, gla_w2_f, gla_b_f, gla_w2_b, gla_b_b, gla_norm_g, q_norm_g, k_norm_g,
                           w_up_a, w_up_b, w_up_c, w_out, ln1_g, ln1_b, w_router, router_bias,
                           moe_w_gate, moe_w_up, moe_w_down, ln2_g, ln2_b)
    return _run_trunks((x_prompt, x_sample), wts)
```

```python
import functools

import jax
import jax.numpy as jnp
from jax import lax
from jax.experimental import pallas as pl
from jax.experimental.pallas import tpu as pltpu

D_MODEL = 2048
DEPTH = 4
HEAD_DIM = 128
GRID_W = 64
ROPE_THETA = 10000.0
LN_EPS = 1e-5
RMS_EPS = 1e-6
GLA_HEADS = 4
GLA_DK = 128
GLA_DV = 256
GLA_RANK = 16
GLA_TAU = 16.0
GLA_CHUNK = 64
DIL_CONFIGS = ((128, 1), (512, 4), (2048, 16))
DIL_HEADS = 4
GQA_Q_HEADS = 8
GQA_KV_HEADS = 2
N_EXPERTS = 16
N_EXPERT_GROUPS = 4
EXPERTS_PER_GROUP = N_EXPERTS // N_EXPERT_GROUPS
D_FF_EXPERT = 1024
DEEPNORM_ALPHA = (2.0 * DEPTH) ** 0.25

A_QK = GLA_HEADS * GLA_DK
A_V = GLA_HEADS * GLA_DV
A_LR = 2 * GLA_RANK
B_HEADS = len(DIL_CONFIGS) * DIL_HEADS
B_QKV = 3 * B_HEADS * HEAD_DIM
B_OUT = DIL_HEADS * HEAD_DIM
C_Q = GQA_Q_HEADS * HEAD_DIM
C_KV = GQA_KV_HEADS * HEAD_DIM
GATE_COLS = 3 * D_MODEL
IN_SPLITS = (A_QK, A_QK, A_V, A_V, A_LR, B_QKV, C_Q, C_KV, C_KV, GATE_COLS)

LANES = 128
VMEM_LIMIT = 56 * 1024 * 1024
LR_PAD = LANES

OFF_GATE = 0
OFF_AQ = OFF_GATE + GATE_COLS
OFF_AK = OFF_AQ + A_QK
OFF_AV = OFF_AK + A_QK
OFF_AR = OFF_AV + A_V
OFF_CQ = OFF_AR + A_V
OFF_CK = OFF_CQ + C_Q
OFF_CV = OFF_CK + C_KV
OFF_LR = OFF_CV + C_KV
MAIN_USED = OFF_LR + LR_PAD

TM_PROJ = 1024
TN_MAIN = 1024
TM_ROUTE = 2048
MAIN_COLS = -(-MAIN_USED // TN_MAIN) * TN_MAIN
TN_B = B_HEADS * HEAD_DIM
TM_TOK = 256
TM_MOE = 512
MOE_GATHER_ROWS = 22528
COMBINE_GATHER_ROWS = 16384
GATHER_TABLE_ROWS = 32768
TM_MAIN = 2048
GQA_TQ = 512
GQA_TK = 1024
GQA_GROUP = 2
GQA_STRIP = 32
LOG2_E = 1.4426950408889634
GLA_SUPER = 8
DIL_UNROLL = 16
DIL_HALF = 64
NEG_BIG = -1e30

f32 = jnp.float32
bf16 = jnp.bfloat16


def _cparams(sem):
    return pltpu.CompilerParams(dimension_semantics=("arbitrary",) * len(sem),
                                vmem_limit_bytes=VMEM_LIMIT)


def _nt_dot(a, b):
    return lax.dot_general(a, b, (((1,), (1,)), ((), ())), preferred_element_type=f32)


def _tn_dot(a, b):
    return lax.dot_general(a, b, (((0,), (0,)), ((), ())), preferred_element_type=f32)


def _axial_partner(x):
    lane = lax.broadcasted_iota(jnp.int32, x.shape, 1)
    quarter = HEAD_DIM // 4
    first = (lane % (2 * quarter)) < quarter
    return jnp.where(first, pltpu.roll(x, HEAD_DIM - quarter, 1), pltpu.roll(x, quarter, 1))


def _mm_kernel(x_ref, w_ref, o_ref):
    o_ref[...] = jnp.dot(x_ref[...], w_ref[...], preferred_element_type=f32).astype(o_ref.dtype)


def _proj_main(xb, w_main, layer):
    n, k = xb.shape
    grid = (n // TM_MAIN, MAIN_COLS // TN_MAIN)
    return pl.pallas_call(
        _mm_kernel,
        grid=grid,
        in_specs=[
            pl.BlockSpec((TM_MAIN, k), lambda i, j: (i, 0)),
            pl.BlockSpec((None, None, k, TN_MAIN), lambda i, j: (layer, j, 0, 0)),
        ],
        out_specs=pl.BlockSpec((None, TM_MAIN, TN_MAIN), lambda i, j: (j, i, 0)),
        out_shape=jax.ShapeDtypeStruct((MAIN_COLS // TN_MAIN, n, TN_MAIN), bf16),
        compiler_params=_cparams(("parallel", "parallel")),
    )(xb, w_main)


def _proj_b_kernel(x_ref, w_ref, cos_ref, sin_ref, o_ref):
    j = pl.program_id(0)
    pair = 2 * HEAD_DIM

    def emit(rotary, scale):
        x = x_ref[...]
        if rotary:
            cos = cos_ref[...] * scale
            sin = sin_ref[...] * scale
        for c in range(TN_B // pair):
            acc = jnp.dot(x, w_ref[:, c * pair:(c + 1) * pair], preferred_element_type=f32)
            for h in range(2):
                a = acc[:, h * HEAD_DIM:(h + 1) * HEAD_DIM]
                if rotary:
                    a = a * cos + pltpu.roll(a, HEAD_DIM // 2, 1) * sin
                o_ref[2 * c + h] = a.astype(o_ref.dtype)

    @pl.when(j == 0)
    def _():
        emit(True, HEAD_DIM ** -0.5)

    @pl.when(j == 1)
    def _():
        emit(True, 1.0)

    @pl.when(j == 2)
    def _():
        emit(False, 1.0)


def _proj_b(xb, w_b, cos_b, sin_b, layer, t):
    n, k = xb.shape
    tpb = t // TM_PROJ
    grid = (B_QKV // TN_B, n // TM_PROJ)
    hpt = TN_B // HEAD_DIM
    return pl.pallas_call(
        _proj_b_kernel,
        grid=grid,
        in_specs=[
            pl.BlockSpec((TM_PROJ, k), lambda j, i: (i, 0)),
            pl.BlockSpec((None, None, k, TN_B), lambda j, i: (layer, j, 0, 0)),
            pl.BlockSpec((TM_PROJ, HEAD_DIM), lambda j, i: (i % tpb, 0)),
            pl.BlockSpec((TM_PROJ, HEAD_DIM), lambda j, i: (i % tpb, 0)),
        ],
        out_specs=pl.BlockSpec((hpt, TM_PROJ, HEAD_DIM), lambda j, i: (j, i, 0)),
        out_shape=jax.ShapeDtypeStruct((3 * B_HEADS, n, HEAD_DIM), bf16),
        compiler_params=_cparams(("parallel", "parallel")),
    )(xb, w_b, cos_b, sin_b)


def _gla_kernel(q_ref, k_ref, v_ref, r_ref, lr_ref, w2f_ref, bf_ref, w2b_ref, bb_ref, g_ref,
                o_ref, accf_ref, accb_ref, *, t):
    c_len = GLA_CHUNK
    sup = GLA_SUPER * c_len
    n_sup = t // sup
    row = lax.broadcasted_iota(jnp.int32, (sup, sup), 0)
    col = lax.broadcasted_iota(jnp.int32, (sup, sup), 1)
    same = (row // c_len) == (col // c_len)

    def stage_decay(base, fwd):
        w2 = (w2f_ref if fwd else w2b_ref)[...].astype(bf16)
        bias = (bf_ref if fwd else bb_ref)[...]
        lo = 0 if fwd else GLA_RANK
        lr = lr_ref[pl.ds(base, sup), :][:, lo:lo + GLA_RANK]
        z = jnp.dot(lr, w2, preferred_element_type=f32) + bias
        g = (jnp.minimum(z, 0.0) - jnp.log(1.0 + jnp.exp(-jnp.abs(z)))) * (1.0 / GLA_TAU)
        g_hi = g.astype(bf16)
        g_lo = (g - g_hi.astype(f32)).astype(bf16)
        return g_hi, g_lo

    def stage_cumsum(parts, fwd):
        g_hi, g_lo = parts
        tri = jnp.where(same & ((row >= col) if fwd else (row <= col)), 1.0, 0.0).astype(bf16)
        b = (jnp.dot(tri, g_hi, preferred_element_type=f32)
             + jnp.dot(tri, g_lo, preferred_element_type=f32))
        edge = c_len - 1 if fwd else 0
        tot = jnp.concatenate(
            [jnp.broadcast_to(b[c * c_len + edge:c * c_len + edge + 1, :], (c_len, GLA_DK))
             for c in range(GLA_SUPER)], axis=0)
        return b, tot

    def stage_scale(base, b, tot):
        sl = pl.ds(base, sup)
        q = q_ref[sl, :].astype(f32) * (GLA_DK ** -0.5)
        k = k_ref[sl, :].astype(f32)
        q_t = (q * jnp.exp(b)).astype(bf16)
        k_t = (k * jnp.exp(-b)).astype(bf16)
        k_s = (k * jnp.exp(tot - b)).astype(bf16)
        return q_t, k_t, k_s, jnp.exp(tot)

    def stage_intra(base, q_t, k_t, fwd):
        smask = same & ((col <= row) if fwd else (col > row))
        scores = jnp.where(smask, _nt_dot(q_t, k_t), 0.0)
        return jnp.dot(scores.astype(bf16), v_ref[pl.ds(base, sup), :], preferred_element_type=f32)

    def chunk_step(base, c, fwd, q_t, k_s, dec, o_intra, s_t):
        acc_ref = accf_ref if fwd else accb_ref
        rs = slice(c * c_len, (c + 1) * c_len)
        rows_c = pl.ds(base + c * c_len, c_len)
        acc_ref[rows_c, :] = o_intra[rs, :] + _nt_dot(q_t[rs, :], s_t.astype(bf16))
        return s_t * dec[c * c_len:c * c_len + 1, :] + _tn_dot(v_ref[rows_c, :], k_s[rs, :])

    def body(i, carry):
        s_f, s_b = carry
        base_f = pl.multiple_of(i * sup, sup)
        base_b = pl.multiple_of((n_sup - 1 - i) * sup, sup)
        parts_f = stage_decay(base_f, True)
        parts_b = stage_decay(base_b, False)
        b_f, tot_f = stage_cumsum(parts_f, True)
        b_b, tot_b = stage_cumsum(parts_b, False)
        qt_f, kt_f, ks_f, dec_f = stage_scale(base_f, b_f, tot_f)
        qt_b, kt_b, ks_b, dec_b = stage_scale(base_b, b_b, tot_b)
        oi_f = stage_intra(base_f, qt_f, kt_f, True)
        oi_b = stage_intra(base_b, qt_b, kt_b, False)
        for c in range(GLA_SUPER):
            s_f = chunk_step(base_f, c, True, qt_f, ks_f, dec_f, oi_f, s_f)
            s_b = chunk_step(base_b, GLA_SUPER - 1 - c, False, qt_b, ks_b, dec_b, oi_b, s_b)
        return s_f, s_b

    zero = jnp.zeros((GLA_DV, GLA_DK), f32)
    lax.fori_loop(0, n_sup, body, (zero, zero))

    rows = 256
    gain = g_ref[...]

    def finish(i, carry):
        sl = pl.ds(pl.multiple_of(i * rows, rows), rows)
        x = accf_ref[sl, :] + accb_ref[sl, :]
        ms = jnp.mean(x * x, axis=-1, keepdims=True)
        y = x * lax.rsqrt(ms + RMS_EPS) * gain
        r = r_ref[sl, :].astype(f32)
        o_ref[sl, :] = (y * (r * jax.nn.sigmoid(r))).astype(o_ref.dtype)
        return carry

    lax.fori_loop(0, t // rows, finish, 0)


def _gla(main, w2f, b_f, w2b, b_b, gain, layer, bsz, t):
    n = bsz * t
    blk = lambda off, width: (lambda b, h: (off // TN_MAIN, b, (off % TN_MAIN) // width + h))
    wspec = pl.BlockSpec((None, GLA_RANK, GLA_DK), lambda b, h: (layer, 0, h))
    bspec = pl.BlockSpec((None, 1, GLA_DK), lambda b, h: (layer, 0, h))
    return pl.pallas_call(
        functools.partial(_gla_kernel, t=t),
        grid=(bsz, GLA_HEADS),
        in_specs=[
            pl.BlockSpec((None, t, GLA_DK), blk(OFF_AQ, GLA_DK)),
            pl.BlockSpec((None, t, GLA_DK), blk(OFF_AK, GLA_DK)),
            pl.BlockSpec((None, t, GLA_DV), blk(OFF_AV, GLA_DV)),
            pl.BlockSpec((None, t, GLA_DV), blk(OFF_AR, GLA_DV)),
            pl.BlockSpec((None, t, LR_PAD),
                         lambda b, h: (OFF_LR // TN_MAIN, b, (OFF_LR % TN_MAIN) // LR_PAD)),
            wspec, bspec, wspec, bspec,
            pl.BlockSpec((None, 1, GLA_DV), lambda b, h: (layer, 0, h)),
        ],
        out_specs=pl.BlockSpec((t, GLA_DV), lambda b, h: (b, h)),
        out_shape=jax.ShapeDtypeStruct((n, A_V), bf16),
        scratch_shapes=[pltpu.VMEM((t, GLA_DV), f32), pltpu.VMEM((t, GLA_DV), f32)],
        compiler_params=_cparams(("parallel", "parallel")),
    )(main, main, main, main, main, w2f, b_f, w2b, b_b, gain)


def _dil_kernel(q_ref, k_ref, v_ref, o_ref, lse_ref, *scratch, t, dil):
    n_sub = t // dil
    bq = min(128, n_sub)
    win = min(bq + 2 * DIL_HALF, n_sub)
    n_blk = n_sub // bq
    qi = lax.broadcasted_iota(jnp.int32, (bq, win), 0)
    ki = lax.broadcasted_iota(jnp.int32, (bq, win), 1)
    cvt = 512

    if dil > 1:
        q32, k32, v32, o32 = scratch

        def widen(i, carry):
            sl = pl.ds(pl.multiple_of(i * cvt, cvt), cvt)
            q32[sl, :] = q_ref[0, sl, :].astype(f32)
            k32[sl, :] = k_ref[0, sl, :].astype(f32)
            v32[sl, :] = v_ref[0, sl, :].astype(f32)
            return carry

        lax.fori_loop(0, t // cvt, widen, 0)

    def body(idx, carry):
        r = idx // n_blk
        i = idx % n_blk
        q0 = i * bq
        k0 = jnp.clip(q0 - DIL_HALF, 0, n_sub - win)
        if dil == 1:
            qs = pl.ds(pl.multiple_of(q0, bq), bq)
            ks = pl.ds(pl.multiple_of(k0, DIL_HALF), win)
            q = q_ref[0, qs, :]
            k = k_ref[0, ks, :]
            v = v_ref[0, ks, :]
        else:
            qs = pl.ds(r + q0 * dil, bq, stride=dil)
            ks = pl.ds(r + k0 * dil, win, stride=dil)
            q = q32[qs, :].astype(bf16)
            k = k32[ks, :].astype(bf16)
            v = v32[ks, :].astype(bf16)
        s = _nt_dot(q, k)
        valid = jnp.abs((q0 + qi) - (k0 + ki)) <= DIL_HALF
        s = jnp.where(valid, s, NEG_BIG)
        m = jnp.max(s, axis=-1, keepdims=True)
        p = jnp.exp(s - m)
        l = jnp.sum(p, axis=-1, keepdims=True)
        o = jnp.dot(p.astype(bf16), v, preferred_element_type=f32) / l
        lse = jnp.broadcast_to(m + jnp.log(l), (bq, HEAD_DIM))
        if dil == 1:
            o_ref[0, qs, :] = o.astype(o_ref.dtype)
            lse_ref[0, qs, :] = lse
        else:
            o32[qs, :] = o
            lse_ref[0, qs, :] = lse
        return carry

    def body_group(j, carry):
        for u in range(DIL_UNROLL):
            body(j * DIL_UNROLL + u, carry)
        return carry

    lax.fori_loop(0, dil * n_blk // DIL_UNROLL, body_group, 0)

    if dil > 1:
        def narrow(i, carry):
            sl = pl.ds(pl.multiple_of(i * cvt, cvt), cvt)
            o_ref[0, sl, :] = o32[sl, :].astype(o_ref.dtype)
            return carry

        lax.fori_loop(0, t // cvt, narrow, 0)


def _dilated(qkv_b, group, bsz, t):
    _, dil = DIL_CONFIGS[group]
    n = bsz * t
    blk = (1, t, HEAD_DIM)
    head0 = group * DIL_HEADS
    scratch = [pltpu.VMEM((t, HEAD_DIM), f32)] * 4 if dil > 1 else []
    return pl.pallas_call(
        functools.partial(_dil_kernel, t=t, dil=dil),
        grid=(bsz, DIL_HEADS),
        in_specs=[
            pl.BlockSpec(blk, lambda b, h: (head0 + h, b, 0)),
            pl.BlockSpec(blk, lambda b, h: (B_HEADS + head0 + h, b, 0)),
            pl.BlockSpec(blk, lambda b, h: (2 * B_HEADS + head0 + h, b, 0)),
        ],
        out_specs=[
            pl.BlockSpec(blk, lambda b, h: (h, b, 0)),
            pl.BlockSpec(blk, lambda b, h: (h, b, 0)),
        ],
        out_shape=[
            jax.ShapeDtypeStruct((DIL_HEADS, n, HEAD_DIM), bf16),
            jax.ShapeDtypeStruct((DIL_HEADS, n, HEAD_DIM), f32),
        ],
        scratch_shapes=scratch,
        compiler_params=_cparams(("parallel", "parallel")),
    )(qkv_b, qkv_b, qkv_b)


def _c_prep_kernel(q_ref, k_ref, cos_ref, sin_ref, qg_ref, kg_ref, qo_ref, ko_ref):
    cos = cos_ref[...]
    sin = sin_ref[...]

    def prep(x, gain, scale):
        x = x.astype(f32)
        ms = jnp.mean(x * x, axis=-1, keepdims=True)
        y = x * lax.rsqrt(ms + RMS_EPS) * gain
        return (y * cos + _axial_partner(y) * sin) * scale

    qg = qg_ref[...]
    kg = kg_ref[...]
    for h in range(GQA_Q_HEADS):
        ls = slice(h * HEAD_DIM, (h + 1) * HEAD_DIM)
        qo_ref[:, ls] = prep(q_ref[:, ls], qg, HEAD_DIM ** -0.5 * LOG2_E).astype(qo_ref.dtype)
    for h in range(GQA_KV_HEADS):
        ls = slice(h * HEAD_DIM, (h + 1) * HEAD_DIM)
        ko_ref[:, ls] = prep(k_ref[:, ls], kg, 1.0).astype(ko_ref.dtype)


def _c_prep(main, cos_c, sin_c, q_gain, k_gain, layer, t):
    n = main.shape[1]
    tm = TM_PROJ
    tpb = t // tm
    gspec = pl.BlockSpec((None, 1, HEAD_DIM), lambda i: (layer, 0, 0))
    return pl.pallas_call(
        _c_prep_kernel,
        grid=(n // tm,),
        in_specs=[
            pl.BlockSpec((None, tm, C_Q), lambda i: (OFF_CQ // TN_MAIN, i, (OFF_CQ % TN_MAIN) // C_Q)),
            pl.BlockSpec((None, tm, C_KV), lambda i: (OFF_CK // TN_MAIN, i, (OFF_CK % TN_MAIN) // C_KV)),
            pl.BlockSpec((tm, HEAD_DIM), lambda i: (i % tpb, 0)),
            pl.BlockSpec((tm, HEAD_DIM), lambda i: (i % tpb, 0)),
            gspec, gspec,
        ],
        out_specs=[
            pl.BlockSpec((tm, C_Q), lambda i: (i, 0)),
            pl.BlockSpec((tm, C_KV), lambda i: (i, 0)),
        ],
        out_shape=[
            jax.ShapeDtypeStruct((n, C_Q), bf16),
            jax.ShapeDtypeStruct((n, C_KV), bf16),
        ],
        compiler_params=_cparams(("parallel",)),
    )(main, main, cos_c, sin_c, q_gain, k_gain)


def _gqa_kernel(q_ref, k_ref, v_ref, o_ref, s_scr, p_scr, m_scr, l_scr, acc_scr, *, t, tq, tk):
    grp = GQA_Q_HEADS // GQA_KV_HEADS
    q4 = q_ref[...]
    q = jnp.concatenate([q4[:, h * HEAD_DIM:(h + 1) * HEAD_DIM] for h in range(grp)], axis=0)
    rows = grp * tq
    n_strips = rows // GQA_STRIP
    m_scr[...] = jnp.full((rows, LANES), NEG_BIG, f32)
    l_scr[...] = jnp.zeros((rows, LANES), f32)
    acc_scr[...] = jnp.zeros((rows, HEAD_DIM), f32)

    def widen(col):
        return jnp.broadcast_to(col, (col.shape[0], LANES))

    def softmax_chunk(buf):
        mx = []
        for i in range(n_strips):
            rs = slice(i * GQA_STRIP, (i + 1) * GQA_STRIP)
            mx.append(widen(jnp.max(s_scr[buf, rs, :], axis=-1, keepdims=True)))
        m_old = m_scr[...]
        m_new = jnp.maximum(m_old, jnp.concatenate(mx, axis=0))
        a = jnp.exp2(m_old - m_new)
        m_scr[...] = m_new
        sums = []
        for i in range(n_strips):
            rs = slice(i * GQA_STRIP, (i + 1) * GQA_STRIP)
            m_wide = jnp.concatenate([m_new[rs, :]] * (tk // LANES), axis=1)
            p = jnp.exp2(s_scr[buf, rs, :] - m_wide)
            sums.append(widen(jnp.sum(p, axis=-1, keepdims=True)))
            p_scr[buf, rs, :] = p.astype(bf16)
        l_scr[...] = a * l_scr[...] + jnp.concatenate(sums, axis=0)
        return a

    def chunk_group(cg, carry):
        sls = []
        for u in range(GQA_GROUP):
            sl = pl.ds(pl.multiple_of((cg * GQA_GROUP + u) * tk, tk), tk)
            sls.append(sl)
            s_scr[u] = _nt_dot(q, k_ref[sl, :])
        for u in range(GQA_GROUP):
            a = softmax_chunk(u)
            acc_scr[...] = acc_scr[...] * a + jnp.dot(p_scr[u], v_ref[sls[u], :],
                                                      preferred_element_type=f32)
        return carry

    lax.fori_loop(0, t // (tk * GQA_GROUP), chunk_group, 0)
    o = (acc_scr[...] / l_scr[...]).astype(o_ref.dtype)
    o_ref[...] = jnp.concatenate([o[h * tq:(h + 1) * tq, :] for h in range(grp)], axis=1)


def _gqa(qc, kc, main, bsz, t):
    n = bsz * t
    tq = GQA_TQ
    tk = min(t, GQA_TK)
    grp = GQA_Q_HEADS // GQA_KV_HEADS
    grp_cols = grp * HEAD_DIM
    rows = grp * tq
    nq = t // tq
    return pl.pallas_call(
        functools.partial(_gqa_kernel, t=t, tq=tq, tk=tk),
        grid=(bsz, GQA_KV_HEADS, nq),
        in_specs=[
            pl.BlockSpec((tq, grp_cols), lambda b, j, i: (b * nq + i, j)),
            pl.BlockSpec((t, HEAD_DIM), lambda b, j, i: (b, j)),
            pl.BlockSpec((None, t, HEAD_DIM),
                         lambda b, j, i: (OFF_CV // TN_MAIN, b, (OFF_CV % TN_MAIN) // HEAD_DIM + j)),
        ],
        out_specs=pl.BlockSpec((tq, grp_cols), lambda b, j, i: (b * nq + i, j)),
        out_shape=jax.ShapeDtypeStruct((n, C_Q), bf16),
        scratch_shapes=[
            pltpu.VMEM((GQA_GROUP, rows, tk), f32),
            pltpu.VMEM((GQA_GROUP, rows, tk), bf16),
            pltpu.VMEM((rows, LANES), f32),
            pltpu.VMEM((rows, LANES), f32),
            pltpu.VMEM((rows, HEAD_DIM), f32),
        ],
        compiler_params=_cparams(("parallel", "parallel", "parallel")),
    )(qc, kc, main)


def _merge_kernel(oa_ref, ob0_ref, ob1_ref, ob2_ref, l0_ref, l1_ref, l2_ref, oc_ref,
                  ga_ref, gb_ref, gc_ref, wa_ref, wb_ref, wc_ref, o_ref):
    heads = []
    for h in range(DIL_HEADS):
        l0, l1, l2 = l0_ref[h], l1_ref[h], l2_ref[h]
        m = jnp.maximum(jnp.maximum(l0, l1), l2)
        e0, e1, e2 = jnp.exp(l0 - m), jnp.exp(l1 - m), jnp.exp(l2 - m)
        mix = (e0 * ob0_ref[h].astype(f32) + e1 * ob1_ref[h].astype(f32)
               + e2 * ob2_ref[h].astype(f32)) / (e0 + e1 + e2)
        heads.append(mix.astype(bf16))
    o_b = jnp.concatenate(heads, axis=1)
    up_a = jnp.dot(oa_ref[...], wa_ref[...], preferred_element_type=f32)
    up_b = jnp.dot(o_b, wb_ref[...], preferred_element_type=f32)
    up_c = jnp.dot(oc_ref[...], wc_ref[...], preferred_element_type=f32)
    def gate(ref):
        logits = jnp.concatenate([ref[c] for c in range(ref.shape[0])], axis=1)
        return jax.nn.sigmoid(logits.astype(f32))

    merged = gate(ga_ref) * up_a + gate(gb_ref) * up_b + gate(gc_ref) * up_c
    o_ref[...] = merged.astype(o_ref.dtype)


def _merge(o_a, o_bs, lses, o_c, main, w_up_a, w_up_b, w_up_c, layer):
    n = o_a.shape[0]
    tm = TM_TOK
    tiles_per_gate = D_MODEL // TN_MAIN
    hspec = pl.BlockSpec((DIL_HEADS, tm, HEAD_DIM), lambda i: (0, i, 0))
    gspec = lambda which: pl.BlockSpec((tiles_per_gate, tm, TN_MAIN),
                                       lambda i: (OFF_GATE // D_MODEL + which, i, 0))
    wspec = lambda rows: pl.BlockSpec((None, rows, D_MODEL), lambda i: (layer, 0, 0))
    return pl.pallas_call(
        _merge_kernel,
        grid=(n // tm,),
        in_specs=[
            pl.BlockSpec((tm, A_V), lambda i: (i, 0)),
            hspec, hspec, hspec, hspec, hspec, hspec,
            pl.BlockSpec((tm, C_Q), lambda i: (i, 0)),
            gspec(0), gspec(1), gspec(2),
            wspec(A_V), wspec(B_OUT), wspec(C_Q),
        ],
        out_specs=pl.BlockSpec((tm, D_MODEL), lambda i: (i, 0)),
        out_shape=jax.ShapeDtypeStruct((n, D_MODEL), bf16),
        compiler_params=_cparams(("parallel",)),
    )(o_a, o_bs[0], o_bs[1], o_bs[2], lses[0], lses[1], lses[2], o_c,
      main, main, main, w_up_a, w_up_b, w_up_c)


def _layer_norm_rows(y, g, b):
    mu = jnp.mean(y, axis=-1, keepdims=True)
    d = y - mu
    var = jnp.mean(d * d, axis=-1, keepdims=True)
    return d * lax.rsqrt(var + LN_EPS) * g + b


def _route_rows(logits_t, bias_col):
    scores = jax.nn.sigmoid(logits_t)
    sel = scores + bias_col
    rows = [sel[e:e + 1, :] for e in range(N_EXPERTS)]
    srow = [scores[e:e + 1, :] for e in range(N_EXPERTS)]
    best_val = None
    best_grp = None
    for g in range(N_EXPERT_GROUPS):
        mem = rows[g * EXPERTS_PER_GROUP:(g + 1) * EXPERTS_PER_GROUP]
        top2 = None
        for a in range(EXPERTS_PER_GROUP):
            for b in range(a + 1, EXPERTS_PER_GROUP):
                pair = mem[a] + mem[b]
                top2 = pair if top2 is None else jnp.maximum(top2, pair)
        if g == 0:
            best_val, best_grp = top2, jnp.zeros_like(top2, dtype=jnp.int32)
        else:
            upd = top2 > best_val
            best_val = jnp.where(upd, top2, best_val)
            best_grp = jnp.where(upd, g, best_grp)
    neg = jnp.full_like(best_val, -jnp.inf)
    cand = [jnp.where(best_grp == (e // EXPERTS_PER_GROUP), rows[e], neg) for e in range(N_EXPERTS)]

    def arg_top(vals):
        bv, bi = vals[0], jnp.zeros_like(best_grp)
        for e in range(1, N_EXPERTS):
            upd = vals[e] > bv
            bv = jnp.where(upd, vals[e], bv)
            bi = jnp.where(upd, e, bi)
        return bi

    idx1 = arg_top(cand)
    idx2 = arg_top([jnp.where(idx1 == e, neg, cand[e]) for e in range(N_EXPERTS)])
    zero = jnp.zeros_like(best_val)
    s1 = zero
    s2 = zero
    for e in range(N_EXPERTS):
        s1 = s1 + jnp.where(idx1 == e, srow[e], zero)
        s2 = s2 + jnp.where(idx2 == e, srow[e], zero)
    tot = s1 + s2
    return idx1, idx2, s1 / tot, s2 / tot


def _outproj_kernel(m_ref, w_ref, x_ref, g_ref, b_ref, wr_ref, x1_ref, x1b_ref, lg_ref):
    h = jnp.dot(m_ref[...], w_ref[...], preferred_element_type=f32)
    x1 = _layer_norm_rows(DEEPNORM_ALPHA * x_ref[...] + h, g_ref[...], b_ref[...])
    x1_ref[...] = x1
    x_hi = x1.astype(bf16)
    x1b_ref[...] = x_hi
    x_lo = (x1 - x_hi.astype(f32)).astype(bf16)
    wr = wr_ref[...]
    w_hi = wr.astype(bf16)
    w_lo = (wr - w_hi.astype(f32)).astype(bf16)
    lg_ref[...] = (jnp.dot(x_hi, w_hi, preferred_element_type=f32)
                   + jnp.dot(x_hi, w_lo, preferred_element_type=f32)
                   + jnp.dot(x_lo, w_hi, preferred_element_type=f32))


def _outproj_ln(merged, w_out, x, ln_g, ln_b, w_router_pad, layer):
    n = x.shape[0]
    tm = TM_TOK
    vspec = pl.BlockSpec((None, 1, D_MODEL), lambda i: (layer, 0, 0))
    row = pl.BlockSpec((tm, D_MODEL), lambda i: (i, 0))
    return pl.pallas_call(
        _outproj_kernel,
        grid=(n // tm,),
        in_specs=[
            row,
            pl.BlockSpec((None, D_MODEL, D_MODEL), lambda i: (layer, 0, 0)),
            row,
            vspec, vspec,
            pl.BlockSpec((D_MODEL, LANES), lambda i: (0, 0)),
        ],
        out_specs=[row, row, pl.BlockSpec((tm, LANES), lambda i: (i, 0))],
        out_shape=[
            jax.ShapeDtypeStruct((n, D_MODEL), f32),
            jax.ShapeDtypeStruct((n, D_MODEL), bf16),
            jax.ShapeDtypeStruct((n, LANES), f32),
        ],
        compiler_params=_cparams(("parallel",)),
    )(merged, w_out, x, ln_g, ln_b, w_router_pad)


def _route_kernel(lg_ref, rb_ref, idx_ref, w1_ref, w2_ref):
    logits_t = lg_ref[...].T[0:N_EXPERTS, :]
    i1, i2, w1, w2 = _route_rows(logits_t, rb_ref[...])
    idx_ref[0:1, :] = i1
    idx_ref[1:2, :] = i2
    tr = logits_t.shape[1]
    w1_ref[...] = jnp.broadcast_to(w1, (LANES, tr)).T
    w2_ref[...] = jnp.broadcast_to(w2, (LANES, tr)).T


def _route(logits, router_bias_col):
    n = logits.shape[0]
    tr = TM_ROUTE
    return pl.pallas_call(
        _route_kernel,
        grid=(n // tr,),
        in_specs=[
            pl.BlockSpec((tr, LANES), lambda i: (i, 0)),
            pl.BlockSpec((N_EXPERTS, 1), lambda i: (0, 0)),
        ],
        out_specs=[
            pl.BlockSpec((2, tr), lambda i: (0, i)),
            pl.BlockSpec((tr, LANES), lambda i: (i, 0)),
            pl.BlockSpec((tr, LANES), lambda i: (i, 0)),
        ],
        out_shape=[
            jax.ShapeDtypeStruct((2, n), jnp.int32),
            jax.ShapeDtypeStruct((n, LANES), f32),
            jax.ShapeDtypeStruct((n, LANES), f32),
        ],
        compiler_params=_cparams(("parallel",)),
    )(logits, router_bias_col)


def _moe_kernel(te_ref, nt_ref, *refs, tiles_per_piece):
    del te_ref
    x_refs, (wg_ref, wu_ref, wd_ref, o_ref) = refs[:-4], refs[-4:]
    tile = pl.program_id(0)

    @pl.when(tile < nt_ref[0])
    def _():
        x = x_refs[0][...]
        for p in range(1, len(x_refs)):
            x = jnp.where(tile >= p * tiles_per_piece, x_refs[p][...], x)
        gate = jnp.dot(x, wg_ref[...], preferred_element_type=f32)
        up = jnp.dot(x, wu_ref[...], preferred_element_type=f32)
        h = (gate * jax.nn.sigmoid(gate) * up).astype(bf16)
        o_ref[...] = jnp.dot(h, wd_ref[...], preferred_element_type=f32).astype(o_ref.dtype)

    @pl.when(tile >= nt_ref[0])
    def _():
        o_ref[...] = jnp.zeros_like(o_ref)


def _moe_grouped(x_pieces, tile_expert, n_tiles_used, w_gate, w_up, w_down, layer):
    tpp = x_pieces[0].shape[0] // TM_MOE
    n_tiles = tpp * len(x_pieces)
    wspec_in = pl.BlockSpec((None, None, D_MODEL, D_FF_EXPERT), lambda t, te, nt: (layer, te[t], 0, 0))
    wspec_out = pl.BlockSpec((None, None, D_FF_EXPERT, D_MODEL), lambda t, te, nt: (layer, te[t], 0, 0))

    def piece_spec(p):
        return pl.BlockSpec((TM_MOE, D_MODEL),
                            lambda t, te, nt: (jnp.clip(t - p * tpp, 0, tpp - 1), 0))

    grid_spec = pltpu.PrefetchScalarGridSpec(
        num_scalar_prefetch=2,
        grid=(n_tiles,),
        in_specs=[piece_spec(p) for p in range(len(x_pieces))] + [wspec_in, wspec_in, wspec_out],
        out_specs=pl.BlockSpec((TM_MOE, D_MODEL), lambda t, te, nt: (t, 0)),
    )
    return pl.pallas_call(
        functools.partial(_moe_kernel, tiles_per_piece=tpp),
        grid_spec=grid_spec,
        out_shape=jax.ShapeDtypeStruct((n_tiles * TM_MOE, D_MODEL), bf16),
        compiler_params=_cparams(("arbitrary",)),
    )(tile_expert, n_tiles_used, *x_pieces, w_gate, w_up, w_down)


def _combine_kernel(x_ref, *refs, n_pieces, tiles_per_piece):
    y1_refs, y2_refs = refs[:n_pieces], refs[n_pieces:2 * n_pieces]
    w1_ref, w2_ref, g_ref, b_ref, o_ref, ob_ref = refs[2 * n_pieces:]
    tile = pl.program_id(0)

    def pick(piece_refs):
        y = piece_refs[0][...]
        for p in range(1, n_pieces):
            y = jnp.where(tile >= p * tiles_per_piece, piece_refs[p][...], y)
        return y

    y1 = pick(y1_refs)
    y2 = pick(y2_refs)
    w1 = w1_ref[...]
    w2 = w2_ref[...]
    for c in range(D_MODEL // LANES):
        ls = slice(c * LANES, (c + 1) * LANES)
        o_ref[:, ls] = (DEEPNORM_ALPHA * x_ref[:, ls] + w1 * y1[:, ls].astype(f32)
                        + w2 * y2[:, ls].astype(f32))
    x2 = _layer_norm_rows(o_ref[...], g_ref[...], b_ref[...])
    o_ref[...] = x2
    ob_ref[...] = x2.astype(bf16)


def _combine_ln(x1, y1_pieces, y2_pieces, w1b, w2b, ln_g, ln_b, layer):
    n = x1.shape[0]
    tm = TM_TOK
    n_pieces = len(y1_pieces)
    tpp = y1_pieces[0].shape[0] // tm
    row = pl.BlockSpec((tm, D_MODEL), lambda i: (i, 0))
    wsp = pl.BlockSpec((tm, LANES), lambda i: (i, 0))
    vspec = pl.BlockSpec((None, 1, D_MODEL), lambda i: (layer, 0, 0))

    def piece_spec(p):
        return pl.BlockSpec((tm, D_MODEL), lambda i: (jnp.clip(i - p * tpp, 0, tpp - 1), 0))

    pieces = [piece_spec(p) for p in range(n_pieces)]
    return pl.pallas_call(
        functools.partial(_combine_kernel, n_pieces=n_pieces, tiles_per_piece=tpp),
        grid=(n // tm,),
        in_specs=[row] + pieces + pieces + [wsp, wsp, vspec, vspec],
        out_specs=[row, row],
        out_shape=[jax.ShapeDtypeStruct((n, D_MODEL), f32), jax.ShapeDtypeStruct((n, D_MODEL), bf16)],
        compiler_params=_cparams(("parallel",)),
    )(x1, *y1_pieces, *y2_pieces, w1b, w2b, ln_g, ln_b)


def _dispatch_plan(idx):
    n = idx.shape[1]
    m_pad = 2 * n + N_EXPERTS * TM_MOE
    n_tiles = m_pad // TM_MOE
    e_flat = idx.reshape(-1)
    onehot = (e_flat[:, None] == jnp.arange(N_EXPERTS, dtype=jnp.int32)[None, :]).astype(jnp.int32)
    rank = jnp.sum((jnp.cumsum(onehot, axis=0) - onehot) * onehot, axis=1)
    counts = jnp.sum(onehot, axis=0)
    tiles_per = (counts + TM_MOE - 1) // TM_MOE
    tile_end = jnp.cumsum(tiles_per)
    start_row = (tile_end - tiles_per) * TM_MOE
    dest = jnp.sum(onehot * start_row[None, :], axis=1) + rank
    tok = jnp.tile(jnp.arange(n, dtype=jnp.int32), 2)
    src = jnp.zeros((m_pad,), jnp.int32).at[dest].set(tok, mode="promise_in_bounds", unique_indices=True)
    n_used = tile_end[-1]
    tile_ids = jnp.minimum(jnp.arange(n_tiles, dtype=jnp.int32), n_used - 1)
    tile_expert = jnp.sum((tile_ids[:, None] >= tile_end[None, :]).astype(jnp.int32), axis=1)
    tile_expert = jnp.minimum(tile_expert, N_EXPERTS - 1)
    return src, dest, tile_expert, n_used.reshape(1).astype(jnp.int32)


def _rope_tables(pos, dim):
    inv_freq = ROPE_THETA ** (-jnp.arange(0, dim, 2, dtype=f32) / dim)
    ang = pos.astype(f32)[:, None] * inv_freq[None, :]
    return jnp.cos(ang), jnp.sin(ang)


def _tables(t):
    cos, sin = _rope_tables(jnp.arange(t), HEAD_DIM)
    cos_b = jnp.concatenate([cos, cos], axis=-1)
    sin_b = jnp.concatenate([-sin, sin], axis=-1)
    tok = jnp.arange(t)
    cr, sr = _rope_tables(tok // GRID_W, HEAD_DIM // 2)
    cc, sc = _rope_tables(tok % GRID_W, HEAD_DIM // 2)
    cos_c = jnp.concatenate([cr, cr, cc, cc], axis=-1)
    sin_c = jnp.concatenate([-sr, sr, -sc, sc], axis=-1)
    return cos_b, sin_b, cos_c, sin_c


def _repack_w_in(w_in):
    offs = [0]
    for s in IN_SPLITS:
        offs.append(offs[-1] + s)
    seg = [w_in[:, :, offs[i]:offs[i + 1]] for i in range(len(IN_SPLITS))]
    a_q, a_k, a_v, a_r, a_lr, b_qkv, c_q, c_k, c_v, gate = seg
    lr_pad = jnp.pad(a_lr, ((0, 0), (0, 0), (0, LR_PAD - A_LR)))
    w_main = jnp.concatenate([gate, a_q, a_k, a_v, a_r, c_q, c_k, c_v, lr_pad], axis=-1)
    w_main = jnp.pad(w_main, ((0, 0), (0, 0), (0, MAIN_COLS - MAIN_USED))).astype(bf16)
    return _tile_major(w_main, TN_MAIN), _tile_major(b_qkv.astype(bf16), TN_B)


def _tile_major(w, tn):
    n_layers, k, cols = w.shape
    return w.reshape(n_layers, k, cols // tn, tn).transpose(0, 2, 1, 3)


def _mixer(xb, wts, layer, bsz, t, tables):
    cos_b, sin_b, cos_c, sin_c = tables
    main = _proj_main(xb, wts["w_main"], layer)
    qkv_b = _proj_b(xb, wts["w_b"], cos_b, sin_b, layer, t)
    o_a = _gla(main, wts["gla_w2_f"], wts["gla_b_f"], wts["gla_w2_b"], wts["gla_b_b"],
               wts["gla_norm_g"], layer, bsz, t)
    o_bs, lses = [], []
    for group in range(len(DIL_CONFIGS)):
        o_g, lse_g = _dilated(qkv_b, group, bsz, t)
        o_bs.append(o_g)
        lses.append(lse_g)
    qc, kc = _c_prep(main, cos_c, sin_c, wts["q_norm_g"], wts["k_norm_g"], layer, t)
    o_c = _gqa(qc, kc, main, bsz, t)
    return _merge(o_a, o_bs, lses, o_c, main, wts["w_up_a"], wts["w_up_b"], wts["w_up_c"], layer)


def _gather_rows(a, rows):
    return a.at[rows].get(mode="promise_in_bounds")


class _Trunk:
    def __init__(self, x3, wts):
        self.bsz, self.t, _ = x3.shape
        self.wts = wts
        self.x = x3.reshape(self.bsz * self.t, D_MODEL)
        self.xb = self.x.astype(bf16)
        self.tables = _tables(self.t)

    def mix_and_route(self, layer):
        wts = self.wts
        merged = _mixer(self.xb, wts, layer, self.bsz, self.t, self.tables)
        self.x1, x1b, logits = _outproj_ln(merged, wts["w_out"], self.x, wts["ln1_g"], wts["ln1_b"],
                                           wts["w_router_pad"], layer)
        idx, self.w1b, self.w2b = _route(logits, wts["router_bias_col"])
        src, self.dest, self.tile_expert, self.n_used = _dispatch_plan(idx)
        n_tiles = src.shape[0] // TM_MOE
        max_rows = min(MOE_GATHER_ROWS, x1b.shape[0])
        n_pieces = next(d for d in range(1, n_tiles + 1)
                        if n_tiles % d == 0 and src.shape[0] // d <= max_rows)
        copies = -(-GATHER_TABLE_ROWS // x1b.shape[0])
        table = x1b if copies == 1 else jnp.concatenate([x1b] * copies, axis=0)
        self.x_pieces = [_gather_rows(table, piece) for piece in jnp.split(src, n_pieces)]

    def experts(self, layer):
        wts = self.wts
        n = self.x1.shape[0]
        y_sorted = _moe_grouped(self.x_pieces, self.tile_expert, self.n_used, wts["moe_w_gate"],
                                wts["moe_w_up"], wts["moe_w_down"], layer)
        c_pieces = -(-n // COMBINE_GATHER_ROWS)
        self.y1 = [_gather_rows(y_sorted, piece) for piece in jnp.split(self.dest[:n], c_pieces)]
        self.y2 = [_gather_rows(y_sorted, piece) for piece in jnp.split(self.dest[n:], c_pieces)]

    def combine(self, layer):
        self.x, self.xb = _combine_ln(self.x1, self.y1, self.y2, self.w1b, self.w2b,
                                      self.wts["ln2_g"], self.wts["ln2_b"], layer)

    def result(self):
        return self.x.reshape(self.bsz, self.t, D_MODEL)


def _run_trunks(inputs, wts, depth=DEPTH):
    trunks = [_Trunk(x3, wts) for x3 in inputs]
    for layer in range(depth):
        for stage in (_Trunk.mix_and_route, _Trunk.experts, _Trunk.combine):
            for trunk in trunks:
                stage(trunk, layer)
    return tuple(trunk.result() for trunk in trunks)


def _cast_kernel(x_ref, o_ref):
    o_ref[...] = x_ref[...].astype(o_ref.dtype)


def _expert_weights_bf16(w):
    n_layers, n_exp, rows, cols = w.shape
    spec = pl.BlockSpec((None, None, rows, cols), lambda i, j: (i, j, 0, 0))
    return pl.pallas_call(
        _cast_kernel,
        grid=(n_layers, n_exp),
        in_specs=[spec],
        out_specs=spec,
        out_shape=jax.ShapeDtypeStruct(w.shape, bf16),
        compiler_params=_cparams(("parallel", "parallel")),
    )(w)


def _prepare_weights(w_in, gla_w2_f, gla_b_f, gla_w2_b, gla_b_b, gla_norm_g, q_norm_g, k_norm_g,
                     w_up_a, w_up_b, w_up_c, w_out, ln1_g, ln1_b, w_router, router_bias,
                     moe_w_gate, moe_w_up, moe_w_down, ln2_g, ln2_b):
    w_main, w_b = _repack_w_in(w_in)
    row = lambda a: a.reshape(DEPTH, 1, a.shape[-1]).astype(f32)
    return {
        "w_main": w_main,
        "w_b": w_b,
        "gla_w2_f": gla_w2_f.astype(f32),
        "gla_b_f": row(gla_b_f),
        "gla_w2_b": gla_w2_b.astype(f32),
        "gla_b_b": row(gla_b_b),
        "gla_norm_g": row(gla_norm_g),
        "q_norm_g": row(q_norm_g),
        "k_norm_g": row(k_norm_g),
        "w_up_a": w_up_a.astype(bf16),
        "w_up_b": w_up_b.astype(bf16),
        "w_up_c": w_up_c.astype(bf16),
        "w_out": w_out.astype(bf16),
        "ln1_g": row(ln1_g),
        "ln1_b": row(ln1_b),
        "w_router_pad": jnp.pad(w_router.astype(f32), ((0, 0), (0, LANES - N_EXPERTS))),
        "router_bias_col": router_bias.astype(f32).reshape(N_EXPERTS, 1),
        "moe_w_gate": _expert_weights_bf16(moe_w_gate.astype(f32)),
        "moe_w_up": _expert_weights_bf16(moe_w_up.astype(f32)),
        "moe_w_down": _expert_weights_bf16(moe_w_down.astype(f32)),
        "ln2_g": row(ln2_g),
        "ln2_b": row(ln2_b),
    }


def kernel(x_prompt, x_sample, w_in, gla_w2_f, gla_b_f, gla_w2_b, gla_b_b, gla_norm_g, q_norm_g, k_norm_g, w_up_a, w_up_b, w_up_c, w_out, ln1_g, ln1_b, w_router, router_bias, moe_w_gate, moe_w_up, moe_w_down, ln2_g, ln2_b):
    wts = _prepare_weights(w_in, gla_w2_f, gla_b_f, gla_w2_b, gla_b_b, gla_norm_g, q_norm_g, k_norm_g,
                           w_up_a, w_up_b, w_up_c, w_out, ln1_g, ln1_b, w_router, router_bias,
                           moe_w_gate, moe_w_up, moe_w_down, ln2_g, ln2_b)
    return _run_trunks((x_prompt, x_sample), wts)
```

```python
import functools

import jax
import jax.numpy as jnp
from jax import lax
from jax.experimental import pallas as pl
from jax.experimental.pallas import tpu as pltpu

D_MODEL = 2048
DEPTH = 4
HEAD_DIM = 128
GRID_W = 64
ROPE_THETA = 10000.0
LN_EPS = 1e-5
RMS_EPS = 1e-6
GLA_HEADS = 4
GLA_DK = 128
GLA_DV = 256
GLA_RANK = 16
GLA_TAU = 16.0
GLA_CHUNK = 64
DIL_CONFIGS = ((128, 1), (512, 4), (2048, 16))
DIL_HEADS = 4
GQA_Q_HEADS = 8
GQA_KV_HEADS = 2
N_EXPERTS = 16
N_EXPERT_GROUPS = 4
EXPERTS_PER_GROUP = N_EXPERTS // N_EXPERT_GROUPS
D_FF_EXPERT = 1024
DEEPNORM_ALPHA = (2.0 * DEPTH) ** 0.25

A_QK = GLA_HEADS * GLA_DK
A_V = GLA_HEADS * GLA_DV
A_LR = 2 * GLA_RANK
B_HEADS = len(DIL_CONFIGS) * DIL_HEADS
B_QKV = 3 * B_HEADS * HEAD_DIM
B_OUT = DIL_HEADS * HEAD_DIM
C_Q = GQA_Q_HEADS * HEAD_DIM
C_KV = GQA_KV_HEADS * HEAD_DIM
GATE_COLS = 3 * D_MODEL
IN_SPLITS = (A_QK, A_QK, A_V, A_V, A_LR, B_QKV, C_Q, C_KV, C_KV, GATE_COLS)

LANES = 128
VMEM_LIMIT = 56 * 1024 * 1024
LR_PAD = LANES

OFF_GATE = 0
OFF_AQ = OFF_GATE + GATE_COLS
OFF_AK = OFF_AQ + A_QK
OFF_AV = OFF_AK + A_QK
OFF_AR = OFF_AV + A_V
OFF_CQ = OFF_AR + A_V
OFF_CK = OFF_CQ + C_Q
OFF_CV = OFF_CK + C_KV
OFF_LR = OFF_CV + C_KV
MAIN_USED = OFF_LR + LR_PAD

TM_PROJ = 1024
TN_MAIN = 1024
TM_ROUTE = 2048
MAIN_COLS = -(-MAIN_USED // TN_MAIN) * TN_MAIN
TN_B = B_HEADS * HEAD_DIM
TM_TOK = 256
TM_COMBINE = 512
TM_MOE = 512
MOE_GATHER_ROWS = 22528
COMBINE_GATHER_ROWS = 16384
GATHER_TABLE_ROWS = 32768
TM_MAIN = 2048
GQA_TQ = 512
GQA_TK = 1024
GQA_GROUP = 2
GQA_STRIP = 32
LOG2_E = 1.4426950408889634
GLA_SUPER = 8
DIL_UNROLL = 16
DIL_HALF = 64
NEG_BIG = -1e30

f32 = jnp.float32
bf16 = jnp.bfloat16


def _cparams(sem):
    return pltpu.CompilerParams(dimension_semantics=("arbitrary",) * len(sem),
                                vmem_limit_bytes=VMEM_LIMIT)


def _nt_dot(a, b):
    return lax.dot_general(a, b, (((1,), (1,)), ((), ())), preferred_element_type=f32)


def _tn_dot(a, b):
    return lax.dot_general(a, b, (((0,), (0,)), ((), ())), preferred_element_type=f32)


def _axial_partner(x):
    lane = lax.broadcasted_iota(jnp.int32, x.shape, 1)
    quarter = HEAD_DIM // 4
    first = (lane % (2 * quarter)) < quarter
    return jnp.where(first, pltpu.roll(x, HEAD_DIM - quarter, 1), pltpu.roll(x, quarter, 1))


def _mm_kernel(x_ref, w_ref, o_ref):
    o_ref[...] = jnp.dot(x_ref[...], w_ref[...], preferred_element_type=f32).astype(o_ref.dtype)


def _proj_main(xb, w_main, layer):
    n, k = xb.shape
    grid = (n // TM_MAIN, MAIN_COLS // TN_MAIN)
    return pl.pallas_call(
        _mm_kernel,
        grid=grid,
        in_specs=[
            pl.BlockSpec((TM_MAIN, k), lambda i, j: (i, 0)),
            pl.BlockSpec((None, None, k, TN_MAIN), lambda i, j: (layer, j, 0, 0)),
        ],
        out_specs=pl.BlockSpec((None, TM_MAIN, TN_MAIN), lambda i, j: (j, i, 0)),
        out_shape=jax.ShapeDtypeStruct((MAIN_COLS // TN_MAIN, n, TN_MAIN), bf16),
        compiler_params=_cparams(("parallel", "parallel")),
    )(xb, w_main)


def _proj_b_kernel(x_ref, w_ref, cos_ref, sin_ref, o_ref):
    j = pl.program_id(0)
    pair = 2 * HEAD_DIM

    def emit(rotary, scale):
        x = x_ref[...]
        if rotary:
            cos = cos_ref[...] * scale
            sin = sin_ref[...] * scale
        for c in range(TN_B // pair):
            acc = jnp.dot(x, w_ref[:, c * pair:(c + 1) * pair], preferred_element_type=f32)
            for h in range(2):
                a = acc[:, h * HEAD_DIM:(h + 1) * HEAD_DIM]
                if rotary:
                    a = a * cos + pltpu.roll(a, HEAD_DIM // 2, 1) * sin
                o_ref[2 * c + h] = a.astype(o_ref.dtype)

    @pl.when(j == 0)
    def _():
        emit(True, HEAD_DIM ** -0.5)

    @pl.when(j == 1)
    def _():
        emit(True, 1.0)

    @pl.when(j == 2)
    def _():
        emit(False, 1.0)


def _proj_b(xb, w_b, cos_b, sin_b, layer, t):
    n, k = xb.shape
    tpb = t // TM_PROJ
    grid = (B_QKV // TN_B, n // TM_PROJ)
    hpt = TN_B // HEAD_DIM
    return pl.pallas_call(
        _proj_b_kernel,
        grid=grid,
        in_specs=[
            pl.BlockSpec((TM_PROJ, k), lambda j, i: (i, 0)),
            pl.BlockSpec((None, None, k, TN_B), lambda j, i: (layer, j, 0, 0)),
            pl.BlockSpec((TM_PROJ, HEAD_DIM), lambda j, i: (i % tpb, 0)),
            pl.BlockSpec((TM_PROJ, HEAD_DIM), lambda j, i: (i % tpb, 0)),
        ],
        out_specs=pl.BlockSpec((hpt, TM_PROJ, HEAD_DIM), lambda j, i: (j, i, 0)),
        out_shape=jax.ShapeDtypeStruct((3 * B_HEADS, n, HEAD_DIM), bf16),
        compiler_params=_cparams(("parallel", "parallel")),
    )(xb, w_b, cos_b, sin_b)


def _gla_kernel(q_ref, k_ref, v_ref, r_ref, lr_ref, w2f_ref, bf_ref, w2b_ref, bb_ref, g_ref,
                o_ref, accf_ref, accb_ref, *, t):
    c_len = GLA_CHUNK
    sup = GLA_SUPER * c_len
    n_sup = t // sup
    row = lax.broadcasted_iota(jnp.int32, (sup, sup), 0)
    col = lax.broadcasted_iota(jnp.int32, (sup, sup), 1)
    same = (row // c_len) == (col // c_len)

    def stage_decay(base, fwd):
        w2 = (w2f_ref if fwd else w2b_ref)[...].astype(bf16)
        bias = (bf_ref if fwd else bb_ref)[...]
        lo = 0 if fwd else GLA_RANK
        lr = lr_ref[pl.ds(base, sup), :][:, lo:lo + GLA_RANK]
        z = jnp.dot(lr, w2, preferred_element_type=f32) + bias
        g = (jnp.minimum(z, 0.0) - jnp.log(1.0 + jnp.exp(-jnp.abs(z)))) * (1.0 / GLA_TAU)
        g_hi = g.astype(bf16)
        g_lo = (g - g_hi.astype(f32)).astype(bf16)
        return g_hi, g_lo

    def stage_cumsum(parts, fwd):
        g_hi, g_lo = parts
        tri = jnp.where(same & ((row >= col) if fwd else (row <= col)), 1.0, 0.0).astype(bf16)
        b = (jnp.dot(tri, g_hi, preferred_element_type=f32)
             + jnp.dot(tri, g_lo, preferred_element_type=f32))
        edge = c_len - 1 if fwd else 0
        tot = jnp.concatenate(
            [jnp.broadcast_to(b[c * c_len + edge:c * c_len + edge + 1, :], (c_len, GLA_DK))
             for c in range(GLA_SUPER)], axis=0)
        return b, tot

    def stage_scale(base, b, tot):
        sl = pl.ds(base, sup)
        q = q_ref[sl, :].astype(f32) * (GLA_DK ** -0.5)
        k = k_ref[sl, :].astype(f32)
        q_t = (q * jnp.exp(b)).astype(bf16)
        k_t = (k * jnp.exp(-b)).astype(bf16)
        k_s = (k * jnp.exp(tot - b)).astype(bf16)
        return q_t, k_t, k_s, jnp.exp(tot)

    def stage_intra(base, q_t, k_t, fwd):
        smask = same & ((col <= row) if fwd else (col > row))
        scores = jnp.where(smask, _nt_dot(q_t, k_t), 0.0)
        return jnp.dot(scores.astype(bf16), v_ref[pl.ds(base, sup), :], preferred_element_type=f32)

    def chunk_step(base, c, fwd, q_t, k_s, dec, o_intra, s_t):
        acc_ref = accf_ref if fwd else accb_ref
        rs = slice(c * c_len, (c + 1) * c_len)
        rows_c = pl.ds(base + c * c_len, c_len)
        acc_ref[rows_c, :] = o_intra[rs, :] + _nt_dot(q_t[rs, :], s_t.astype(bf16))
        return s_t * dec[c * c_len:c * c_len + 1, :] + _tn_dot(v_ref[rows_c, :], k_s[rs, :])

    def body(i, carry):
        s_f, s_b = carry
        base_f = pl.multiple_of(i * sup, sup)
        base_b = pl.multiple_of((n_sup - 1 - i) * sup, sup)
        parts_f = stage_decay(base_f, True)
        parts_b = stage_decay(base_b, False)
        b_f, tot_f = stage_cumsum(parts_f, True)
        b_b, tot_b = stage_cumsum(parts_b, False)
        qt_f, kt_f, ks_f, dec_f = stage_scale(base_f, b_f, tot_f)
        qt_b, kt_b, ks_b, dec_b = stage_scale(base_b, b_b, tot_b)
        oi_f = stage_intra(base_f, qt_f, kt_f, True)
        oi_b = stage_intra(base_b, qt_b, kt_b, False)
        for c in range(GLA_SUPER):
            s_f = chunk_step(base_f, c, True, qt_f, ks_f, dec_f, oi_f, s_f)
            s_b = chunk_step(base_b, GLA_SUPER - 1 - c, False, qt_b, ks_b, dec_b, oi_b, s_b)
        return s_f, s_b

    zero = jnp.zeros((GLA_DV, GLA_DK), f32)
    lax.fori_loop(0, n_sup, body, (zero, zero))

    rows = 256
    gain = g_ref[...]

    def finish(i, carry):
        sl = pl.ds(pl.multiple_of(i * rows, rows), rows)
        x = accf_ref[sl, :] + accb_ref[sl, :]
        ms = jnp.mean(x * x, axis=-1, keepdims=True)
        y = x * lax.rsqrt(ms + RMS_EPS) * gain
        r = r_ref[sl, :].astype(f32)
        o_ref[sl, :] = (y * (r * jax.nn.sigmoid(r))).astype(o_ref.dtype)
        return carry

    lax.fori_loop(0, t // rows, finish, 0)


def _gla(main, w2f, b_f, w2b, b_b, gain, layer, bsz, t):
    n = bsz * t
    blk = lambda off, width: (lambda b, h: (off // TN_MAIN, b, (off % TN_MAIN) // width + h))
    wspec = pl.BlockSpec((None, GLA_RANK, GLA_DK), lambda b, h: (layer, 0, h))
    bspec = pl.BlockSpec((None, 1, GLA_DK), lambda b, h: (layer, 0, h))
    return pl.pallas_call(
        functools.partial(_gla_kernel, t=t),
        grid=(bsz, GLA_HEADS),
        in_specs=[
            pl.BlockSpec((None, t, GLA_DK), blk(OFF_AQ, GLA_DK)),
            pl.BlockSpec((None, t, GLA_DK), blk(OFF_AK, GLA_DK)),
            pl.BlockSpec((None, t, GLA_DV), blk(OFF_AV, GLA_DV)),
            pl.BlockSpec((None, t, GLA_DV), blk(OFF_AR, GLA_DV)),
            pl.BlockSpec((None, t, LR_PAD),
                         lambda b, h: (OFF_LR // TN_MAIN, b, (OFF_LR % TN_MAIN) // LR_PAD)),
            wspec, bspec, wspec, bspec,
            pl.BlockSpec((None, 1, GLA_DV), lambda b, h: (layer, 0, h)),
        ],
        out_specs=pl.BlockSpec((t, GLA_DV), lambda b, h: (b, h)),
        out_shape=jax.ShapeDtypeStruct((n, A_V), bf16),
        scratch_shapes=[pltpu.VMEM((t, GLA_DV), f32), pltpu.VMEM((t, GLA_DV), f32)],
        compiler_params=_cparams(("parallel", "parallel")),
    )(main, main, main, main, main, w2f, b_f, w2b, b_b, gain)


def _dil_kernel(q_ref, k_ref, v_ref, o_ref, lse_ref, *scratch, t, dil):
    n_sub = t // dil
    bq = min(128, n_sub)
    win = min(bq + 2 * DIL_HALF, n_sub)
    n_blk = n_sub // bq
    qi = lax.broadcasted_iota(jnp.int32, (bq, win), 0)
    ki = lax.broadcasted_iota(jnp.int32, (bq, win), 1)
    cvt = 512

    if dil > 1:
        q32, k32, v32, o32 = scratch

        def widen(i, carry):
            sl = pl.ds(pl.multiple_of(i * cvt, cvt), cvt)
            q32[sl, :] = q_ref[0, sl, :].astype(f32)
            k32[sl, :] = k_ref[0, sl, :].astype(f32)
            v32[sl, :] = v_ref[0, sl, :].astype(f32)
            return carry

        lax.fori_loop(0, t // cvt, widen, 0)

    def body(idx, carry):
        r = idx // n_blk
        i = idx % n_blk
        q0 = i * bq
        k0 = jnp.clip(q0 - DIL_HALF, 0, n_sub - win)
        if dil == 1:
            qs = pl.ds(pl.multiple_of(q0, bq), bq)
            ks = pl.ds(pl.multiple_of(k0, DIL_HALF), win)
            q = q_ref[0, qs, :]
            k = k_ref[0, ks, :]
            v = v_ref[0, ks, :]
        else:
            qs = pl.ds(r + q0 * dil, bq, stride=dil)
            ks = pl.ds(r + k0 * dil, win, stride=dil)
            q = q32[qs, :].astype(bf16)
            k = k32[ks, :].astype(bf16)
            v = v32[ks, :].astype(bf16)
        s = _nt_dot(q, k)
        valid = jnp.abs((q0 + qi) - (k0 + ki)) <= DIL_HALF
        s = jnp.where(valid, s, NEG_BIG)
        m = jnp.max(s, axis=-1, keepdims=True)
        p = jnp.exp(s - m)
        l = jnp.sum(p, axis=-1, keepdims=True)
        o = jnp.dot(p.astype(bf16), v, preferred_element_type=f32) / l
        lse = jnp.broadcast_to(m + jnp.log(l), (bq, HEAD_DIM))
        if dil == 1:
            o_ref[0, qs, :] = o.astype(o_ref.dtype)
            lse_ref[0, qs, :] = lse
        else:
            o32[qs, :] = o
            lse_ref[0, qs, :] = lse
        return carry

    def body_group(j, carry):
        for u in range(DIL_UNROLL):
            body(j * DIL_UNROLL + u, carry)
        return carry

    lax.fori_loop(0, dil * n_blk // DIL_UNROLL, body_group, 0)

    if dil > 1:
        def narrow(i, carry):
            sl = pl.ds(pl.multiple_of(i * cvt, cvt), cvt)
            o_ref[0, sl, :] = o32[sl, :].astype(o_ref.dtype)
            return carry

        lax.fori_loop(0, t // cvt, narrow, 0)


def _dilated(qkv_b, group, bsz, t):
    _, dil = DIL_CONFIGS[group]
    n = bsz * t
    blk = (1, t, HEAD_DIM)
    head0 = group * DIL_HEADS
    scratch = [pltpu.VMEM((t, HEAD_DIM), f32)] * 4 if dil > 1 else []
    return pl.pallas_call(
        functools.partial(_dil_kernel, t=t, dil=dil),
        grid=(bsz, DIL_HEADS),
        in_specs=[
            pl.BlockSpec(blk, lambda b, h: (head0 + h, b, 0)),
            pl.BlockSpec(blk, lambda b, h: (B_HEADS + head0 + h, b, 0)),
            pl.BlockSpec(blk, lambda b, h: (2 * B_HEADS + head0 + h, b, 0)),
        ],
        out_specs=[
            pl.BlockSpec(blk, lambda b, h: (h, b, 0)),
            pl.BlockSpec(blk, lambda b, h: (h, b, 0)),
        ],
        out_shape=[
            jax.ShapeDtypeStruct((DIL_HEADS, n, HEAD_DIM), bf16),
            jax.ShapeDtypeStruct((DIL_HEADS, n, HEAD_DIM), f32),
        ],
        scratch_shapes=scratch,
        compiler_params=_cparams(("parallel", "parallel")),
    )(qkv_b, qkv_b, qkv_b)


def _c_prep_kernel(q_ref, k_ref, cos_ref, sin_ref, qg_ref, kg_ref, qo_ref, ko_ref):
    cos = cos_ref[...]
    sin = sin_ref[...]

    def prep(x, gain, scale):
        x = x.astype(f32)
        ms = jnp.mean(x * x, axis=-1, keepdims=True)
        y = x * lax.rsqrt(ms + RMS_EPS) * gain
        return (y * cos + _axial_partner(y) * sin) * scale

    qg = qg_ref[...]
    kg = kg_ref[...]
    for h in range(GQA_Q_HEADS):
        ls = slice(h * HEAD_DIM, (h + 1) * HEAD_DIM)
        qo_ref[:, ls] = prep(q_ref[:, ls], qg, HEAD_DIM ** -0.5 * LOG2_E).astype(qo_ref.dtype)
    for h in range(GQA_KV_HEADS):
        ls = slice(h * HEAD_DIM, (h + 1) * HEAD_DIM)
        ko_ref[:, ls] = prep(k_ref[:, ls], kg, 1.0).astype(ko_ref.dtype)


def _c_prep(main, cos_c, sin_c, q_gain, k_gain, layer, t):
    n = main.shape[1]
    tm = TM_PROJ
    tpb = t // tm
    gspec = pl.BlockSpec((None, 1, HEAD_DIM), lambda i: (layer, 0, 0))
    return pl.pallas_call(
        _c_prep_kernel,
        grid=(n // tm,),
        in_specs=[
            pl.BlockSpec((None, tm, C_Q), lambda i: (OFF_CQ // TN_MAIN, i, (OFF_CQ % TN_MAIN) // C_Q)),
            pl.BlockSpec((None, tm, C_KV), lambda i: (OFF_CK // TN_MAIN, i, (OFF_CK % TN_MAIN) // C_KV)),
            pl.BlockSpec((tm, HEAD_DIM), lambda i: (i % tpb, 0)),
            pl.BlockSpec((tm, HEAD_DIM), lambda i: (i % tpb, 0)),
            gspec, gspec,
        ],
        out_specs=[
            pl.BlockSpec((tm, C_Q), lambda i: (i, 0)),
            pl.BlockSpec((tm, C_KV), lambda i: (i, 0)),
        ],
        out_shape=[
            jax.ShapeDtypeStruct((n, C_Q), bf16),
            jax.ShapeDtypeStruct((n, C_KV), bf16),
        ],
        compiler_params=_cparams(("parallel",)),
    )(main, main, cos_c, sin_c, q_gain, k_gain)


def _gqa_kernel(q_ref, k_ref, v_ref, o_ref, s_scr, p_scr, m_scr, l_scr, acc_scr, *, t, tq, tk):
    grp = GQA_Q_HEADS // GQA_KV_HEADS
    q4 = q_ref[...]
    q = jnp.concatenate([q4[:, h * HEAD_DIM:(h + 1) * HEAD_DIM] for h in range(grp)], axis=0)
    rows = grp * tq
    n_strips = rows // GQA_STRIP
    m_scr[...] = jnp.full((rows, LANES), NEG_BIG, f32)
    l_scr[...] = jnp.zeros((rows, LANES), f32)
    acc_scr[...] = jnp.zeros((rows, HEAD_DIM), f32)

    def widen(col):
        return jnp.broadcast_to(col, (col.shape[0], LANES))

    def softmax_chunk(buf):
        mx = []
        for i in range(n_strips):
            rs = slice(i * GQA_STRIP, (i + 1) * GQA_STRIP)
            mx.append(widen(jnp.max(s_scr[buf, rs, :], axis=-1, keepdims=True)))
        m_old = m_scr[...]
        m_new = jnp.maximum(m_old, jnp.concatenate(mx, axis=0))
        a = jnp.exp2(m_old - m_new)
        m_scr[...] = m_new
        sums = []
        for i in range(n_strips):
            rs = slice(i * GQA_STRIP, (i + 1) * GQA_STRIP)
            m_wide = jnp.concatenate([m_new[rs, :]] * (tk // LANES), axis=1)
            p = jnp.exp2(s_scr[buf, rs, :] - m_wide)
            sums.append(widen(jnp.sum(p, axis=-1, keepdims=True)))
            p_scr[buf, rs, :] = p.astype(bf16)
        l_scr[...] = a * l_scr[...] + jnp.concatenate(sums, axis=0)
        return a

    def chunk_group(cg, carry):
        sls = []
        for u in range(GQA_GROUP):
            sl = pl.ds(pl.multiple_of((cg * GQA_GROUP + u) * tk, tk), tk)
            sls.append(sl)
            s_scr[u] = _nt_dot(q, k_ref[sl, :])
        for u in range(GQA_GROUP):
            a = softmax_chunk(u)
            acc_scr[...] = acc_scr[...] * a + jnp.dot(p_scr[u], v_ref[sls[u], :],
                                                      preferred_element_type=f32)
        return carry

    lax.fori_loop(0, t // (tk * GQA_GROUP), chunk_group, 0)
    o = (acc_scr[...] / l_scr[...]).astype(o_ref.dtype)
    o_ref[...] = jnp.concatenate([o[h * tq:(h + 1) * tq, :] for h in range(grp)], axis=1)


def _gqa(qc, kc, main, bsz, t):
    n = bsz * t
    tq = GQA_TQ
    tk = min(t, GQA_TK)
    grp = GQA_Q_HEADS // GQA_KV_HEADS
    grp_cols = grp * HEAD_DIM
    rows = grp * tq
    nq = t // tq
    return pl.pallas_call(
        functools.partial(_gqa_kernel, t=t, tq=tq, tk=tk),
        grid=(bsz, GQA_KV_HEADS, nq),
        in_specs=[
            pl.BlockSpec((tq, grp_cols), lambda b, j, i: (b * nq + i, j)),
            pl.BlockSpec((t, HEAD_DIM), lambda b, j, i: (b, j)),
            pl.BlockSpec((None, t, HEAD_DIM),
                         lambda b, j, i: (OFF_CV // TN_MAIN, b, (OFF_CV % TN_MAIN) // HEAD_DIM + j)),
        ],
        out_specs=pl.BlockSpec((tq, grp_cols), lambda b, j, i: (b * nq + i, j)),
        out_shape=jax.ShapeDtypeStruct((n, C_Q), bf16),
        scratch_shapes=[
            pltpu.VMEM((GQA_GROUP, rows, tk), f32),
            pltpu.VMEM((GQA_GROUP, rows, tk), bf16),
            pltpu.VMEM((rows, LANES), f32),
            pltpu.VMEM((rows, LANES), f32),
            pltpu.VMEM((rows, HEAD_DIM), f32),
        ],
        compiler_params=_cparams(("parallel", "parallel", "parallel")),
    )(qc, kc, main)


def _merge_kernel(oa_ref, ob0_ref, ob1_ref, ob2_ref, l0_ref, l1_ref, l2_ref, oc_ref,
                  ga_ref, gb_ref, gc_ref, wa_ref, wb_ref, wc_ref, o_ref):
    heads = []
    for h in range(DIL_HEADS):
        l0, l1, l2 = l0_ref[h], l1_ref[h], l2_ref[h]
        m = jnp.maximum(jnp.maximum(l0, l1), l2)
        e0, e1, e2 = jnp.exp(l0 - m), jnp.exp(l1 - m), jnp.exp(l2 - m)
        mix = (e0 * ob0_ref[h].astype(f32) + e1 * ob1_ref[h].astype(f32)
               + e2 * ob2_ref[h].astype(f32)) / (e0 + e1 + e2)
        heads.append(mix.astype(bf16))
    o_b = jnp.concatenate(heads, axis=1)
    up_a = jnp.dot(oa_ref[...], wa_ref[...], preferred_element_type=f32)
    up_b = jnp.dot(o_b, wb_ref[...], preferred_element_type=f32)
    up_c = jnp.dot(oc_ref[...], wc_ref[...], preferred_element_type=f32)
    def gate(ref):
        logits = jnp.concatenate([ref[c] for c in range(ref.shape[0])], axis=1)
        return jax.nn.sigmoid(logits.astype(f32))

    merged = gate(ga_ref) * up_a + gate(gb_ref) * up_b + gate(gc_ref) * up_c
    o_ref[...] = merged.astype(o_ref.dtype)


def _merge(o_a, o_bs, lses, o_c, main, w_up_a, w_up_b, w_up_c, layer):
    n = o_a.shape[0]
    tm = TM_TOK
    tiles_per_gate = D_MODEL // TN_MAIN
    hspec = pl.BlockSpec((DIL_HEADS, tm, HEAD_DIM), lambda i: (0, i, 0))
    gspec = lambda which: pl.BlockSpec((tiles_per_gate, tm, TN_MAIN),
                                       lambda i: (OFF_GATE // D_MODEL + which, i, 0))
    wspec = lambda rows: pl.BlockSpec((None, rows, D_MODEL), lambda i: (layer, 0, 0))
    return pl.pallas_call(
        _merge_kernel,
        grid=(n // tm,),
        in_specs=[
            pl.BlockSpec((tm, A_V), lambda i: (i, 0)),
            hspec, hspec, hspec, hspec, hspec, hspec,
            pl.BlockSpec((tm, C_Q), lambda i: (i, 0)),
            gspec(0), gspec(1), gspec(2),
            wspec(A_V), wspec(B_OUT), wspec(C_Q),
        ],
        out_specs=pl.BlockSpec((tm, D_MODEL), lambda i: (i, 0)),
        out_shape=jax.ShapeDtypeStruct((n, D_MODEL), bf16),
        compiler_params=_cparams(("parallel",)),
    )(o_a, o_bs[0], o_bs[1], o_bs[2], lses[0], lses[1], lses[2], o_c,
      main, main, main, w_up_a, w_up_b, w_up_c)


def _layer_norm_rows(y, g, b):
    mu = jnp.mean(y, axis=-1, keepdims=True)
    d = y - mu
    var = jnp.mean(d * d, axis=-1, keepdims=True)
    return d * lax.rsqrt(var + LN_EPS) * g + b


def _route_rows(logits_t, bias_col):
    scores = jax.nn.sigmoid(logits_t)
    sel = scores + bias_col
    rows = [sel[e:e + 1, :] for e in range(N_EXPERTS)]
    srow = [scores[e:e + 1, :] for e in range(N_EXPERTS)]
    best_val = None
    best_grp = None
    for g in range(N_EXPERT_GROUPS):
        mem = rows[g * EXPERTS_PER_GROUP:(g + 1) * EXPERTS_PER_GROUP]
        top2 = None
        for a in range(EXPERTS_PER_GROUP):
            for b in range(a + 1, EXPERTS_PER_GROUP):
                pair = mem[a] + mem[b]
                top2 = pair if top2 is None else jnp.maximum(top2, pair)
        if g == 0:
            best_val, best_grp = top2, jnp.zeros_like(top2, dtype=jnp.int32)
        else:
            upd = top2 > best_val
            best_val = jnp.where(upd, top2, best_val)
            best_grp = jnp.where(upd, g, best_grp)
    neg = jnp.full_like(best_val, -jnp.inf)
    cand = [jnp.where(best_grp == (e // EXPERTS_PER_GROUP), rows[e], neg) for e in range(N_EXPERTS)]

    def arg_top(vals):
        bv, bi = vals[0], jnp.zeros_like(best_grp)
        for e in range(1, N_EXPERTS):
            upd = vals[e] > bv
            bv = jnp.where(upd, vals[e], bv)
            bi = jnp.where(upd, e, bi)
        return bi

    idx1 = arg_top(cand)
    idx2 = arg_top([jnp.where(idx1 == e, neg, cand[e]) for e in range(N_EXPERTS)])
    zero = jnp.zeros_like(best_val)
    s1 = zero
    s2 = zero
    for e in range(N_EXPERTS):
        s1 = s1 + jnp.where(idx1 == e, srow[e], zero)
        s2 = s2 + jnp.where(idx2 == e, srow[e], zero)
    tot = s1 + s2
    return idx1, idx2, s1 / tot, s2 / tot


def _outproj_kernel(m_ref, w_ref, x_ref, g_ref, b_ref, wr_ref, x1_ref, x1b_ref, lg_ref):
    h = jnp.dot(m_ref[...], w_ref[...], preferred_element_type=f32)
    x1 = _layer_norm_rows(DEEPNORM_ALPHA * x_ref[...] + h, g_ref[...], b_ref[...])
    x1_ref[...] = x1
    x_hi = x1.astype(bf16)
    x1b_ref[...] = x_hi
    x_lo = (x1 - x_hi.astype(f32)).astype(bf16)
    wr = wr_ref[...]
    w_hi = wr.astype(bf16)
    w_lo = (wr - w_hi.astype(f32)).astype(bf16)
    lg_ref[...] = (jnp.dot(x_hi, w_hi, preferred_element_type=f32)
                   + jnp.dot(x_hi, w_lo, preferred_element_type=f32)
                   + jnp.dot(x_lo, w_hi, preferred_element_type=f32))


def _outproj_ln(merged, w_out, x, ln_g, ln_b, w_router_pad, layer):
    n = x.shape[0]
    tm = TM_TOK
    vspec = pl.BlockSpec((None, 1, D_MODEL), lambda i: (layer, 0, 0))
    row = pl.BlockSpec((tm, D_MODEL), lambda i: (i, 0))
    return pl.pallas_call(
        _outproj_kernel,
        grid=(n // tm,),
        in_specs=[
            row,
            pl.BlockSpec((None, D_MODEL, D_MODEL), lambda i: (layer, 0, 0)),
            row,
            vspec, vspec,
            pl.BlockSpec((D_MODEL, LANES), lambda i: (0, 0)),
        ],
        out_specs=[row, row, pl.BlockSpec((tm, LANES), lambda i: (i, 0))],
        out_shape=[
            jax.ShapeDtypeStruct((n, D_MODEL), f32),
            jax.ShapeDtypeStruct((n, D_MODEL), bf16),
            jax.ShapeDtypeStruct((n, LANES), f32),
        ],
        compiler_params=_cparams(("parallel",)),
    )(merged, w_out, x, ln_g, ln_b, w_router_pad)


def _route_kernel(lg_ref, rb_ref, idx_ref, w1_ref, w2_ref):
    logits_t = lg_ref[...].T[0:N_EXPERTS, :]
    i1, i2, w1, w2 = _route_rows(logits_t, rb_ref[...])
    idx_ref[0:1, :] = i1
    idx_ref[1:2, :] = i2
    tr = logits_t.shape[1]
    w1_ref[...] = jnp.broadcast_to(w1, (LANES, tr)).T
    w2_ref[...] = jnp.broadcast_to(w2, (LANES, tr)).T


def _route(logits, router_bias_col):
    n = logits.shape[0]
    tr = TM_ROUTE
    return pl.pallas_call(
        _route_kernel,
        grid=(n // tr,),
        in_specs=[
            pl.BlockSpec((tr, LANES), lambda i: (i, 0)),
            pl.BlockSpec((N_EXPERTS, 1), lambda i: (0, 0)),
        ],
        out_specs=[
            pl.BlockSpec((2, tr), lambda i: (0, i)),
            pl.BlockSpec((tr, LANES), lambda i: (i, 0)),
            pl.BlockSpec((tr, LANES), lambda i: (i, 0)),
        ],
        out_shape=[
            jax.ShapeDtypeStruct((2, n), jnp.int32),
            jax.ShapeDtypeStruct((n, LANES), f32),
            jax.ShapeDtypeStruct((n, LANES), f32),
        ],
        compiler_params=_cparams(("parallel",)),
    )(logits, router_bias_col)


def _moe_kernel(te_ref, nt_ref, *refs, tiles_per_piece):
    del te_ref
    x_refs, (wg_ref, wu_ref, wd_ref, o_ref) = refs[:-4], refs[-4:]
    tile = pl.program_id(0)

    @pl.when(tile < nt_ref[0])
    def _():
        x = x_refs[0][...]
        for p in range(1, len(x_refs)):
            x = jnp.where(tile >= p * tiles_per_piece, x_refs[p][...], x)
        gate = jnp.dot(x, wg_ref[...], preferred_element_type=f32)
        up = jnp.dot(x, wu_ref[...], preferred_element_type=f32)
        h = (gate * jax.nn.sigmoid(gate) * up).astype(bf16)
        o_ref[...] = jnp.dot(h, wd_ref[...], preferred_element_type=f32).astype(o_ref.dtype)

    @pl.when(tile >= nt_ref[0])
    def _():
        o_ref[...] = jnp.zeros_like(o_ref)


def _moe_grouped(x_pieces, tile_expert, n_tiles_used, w_gate, w_up, w_down, layer):
    tpp = x_pieces[0].shape[0] // TM_MOE
    n_tiles = tpp * len(x_pieces)
    wspec_in = pl.BlockSpec((None, None, D_MODEL, D_FF_EXPERT), lambda t, te, nt: (layer, te[t], 0, 0))
    wspec_out = pl.BlockSpec((None, None, D_FF_EXPERT, D_MODEL), lambda t, te, nt: (layer, te[t], 0, 0))

    def piece_spec(p):
        return pl.BlockSpec((TM_MOE, D_MODEL),
                            lambda t, te, nt: (jnp.clip(t - p * tpp, 0, tpp - 1), 0))

    grid_spec = pltpu.PrefetchScalarGridSpec(
        num_scalar_prefetch=2,
        grid=(n_tiles,),
        in_specs=[piece_spec(p) for p in range(len(x_pieces))] + [wspec_in, wspec_in, wspec_out],
        out_specs=pl.BlockSpec((TM_MOE, D_MODEL), lambda t, te, nt: (t, 0)),
    )
    return pl.pallas_call(
        functools.partial(_moe_kernel, tiles_per_piece=tpp),
        grid_spec=grid_spec,
        out_shape=jax.ShapeDtypeStruct((n_tiles * TM_MOE, D_MODEL), bf16),
        compiler_params=_cparams(("arbitrary",)),
    )(tile_expert, n_tiles_used, *x_pieces, w_gate, w_up, w_down)


def _combine_kernel(x_ref, *refs, n_pieces, tiles_per_piece):
    y1_refs, y2_refs = refs[:n_pieces], refs[n_pieces:2 * n_pieces]
    w1_ref, w2_ref, g_ref, b_ref, o_ref, ob_ref = refs[2 * n_pieces:]
    tile = pl.program_id(0)

    def pick(piece_refs):
        y = piece_refs[0][...]
        for p in range(1, n_pieces):
            y = jnp.where(tile >= p * tiles_per_piece, piece_refs[p][...], y)
        return y

    y1 = pick(y1_refs)
    y2 = pick(y2_refs)
    w1 = w1_ref[...]
    w2 = w2_ref[...]
    for c in range(D_MODEL // LANES):
        ls = slice(c * LANES, (c + 1) * LANES)
        o_ref[:, ls] = (DEEPNORM_ALPHA * x_ref[:, ls] + w1 * y1[:, ls].astype(f32)
                        + w2 * y2[:, ls].astype(f32))
    x2 = _layer_norm_rows(o_ref[...], g_ref[...], b_ref[...])
    o_ref[...] = x2
    ob_ref[...] = x2.astype(bf16)


def _combine_ln(x1, y1_pieces, y2_pieces, w1b, w2b, ln_g, ln_b, layer):
    n = x1.shape[0]
    tm = TM_COMBINE
    n_pieces = len(y1_pieces)
    tpp = y1_pieces[0].shape[0] // tm
    row = pl.BlockSpec((tm, D_MODEL), lambda i: (i, 0))
    wsp = pl.BlockSpec((tm, LANES), lambda i: (i, 0))
    vspec = pl.BlockSpec((None, 1, D_MODEL), lambda i: (layer, 0, 0))

    def piece_spec(p):
        return pl.BlockSpec((tm, D_MODEL), lambda i: (jnp.clip(i - p * tpp, 0, tpp - 1), 0))

    pieces = [piece_spec(p) for p in range(n_pieces)]
    return pl.pallas_call(
        functools.partial(_combine_kernel, n_pieces=n_pieces, tiles_per_piece=tpp),
        grid=(n // tm,),
        in_specs=[row] + pieces + pieces + [wsp, wsp, vspec, vspec],
        out_specs=[row, row],
        out_shape=[jax.ShapeDtypeStruct((n, D_MODEL), f32), jax.ShapeDtypeStruct((n, D_MODEL), bf16)],
        compiler_params=_cparams(("parallel",)),
    )(x1, *y1_pieces, *y2_pieces, w1b, w2b, ln_g, ln_b)


def _dispatch_plan(idx):
    n = idx.shape[1]
    m_pad = 2 * n + N_EXPERTS * TM_MOE
    n_tiles = m_pad // TM_MOE
    e_flat = idx.reshape(-1)
    onehot = (e_flat[:, None] == jnp.arange(N_EXPERTS, dtype=jnp.int32)[None, :]).astype(jnp.int32)
    rank = jnp.sum((jnp.cumsum(onehot, axis=0) - onehot) * onehot, axis=1)
    counts = jnp.sum(onehot, axis=0)
    tiles_per = (counts + TM_MOE - 1) // TM_MOE
    tile_end = jnp.cumsum(tiles_per)
    start_row = (tile_end - tiles_per) * TM_MOE
    dest = jnp.sum(onehot * start_row[None, :], axis=1) + rank
    tok = jnp.tile(jnp.arange(n, dtype=jnp.int32), 2)
    src = jnp.zeros((m_pad,), jnp.int32).at[dest].set(tok, mode="promise_in_bounds", unique_indices=True)
    n_used = tile_end[-1]
    tile_ids = jnp.minimum(jnp.arange(n_tiles, dtype=jnp.int32), n_used - 1)
    tile_expert = jnp.sum((tile_ids[:, None] >= tile_end[None, :]).astype(jnp.int32), axis=1)
    tile_expert = jnp.minimum(tile_expert, N_EXPERTS - 1)
    return src, dest, tile_expert, n_used.reshape(1).astype(jnp.int32)


def _rope_tables(pos, dim):
    inv_freq = ROPE_THETA ** (-jnp.arange(0, dim, 2, dtype=f32) / dim)
    ang = pos.astype(f32)[:, None] * inv_freq[None, :]
    return jnp.cos(ang), jnp.sin(ang)


def _tables(t):
    cos, sin = _rope_tables(jnp.arange(t), HEAD_DIM)
    cos_b = jnp.concatenate([cos, cos], axis=-1)
    sin_b = jnp.concatenate([-sin, sin], axis=-1)
    tok = jnp.arange(t)
    cr, sr = _rope_tables(tok // GRID_W, HEAD_DIM // 2)
    cc, sc = _rope_tables(tok % GRID_W, HEAD_DIM // 2)
    cos_c = jnp.concatenate([cr, cr, cc, cc], axis=-1)
    sin_c = jnp.concatenate([-sr, sr, -sc, sc], axis=-1)
    return cos_b, sin_b, cos_c, sin_c


def _repack_w_in(w_in):
    offs = [0]
    for s in IN_SPLITS:
        offs.append(offs[-1] + s)
    seg = [w_in[:, :, offs[i]:offs[i + 1]] for i in range(len(IN_SPLITS))]
    a_q, a_k, a_v, a_r, a_lr, b_qkv, c_q, c_k, c_v, gate = seg
    lr_pad = jnp.pad(a_lr, ((0, 0), (0, 0), (0, LR_PAD - A_LR)))
    w_main = jnp.concatenate([gate, a_q, a_k, a_v, a_r, c_q, c_k, c_v, lr_pad], axis=-1)
    w_main = jnp.pad(w_main, ((0, 0), (0, 0), (0, MAIN_COLS - MAIN_USED))).astype(bf16)
    return _tile_major(w_main, TN_MAIN), _tile_major(b_qkv.astype(bf16), TN_B)


def _tile_major(w, tn):
    n_layers, k, cols = w.shape
    return w.reshape(n_layers, k, cols // tn, tn).transpose(0, 2, 1, 3)


def _mixer(xb, wts, layer, bsz, t, tables):
    cos_b, sin_b, cos_c, sin_c = tables
    main = _proj_main(xb, wts["w_main"], layer)
    qkv_b = _proj_b(xb, wts["w_b"], cos_b, sin_b, layer, t)
    o_a = _gla(main, wts["gla_w2_f"], wts["gla_b_f"], wts["gla_w2_b"], wts["gla_b_b"],
               wts["gla_norm_g"], layer, bsz, t)
    o_bs, lses = [], []
    for group in range(len(DIL_CONFIGS)):
        o_g, lse_g = _dilated(qkv_b, group, bsz, t)
        o_bs.append(o_g)
        lses.append(lse_g)
    qc, kc = _c_prep(main, cos_c, sin_c, wts["q_norm_g"], wts["k_norm_g"], layer, t)
    o_c = _gqa(qc, kc, main, bsz, t)
    return _merge(o_a, o_bs, lses, o_c, main, wts["w_up_a"], wts["w_up_b"], wts["w_up_c"], layer)


def _gather_rows(a, rows):
    return a.at[rows].get(mode="promise_in_bounds")


class _Trunk:
    def __init__(self, x3, wts):
        self.bsz, self.t, _ = x3.shape
        self.wts = wts
        self.x = x3.reshape(self.bsz * self.t, D_MODEL)
        self.xb = self.x.astype(bf16)
        self.tables = _tables(self.t)

    def mix_and_route(self, layer):
        wts = self.wts
        merged = _mixer(self.xb, wts, layer, self.bsz, self.t, self.tables)
        self.x1, x1b, logits = _outproj_ln(merged, wts["w_out"], self.x, wts["ln1_g"], wts["ln1_b"],
                                           wts["w_router_pad"], layer)
        idx, self.w1b, self.w2b = _route(logits, wts["router_bias_col"])
        src, self.dest, self.tile_expert, self.n_used = _dispatch_plan(idx)
        n_tiles = src.shape[0] // TM_MOE
        max_rows = min(MOE_GATHER_ROWS, x1b.shape[0])
        n_pieces = next(d for d in range(1, n_tiles + 1)
                        if n_tiles % d == 0 and src.shape[0] // d <= max_rows)
        copies = -(-GATHER_TABLE_ROWS // x1b.shape[0])
        table = x1b if copies == 1 else jnp.concatenate([x1b] * copies, axis=0)
        self.x_pieces = [_gather_rows(table, piece) for piece in jnp.split(src, n_pieces)]

    def experts(self, layer):
        wts = self.wts
        n = self.x1.shape[0]
        y_sorted = _moe_grouped(self.x_pieces, self.tile_expert, self.n_used, wts["moe_w_gate"],
                                wts["moe_w_up"], wts["moe_w_down"], layer)
        c_pieces = -(-n // COMBINE_GATHER_ROWS)
        self.y1 = [_gather_rows(y_sorted, piece) for piece in jnp.split(self.dest[:n], c_pieces)]
        self.y2 = [_gather_rows(y_sorted, piece) for piece in jnp.split(self.dest[n:], c_pieces)]

    def combine(self, layer):
        self.x, self.xb = _combine_ln(self.x1, self.y1, self.y2, self.w1b, self.w2b,
                                      self.wts["ln2_g"], self.wts["ln2_b"], layer)

    def result(self):
        return self.x.reshape(self.bsz, self.t, D_MODEL)


def _run_trunks(inputs, wts, depth=DEPTH):
    trunks = [_Trunk(x3, wts) for x3 in inputs]
    for layer in range(depth):
        for stage in (_Trunk.mix_and_route, _Trunk.experts, _Trunk.combine):
            for trunk in trunks:
                stage(trunk, layer)
    return tuple(trunk.result() for trunk in trunks)


def _cast_kernel(x_ref, o_ref):
    o_ref[...] = x_ref[...].astype(o_ref.dtype)


def _expert_weights_bf16(w):
    n_layers, n_exp, rows, cols = w.shape
    spec = pl.BlockSpec((None, None, rows, cols), lambda i, j: (i, j, 0, 0))
    return pl.pallas_call(
        _cast_kernel,
        grid=(n_layers, n_exp),
        in_specs=[spec],
        out_specs=spec,
        out_shape=jax.ShapeDtypeStruct(w.shape, bf16),
        compiler_params=_cparams(("parallel", "parallel")),
    )(w)


def _prepare_weights(w_in, gla_w2_f, gla_b_f, gla_w2_b, gla_b_b, gla_norm_g, q_norm_g, k_norm_g,
                     w_up_a, w_up_b, w_up_c, w_out, ln1_g, ln1_b, w_router, router_bias,
                     moe_w_gate, moe_w_up, moe_w_down, ln2_g, ln2_b):
    w_main, w_b = _repack_w_in(w_in)
    row = lambda a: a.reshape(DEPTH, 1, a.shape[-1]).astype(f32)
    return {
        "w_main": w_main,
        "w_b": w_b,
        "gla_w2_f": gla_w2_f.astype(f32),
        "gla_b_f": row(gla_b_f),
        "gla_w2_b": gla_w2_b.astype(f32),
        "gla_b_b": row(gla_b_b),
        "gla_norm_g": row(gla_norm_g),
        "q_norm_g": row(q_norm_g),
        "k_norm_g": row(k_norm_g),
        "w_up_a": w_up_a.astype(bf16),
        "w_up_b": w_up_b.astype(bf16),
        "w_up_c": w_up_c.astype(bf16),
        "w_out": w_out.astype(bf16),
        "ln1_g": row(ln1_g),
        "ln1_b": row(ln1_b),
        "w_router_pad": jnp.pad(w_router.astype(f32), ((0, 0), (0, LANES - N_EXPERTS))),
        "router_bias_col": router_bias.astype(f32).reshape(N_EXPERTS, 1),
        "moe_w_gate": _expert_weights_bf16(moe_w_gate.astype(f32)),
        "moe_w_up": _expert_weights_bf16(moe_w_up.astype(f32)),
        "moe_w_down": _expert_weights_bf16(moe_w_down.astype(f32)),
        "ln2_g": row(ln2_g),
        "ln2_b": row(ln2_b),
    }


def kernel(x_prompt, x_sample, w_in, gla_w2_f, gla_b_f, gla_w2_b, gla_b_b, gla_norm_g, q_norm_g, k_norm_g, w_up_a, w_up_b, w_up_c, w_out, ln1_g, ln1_b, w_router, router_bias, moe_w_gate, moe_w_up, moe_w_down, ln2_g, ln2_b):
    wts = _prepare_weights(w_in, gla_w2_f, gla_b_f, gla_w2_b, gla_b_b, gla_norm_g, q_norm_g, k_norm_g,
                           w_up_a, w_up_b, w_up_c, w_out, ln1_g, ln1_b, w_router, router_bias,
                           moe_w_gate, moe_w_up, moe_w_down, ln2_g, ln2_b)
    return _run_trunks((x_prompt, x_sample), wts)
```
